```python
import jax
import jax.numpy as jnp
from jax import lax
import numpy as np

D_MODEL = 1024
BATCH = 8
SEQ = 4096
DEPTH = 4

GRID_W = 64
CTX_LEN = 256
N_MIXERS = 3
N_MOD = 9
D_FF = 2816
HEAD_DIM = 64
N_HEADS = D_MODEL // HEAD_DIM
N_KV_HEADS = 4
GROUP = N_HEADS // N_KV_HEADS
D_KV = N_KV_HEADS * HEAD_DIM
Q_BLOCK = 128
WINDOW = 128
ROPE_THETA = 10000.0
N_FREQ = HEAD_DIM // 4
ATTN_SCALE = HEAD_DIM ** -0.5
RWKV_HEAD = 64
RWKV_HEADS = D_MODEL // RWKV_HEAD
DECAY_LORA = 64
ICLR_LORA = 64
VALUE_LORA = 32
GATE_LORA = 160
NORM_EPS = 1e-6
GN_EPS = 64e-5
L2_EPS = 1e-12
NEG_INF = -1e30
N_RWKV = (DEPTH + 2) // N_MIXERS
N_GLOBAL = (DEPTH + 1) // N_MIXERS
N_LOCAL = DEPTH // N_MIXERS

kernel_name = 'hybrid_rwkv7_gqa_swa_macaron_dit'


def rms_norm(x, g):
    xf = x.astype(jnp.float32)
    y = xf * lax.rsqrt(jnp.mean(xf * xf, axis=-1, keepdims=True) + NORM_EPS)
    return (y * g.astype(jnp.float32)).astype(x.dtype)


def ffn_half(h, shift, scale, gate, g_pre, g_post, w1, w3, w2):
    n = rms_norm(h, g_pre) * (1 + scale) + shift
    y = (jax.nn.silu(n @ w1) * (n @ w3)) @ w2
    return h + 0.5 * gate * rms_norm(y, g_post)


def axial_rope(rows):
    row = jnp.broadcast_to(jnp.arange(rows)[:, None], (rows, GRID_W)).reshape(-1)
    col = jnp.broadcast_to(jnp.arange(GRID_W)[None, :], (rows, GRID_W)).reshape(-1)
    inv_freq = ROPE_THETA ** (-jnp.arange(N_FREQ, dtype=jnp.float32) / N_FREQ)
    ang = jnp.stack([row, col], axis=-1).astype(jnp.float32)[:, :, None] * inv_freq
    return jnp.cos(ang), jnp.sin(ang)


def apply_rope(t, cos, sin):
    shp = t.shape
    tf = t.astype(jnp.float32).reshape(shp[:-1] + (2, 2, N_FREQ))
    t1, t2 = tf[..., 0, :], tf[..., 1, :]
    bshape = (cos.shape[0],) + (1,) * (t.ndim - 3) + (2, N_FREQ)
    c, s = cos.reshape(bshape), sin.reshape(bshape)
    out = jnp.stack([t1 * c - t2 * s, t2 * c + t1 * s], axis=-2)
    return out.reshape(shp).astype(t.dtype)


def gqa_project(n, wq, wk, wv):
    bn, ln, _ = n.shape
    q = (n @ wq).reshape(bn, ln, N_KV_HEADS, GROUP, HEAD_DIM)
    k = (n @ wk).reshape(bn, ln, N_KV_HEADS, HEAD_DIM)
    v = (n @ wv).reshape(bn, ln, N_KV_HEADS, HEAD_DIM)
    return q, k, v


def softmax_attend(q, k, v):
    s = jnp.einsum('bqkgd,bskd->bkgqs', q, k, preferred_element_type=jnp.float32) * ATTN_SCALE
    p = jax.nn.softmax(s, axis=-1).astype(v.dtype)
    return jnp.einsum('bkgqs,bskd->bqkgd', p, v)


def sink_softmax(s, sink_h):
    col = jnp.broadcast_to(sink_h[None, :, :, None, None], s.shape[:-1] + (1,))
    return jax.nn.softmax(jnp.concatenate([s, col], axis=-1), axis=-1)[..., :-1]


def global_attention(n_lat, n_ctx, wq, wk, wv, wo, gq, gk, cos, sin, with_ctx_out):
    bn, ln, _ = n_lat.shape
    q_l, k_l, v_l = gqa_project(n_lat, wq, wk, wv)
    q_c, k_c, v_c = gqa_project(n_ctx, wq, wk, wv)
    q_l = apply_rope(rms_norm(q_l, gq), cos, sin)
    k_l = apply_rope(rms_norm(k_l, gk), cos, sin)
    q_c, k_c = rms_norm(q_c, gq), rms_norm(k_c, gk)
    k_all = jnp.concatenate([k_l, k_c], axis=1)
    v_all = jnp.concatenate([v_l, v_c], axis=1)
    nb = ln // Q_BLOCK
    qb = jnp.moveaxis(q_l.reshape(bn, nb, Q_BLOCK, N_KV_HEADS, GROUP, HEAD_DIM), 1, 0)
    o_l = lax.map(lambda qblk: softmax_attend(qblk, k_all, v_all), qb)
    y_lat = jnp.moveaxis(o_l, 0, 1).reshape(bn, ln, D_MODEL) @ wo
    y_ctx = None
    if with_ctx_out:
        y_ctx = softmax_attend(q_c, k_c, v_c).reshape(n_ctx.shape[0], n_ctx.shape[1], D_MODEL) @ wo
    return y_lat, y_ctx


def window_attention(n_lat, n_ctx, wq, wk, wv, wo, sink, cos, sin, with_ctx_out):
    bn, ln, _ = n_lat.shape
    q_l, k_l, v_l = gqa_project(n_lat, wq, wk, wv)
    q_c, k_c, v_c = gqa_project(n_ctx, wq, wk, wv)
    q_l, k_l = apply_rope(q_l, cos, sin), apply_rope(k_l, cos, sin)
    sink_h = sink.reshape(N_KV_HEADS, GROUP).astype(jnp.float32)
    nb = ln // Q_BLOCK

    def band(t):
        tp = jnp.pad(t.reshape(bn, nb, Q_BLOCK, N_KV_HEADS, HEAD_DIM), ((0, 0), (1, 1), (0, 0), (0, 0), (0, 0)))
        return jnp.moveaxis(jnp.concatenate([tp[:, :-2], tp[:, 1:-1], tp[:, 2:]], axis=2), 1, 0)

    qb = jnp.moveaxis(q_l.reshape(bn, nb, Q_BLOCK, N_KV_HEADS, GROUP, HEAD_DIM), 1, 0)
    kb, vb = band(k_l), band(v_l)
    blk = jnp.arange(nb)
    q_pos = blk[:, None] * Q_BLOCK + jnp.arange(Q_BLOCK)[None, :]
    k_pos = (blk[:, None] - 1) * Q_BLOCK + jnp.arange(3 * Q_BLOCK)[None, :]
    rel = k_pos[:, None, :] - q_pos[:, :, None]
    mask = (jnp.abs(rel) <= WINDOW) & (k_pos[:, None, :] >= 0) & (k_pos[:, None, :] < ln)

    def attend(args):
        qblk, kblk, vblk, m = args
        s_loc = jnp.einsum('bqkgd,bskd->bkgqs', qblk, kblk, preferred_element_type=jnp.float32) * ATTN_SCALE
        s_loc = jnp.where(m, s_loc, NEG_INF)
        s_ctx = jnp.einsum('bqkgd,bskd->bkgqs', qblk, k_c, preferred_element_type=jnp.float32) * ATTN_SCALE
        p = sink_softmax(jnp.concatenate([s_loc, s_ctx], axis=-1), sink_h).astype(vblk.dtype)
        n_loc = kblk.shape[1]
        return (jnp.einsum('bkgqs,bskd->bqkgd', p[..., :n_loc], vblk)
                + jnp.einsum('bkgqs,bskd->bqkgd', p[..., n_loc:], v_c))

    o_l = lax.map(attend, (qb, kb, vb, mask))
    y_lat = jnp.moveaxis(o_l, 0, 1).reshape(bn, ln, D_MODEL) @ wo
    y_ctx = None
    if with_ctx_out:
        s = jnp.einsum('bqkgd,bskd->bkgqs', q_c, k_c, preferred_element_type=jnp.float32) * ATTN_SCALE
        p = sink_softmax(s, sink_h).astype(v_c.dtype)
        o_c = jnp.einsum('bkgqs,bskd->bqkgd', p, v_c)
        y_ctx = o_c.reshape(n_ctx.shape[0], n_ctx.shape[1], D_MODEL) @ wo
    return y_lat, y_ctx


def to_heads(t):
    return t.reshape(t.shape[0], t.shape[1], RWKV_HEADS, RWKV_HEAD)


def centred_shift(x):
    xp = jnp.pad(x, ((0, 0), (1, 1), (0, 0)))
    return 0.5 * (xp[:, :-2] + xp[:, 2:]) - x


def rwkv_features(n, mu, wr, wk, wv, k_k, vres):
    xx = centred_shift(n)
    xr, xw, xk, xv, xa, xg = (n + xx * mu[m] for m in range(6))
    r = xr @ wr
    k = xk @ wk
    v = xv @ wv
    if vres is not None:
        v_first, v0, v1, v2 = vres
        v = v + (v_first - v) * jax.nn.sigmoid(v0 + (xv @ v1) @ v2)
    kk = to_heads(k * k_k).astype(jnp.float32)
    kk = kk / jnp.maximum(jnp.sqrt(jnp.sum(kk * kk, axis=-1, keepdims=True)), L2_EPS)
    return {'r': r, 'k': k, 'v': v, 'kk': kk, 'xw': xw, 'xa': xa, 'xg': xg}


def rwkv_direction(f, k_a, w0, w1, w2, a0, a1, a2, g1, g2):
    w = -jax.nn.softplus(-(w0 + jnp.tanh(f['xw'] @ w1) @ w2)) - 0.5
    a = jax.nn.sigmoid(a0 + (f['xa'] @ a1) @ a2)
    k = f['k'] * (1 + (a - 1) * k_a)
    g = jax.nn.sigmoid(f['xg'] @ g1) @ g2
    a_h = to_heads(a).astype(jnp.float32)
    return {'r': to_heads(f['r']).astype(jnp.float32),
            'decay': jnp.exp(-jnp.exp(to_heads(w).astype(jnp.float32))),
            'k': to_heads(k).astype(jnp.float32),
            'v': to_heads(f['v']).astype(jnp.float32),
            'a': -f['kk'], 'b': f['kk'] * a_h, 'g': g}


def rwkv_scan(state0, dd, reverse):
    def step(s, inp):
        r_t, w_t, k_t, v_t, a_t, b_t = inp
        sa = jnp.einsum('bhvk,bhk->bhv', s, a_t)
        s = s * w_t[:, :, None, :] + sa[..., None] * b_t[:, :, None, :] + v_t[..., None] * k_t[:, :, None, :]
        return s, jnp.einsum('bhvk,bhk->bhv', s, r_t)
    xs = tuple(jnp.moveaxis(dd[nm], 1, 0) for nm in ('r', 'decay', 'k', 'v', 'a', 'b'))
    s_fin, ys = lax.scan(step, state0, xs, reverse=reverse)
    return s_fin, jnp.moveaxis(ys, 0, 1)


def rwkv_readout(y, dd, r_k, ln_w, ln_b):
    mu = jnp.mean(y, axis=-1, keepdims=True)
    var = jnp.mean(jnp.square(y - mu), axis=-1, keepdims=True)
    flat = y.shape[:2] + (D_MODEL,)
    yn = ((y - mu) * lax.rsqrt(var + GN_EPS)).reshape(flat) * ln_w.astype(jnp.float32) + ln_b.astype(jnp.float32)
    bonus = (jnp.sum(dd['r'] * dd['k'] * r_k.astype(jnp.float32), axis=-1, keepdims=True) * dd['v']).reshape(flat)
    return (yn + bonus) * dd['g'].astype(jnp.float32)


def rwkv_mixer(n_lat, n_ctx, p, vres_lat, vres_ctx, with_ctx_out):
    f_lat = rwkv_features(n_lat, p['mu'], p['wr'], p['wk'], p['wv'], p['k_k'], vres_lat)
    f_ctx = rwkv_features(n_ctx, p['mu'], p['wr'], p['wk'], p['wv'], p['k_k'], vres_ctx)
    s0 = jnp.zeros((n_lat.shape[0], RWKV_HEADS, RWKV_HEAD, RWKV_HEAD), jnp.float32)
    o_lat, o_ctx = [], []
    for d in range(2):
        dp = [p[nm][d] for nm in ('w0', 'w1', 'w2', 'a0', 'a1', 'a2', 'g1', 'g2')]
        d_lat = rwkv_direction(f_lat, p['k_a'], *dp)
        d_ctx = rwkv_direction(f_ctx, p['k_a'], *dp)
        s_ctx, y_ctx = rwkv_scan(s0, d_ctx, d == 1)
        _, y_lat = rwkv_scan(s_ctx, d_lat, d == 1)
        o_lat.append(rwkv_readout(y_lat, d_lat, p['r_k'], p['ln_w'][d], p['ln_b'][d]))
        if with_ctx_out:
            o_ctx.append(rwkv_readout(y_ctx, d_ctx, p['r_k'], p['ln_w'][d], p['ln_b'][d]))
    y_lat = (o_lat[0] + o_lat[1]).astype(n_lat.dtype) @ p['wo']
    y_ctx = (o_ctx[0] + o_ctx[1]).astype(n_ctx.dtype) @ p['wo'] if with_ctx_out else None
    return y_lat, y_ctx, f_lat['v'], f_ctx['v']


def setup_inputs(seed: int = 0) -> dict:
    key = jax.random.key(seed)
    ks = iter(jax.random.split(key, 64))

    def nrm(shape, scale=1.0):
        return scale * jax.random.normal(next(ks), shape, jnp.float32)

    d, f = D_MODEL, D_FF
    decay_base = jnp.linspace(-6.5, -1.5, d, dtype=jnp.float32)
    return {
        'x': nrm((BATCH, SEQ, d)),
        'c': nrm((BATCH, d)),
        'ctx': nrm((BATCH, CTX_LEN, d)),
        'c_ctx': nrm((d,)),
        'mod_w': nrm((DEPTH, d, N_MOD * d), 0.5 * d ** -0.5),
        'mod_b': nrm((DEPTH, N_MOD * d), 0.02),
        'norm_g': 1.0 + nrm((DEPTH, 6, d), 0.05),
        'ffn_w1': nrm((DEPTH, 2, d, f), d ** -0.5),
        'ffn_w3': nrm((DEPTH, 2, d, f), d ** -0.5),
        'ffn_w2': nrm((DEPTH, 2, f, d), f ** -0.5),
        'rwkv_mu': jax.random.uniform(next(ks), (N_RWKV, 6, d), jnp.float32),
        'rwkv_wr': nrm((N_RWKV, d, d), d ** -0.5),
        'rwkv_wk': nrm((N_RWKV, d, d), d ** -0.5),
        'rwkv_wv': nrm((N_RWKV, d, d), d ** -0.5),
        'rwkv_wo': nrm((N_RWKV, d, d), d ** -0.5),
        'rwkv_k_k': 0.85 + nrm((N_RWKV, d), 0.05),
        'rwkv_k_a': 1.0 + nrm((N_RWKV, d), 0.05),
        'rwkv_r_k': nrm((N_RWKV, RWKV_HEADS, RWKV_HEAD), 0.1),
        'rwkv_w0': decay_base + nrm((N_RWKV, 2, d), 0.1),
        'rwkv_w1': nrm((N_RWKV, 2, d, DECAY_LORA), 0.1 * d ** -0.5),
        'rwkv_w2': nrm((N_RWKV, 2, DECAY_LORA, d), 0.1 * DECAY_LORA ** -0.5),
        'rwkv_a0': nrm((N_RWKV, 2, d), 0.1),
        'rwkv_a1': nrm((N_RWKV, 2, d, ICLR_LORA), 0.1 * d ** -0.5),
        'rwkv_a2': nrm((N_RWKV, 2, ICLR_LORA, d), 0.1 * ICLR_LORA ** -0.5),
        'rwkv_g1': nrm((N_RWKV, 2, d, GATE_LORA), d ** -0.5),
        'rwkv_g2': nrm((N_RWKV, 2, GATE_LORA, d), GATE_LORA ** -0.5),
        'rwkv_ln_w': 1.0 + nrm((N_RWKV, 2, d), 0.05),
        'rwkv_ln_b': nrm((N_RWKV, 2, d), 0.02),
        'rwkv_v0': 1.0 + nrm((N_RWKV - 1, d), 0.1),
        'rwkv_v1': nrm((N_RWKV - 1, d, VALUE_LORA), 0.1 * d ** -0.5),
        'rwkv_v2': nrm((N_RWKV - 1, VALUE_LORA, d), 0.1 * VALUE_LORA ** -0.5),
        'gattn_wq': nrm((N_GLOBAL, d, N_HEADS * HEAD_DIM), d ** -0.5),
        'gattn_wk': nrm((N_GLOBAL, d, D_KV), d ** -0.5),
        'gattn_wv': nrm((N_GLOBAL, d, D_KV), d ** -0.5),
        'gattn_wo': nrm((N_GLOBAL, N_HEADS * HEAD_DIM, d), (N_HEADS * HEAD_DIM) ** -0.5),
        'gattn_q_norm': 1.0 + nrm((N_GLOBAL, HEAD_DIM), 0.05),
        'gattn_k_norm': 1.0 + nrm((N_GLOBAL, HEAD_DIM), 0.05),
        'wattn_wq': nrm((N_LOCAL, d, N_HEADS * HEAD_DIM), d ** -0.5),
        'wattn_wk': nrm((N_LOCAL, d, D_KV), d ** -0.5),
        'wattn_wv': nrm((N_LOCAL, d, D_KV), d ** -0.5),
        'wattn_wo': nrm((N_LOCAL, N_HEADS * HEAD_DIM, d), (N_HEADS * HEAD_DIM) ** -0.5),
        'wattn_sink': nrm((N_LOCAL, N_HEADS), 0.5),
    }


def reference(x, c, ctx, c_ctx, mod_w, mod_b, norm_g, ffn_w1, ffn_w3, ffn_w2,
              rwkv_mu, rwkv_wr, rwkv_wk, rwkv_wv, rwkv_wo, rwkv_k_k, rwkv_k_a, rwkv_r_k,
              rwkv_w0, rwkv_w1, rwkv_w2, rwkv_a0, rwkv_a1, rwkv_a2, rwkv_g1, rwkv_g2,
              rwkv_ln_w, rwkv_ln_b, rwkv_v0, rwkv_v1, rwkv_v2,
              gattn_wq, gattn_wk, gattn_wv, gattn_wo, gattn_q_norm, gattn_k_norm,
              wattn_wq, wattn_wk, wattn_wv, wattn_wo, wattn_sink):
    rows = x.shape[1] // GRID_W
    cos, sin = axial_rope(rows)
    v_first_lat, v_first_ctx = None, None
    for i in range(DEPTH):
        last = i == DEPTH - 1
        kind, j = i % N_MIXERS, i // N_MIXERS
        g = norm_g[i]
        ml = (jax.nn.silu(c) @ mod_w[i] + mod_b[i]).reshape(-1, N_MOD, 1, D_MODEL)
        ml = [ml[:, m] for m in range(N_MOD)]
        mc = (jax.nn.silu(c_ctx) @ mod_w[i] + mod_b[i]).reshape(N_MOD, D_MODEL)
        x = ffn_half(x, ml[0], ml[1], ml[2], g[0], g[1], ffn_w1[i, 0], ffn_w3[i, 0], ffn_w2[i, 0])
        ctx = ffn_half(ctx, mc[0], mc[1], mc[2], g[0], g[1], ffn_w1[i, 0], ffn_w3[i, 0], ffn_w2[i, 0])
        n_lat = rms_norm(x, g[2]) * (1 + ml[4]) + ml[3]
        n_ctx = rms_norm(ctx, g[2]) * (1 + mc[4]) + mc[3]
        if kind == 0:
            p = {'mu': rwkv_mu[j], 'wr': rwkv_wr[j], 'wk': rwkv_wk[j], 'wv': rwkv_wv[j], 'wo': rwkv_wo[j],
                 'k_k': rwkv_k_k[j], 'k_a': rwkv_k_a[j], 'r_k': rwkv_r_k[j],
                 'w0': rwkv_w0[j], 'w1': rwkv_w1[j], 'w2': rwkv_w2[j],
                 'a0': rwkv_a0[j], 'a1': rwkv_a1[j], 'a2': rwkv_a2[j],
                 'g1': rwkv_g1[j], 'g2': rwkv_g2[j], 'ln_w': rwkv_ln_w[j], 'ln_b': rwkv_ln_b[j]}
            vres_lat, vres_ctx = None, None
            if j > 0:
                vres_lat = (v_first_lat, rwkv_v0[j - 1], rwkv_v1[j - 1], rwkv_v2[j - 1])
                vres_ctx = (v_first_ctx, rwkv_v0[j - 1], rwkv_v1[j - 1], rwkv_v2[j - 1])
            y_lat, y_ctx, v_lat, v_ctx = rwkv_mixer(n_lat, n_ctx, p, vres_lat, vres_ctx, not last)
            if j == 0:
                v_first_lat, v_first_ctx = v_lat, v_ctx
        elif kind == 1:
            y_lat, y_ctx = global_attention(n_lat, n_ctx, gattn_wq[j], gattn_wk[j], gattn_wv[j], gattn_wo[j],
                                            gattn_q_norm[j], gattn_k_norm[j], cos, sin, not last)
        else:
            y_lat, y_ctx = window_attention(n_lat, n_ctx, wattn_wq[j], wattn_wk[j], wattn_wv[j], wattn_wo[j],
                                            wattn_sink[j], cos, sin, not last)
        x = x + ml[5] * rms_norm(y_lat, g[3])
        x = ffn_half(x, ml[6], ml[7], ml[8], g[4], g[5], ffn_w1[i, 1], ffn_w3[i, 1], ffn_w2[i, 1])
        if not last:
            ctx = ctx + mc[5] * rms_norm(y_ctx, g[3])
            ctx = ffn_half(ctx, mc[6], mc[7], mc[8], g[4], g[5], ffn_w1[i, 1], ffn_w3[i, 1], ffn_w2[i, 1])
    return x
```

```python
import functools

import numpy as np
import jax
import jax.numpy as jnp
from jax import lax
from jax.experimental import pallas as pl
from jax.experimental.pallas import tpu as pltpu

F32 = jnp.float32
BF16 = jnp.bfloat16
HIGHEST = lax.Precision.HIGHEST

GRID_W = 64
N_MOD = 9
HEAD_DIM = 64
N_KV_HEADS = 4
WINDOW = 128
ROPE_THETA = 10000.0
N_FREQ = HEAD_DIM // 4
RWKV_HEAD = 64
NORM_EPS = 1e-6
GN_EPS = 64e-5
L2_EPS = 1e-12
NEG_INF = -1e30

LANES = 128
TILE = 256
CHUNK = 64
KEY_CHUNK = 512
LOCAL_KEYS = 3 * TILE
VMEM_LIMIT = 56 * 1024 * 1024


def _cparams(*sem):
    return pltpu.CompilerParams(dimension_semantics=sem, vmem_limit_bytes=VMEM_LIMIT)


def _const_spec(shape):
    nd = len(shape)
    return pl.BlockSpec(shape, lambda *_: (0,) * nd, pipeline_mode=pl.Buffered(1))


def _dot(a, b):
    return jnp.dot(a.astype(BF16), b.astype(BF16), preferred_element_type=F32)


def _dot_nt(a, b):
    return lax.dot_general(a.astype(BF16), b.astype(BF16), (((1,), (1,)), ((), ())),
                           preferred_element_type=F32)


def _dot_tn(a, b):
    return lax.dot_general(a.astype(BF16), b.astype(BF16), (((0,), (0,)), ((), ())),
                           preferred_element_type=F32)


def _dot_f32(a, b):
    return jnp.dot(a, b, precision=HIGHEST, preferred_element_type=F32)


def _sigmoid(x):
    return 1.0 / (1.0 + jnp.exp(-x))


def _rms(x, g):
    return x * lax.rsqrt(jnp.mean(x * x, axis=-1, keepdims=True) + NORM_EPS) * g


def _norm_mod(x, g, scale, shift):
    return _rms(x, g) * (1.0 + scale) + shift


def _seg_sum(x, e_ref, et_ref):
    return _dot_f32(_dot_f32(x, e_ref[...]), et_ref[...])


def _mod_body(c_ref, w_ref, b_ref, o_ref):
    c = c_ref[...]
    o_ref[0] = _dot_f32(c * _sigmoid(c), w_ref[0]) + b_ref[0]


def _modulation(c_rows, mod_w, mod_b):
    depth, d, nd = mod_w.shape
    rows = c_rows.shape[0]
    tn = nd // 6
    return pl.pallas_call(
        _mod_body,
        grid=(depth, nd // tn),
        in_specs=[_const_spec((rows, d)),
                  pl.BlockSpec((1, d, tn), lambda i, j: (i, 0, j)),
                  pl.BlockSpec((1, 1, tn), lambda i, j: (i, 0, j))],
        out_specs=pl.BlockSpec((1, rows, tn), lambda i, j: (i, 0, j)),
        out_shape=jax.ShapeDtypeStruct((depth, rows, nd), F32),
        compiler_params=_cparams("arbitrary", "arbitrary"),
        name="adaln_rows",
    )(c_rows, mod_w, mod_b.reshape(depth, 1, nd))


class _Stream:
    def __init__(self, batch, ctx_len, seq, d):
        assert ctx_len % TILE == 0 and seq % TILE == 0 and d % LANES == 0
        self.b, self.t, self.d = batch, ctx_len + seq, d
        self.ctx = ctx_len
        self.n_tiles = self.t // TILE
        self.n_ctx_tiles = ctx_len // TILE

    def tile_spec(self, width=None):
        return pl.BlockSpec((1, TILE, width or self.d), lambda b, t: (b, t, 0))

    def mod_spec(self):
        nct, ctx_row = self.n_ctx_tiles, self.b
        return pl.BlockSpec((1, N_MOD, self.d), lambda b, t: (jnp.where(t < nct, ctx_row, b), 0, 0))

    def out_shape(self, width=None, dtype=F32):
        return jax.ShapeDtypeStruct((self.b, self.t, width or self.d), dtype)


def _ffn_body(x_ref, mod_ref, g_ref, w1_ref, w3_ref, w2_ref, o_ref, *, mod_row, g_row):
    x = x_ref[0]
    m = mod_ref[0]
    g = g_ref[...]
    shift, scale, gate = m[mod_row:mod_row + 1], m[mod_row + 1:mod_row + 2], m[mod_row + 2:mod_row + 3]
    n = _norm_mod(x, g[g_row:g_row + 1], scale, shift).astype(BF16)
    a = jnp.dot(n, w1_ref[...], preferred_element_type=F32)
    b = jnp.dot(n, w3_ref[...], preferred_element_type=F32)
    h = (a * _sigmoid(a) * b).astype(BF16)
    y = jnp.dot(h, w2_ref[...], preferred_element_type=F32)
    o_ref[0] = x + 0.5 * gate * _rms(y, g[g_row + 1:g_row + 2])


def _ffn_half(st, x, mod, g, w1, w3, w2, mod_row, g_row):
    d, f = w1.shape
    return pl.pallas_call(
        functools.partial(_ffn_body, mod_row=mod_row, g_row=g_row),
        grid=(st.b, st.n_tiles),
        in_specs=[st.tile_spec(), st.mod_spec(), _const_spec(g.shape),
                  _const_spec((d, f)), _const_spec((d, f)), _const_spec((f, d))],
        out_specs=st.tile_spec(),
        out_shape=st.out_shape(),
        compiler_params=_cparams("parallel", "parallel"),
        name="swiglu_half",
    )(x, mod, g, w1, w3, w2)


def _softplus(y):
    return jnp.maximum(y, 0.0) + jnp.log(1.0 + jnp.exp(-jnp.abs(y)))


def _rwkv_feat_body(*refs, n_ctx_tiles, n_tiles, with_vres):
    (x_ref, xp_ref, xn_ref, mod_ref, g_ref, mu_ref, wr_ref, wk_ref, wv_ref, kk_ref, ka_ref,
     w0_ref, w1_ref, w2_ref, a0_ref, a1_ref, a2_ref, g1_ref, g2_ref, e_ref, et_ref) = refs[:21]
    rest = refs[21:]
    if with_vres:
        vf_ref, v0_ref, v1_ref, v2_ref = rest[:4]
        rest = rest[4:]
    r_out, v_out, kk_out, lw_out, kd_out, ab_out, g_out = rest

    t = pl.program_id(1)
    m = mod_ref[0]
    gn = g_ref[...][2:3]
    shift, scale = m[3:4], m[4:5]
    n = _norm_mod(x_ref[0], gn, scale, shift)
    first = jnp.logical_or(t == 0, t == n_ctx_tiles)
    last = jnp.logical_or(t == n_ctx_tiles - 1, t == n_tiles - 1)
    n_before = jnp.where(first, 0.0, _norm_mod(xp_ref[0][7:8], gn, scale, shift))
    n_after = jnp.where(last, 0.0, _norm_mod(xn_ref[0][0:1], gn, scale, shift))
    row = lax.broadcasted_iota(jnp.int32, (TILE, 1), 0)
    n_prev = jnp.where(row == 0, n_before, pltpu.roll(n, 1, 0))
    n_next = jnp.where(row == TILE - 1, n_after, pltpu.roll(n, TILE - 1, 0))
    xx = 0.5 * (n_prev + n_next) - n
    mu = mu_ref[...]
    xr, xw, xk, xv, xa, xg = (n + xx * mu[i:i + 1] for i in range(6))

    r = _dot(xr, wr_ref[...])
    k = _dot(xk, wk_ref[...])
    v = _dot(xv, wv_ref[...])
    if with_vres:
        lora = _dot(_dot(xv, v1_ref[...]), v2_ref[...])
        v = v + (vf_ref[0] - v) * _sigmoid(v0_ref[...] + lora)
    kk = k * kk_ref[...]
    kk = kk / jnp.maximum(jnp.sqrt(_seg_sum(kk * kk, e_ref, et_ref)), L2_EPS)
    r_out[0] = r
    v_out[0] = v
    kk_out[0] = kk
    k_a = ka_ref[...]
    for dr in range(2):
        wl = w0_ref[dr] + _dot(jnp.tanh(_dot(xw, w1_ref[dr])), w2_ref[dr])
        w = -_softplus(-wl) - 0.5
        a = _sigmoid(a0_ref[dr] + _dot(_dot(xa, a1_ref[dr]), a2_ref[dr]))
        lw_out[dr, 0] = -jnp.exp(w)
        kd_out[dr, 0] = k * (1.0 + (a - 1.0) * k_a)
        ab_out[dr, 0] = kk * a
        g_out[dr, 0] = _dot(_sigmoid(_dot(xg, g1_ref[dr])), g2_ref[dr])


def _rwkv_features(st, x, mod, g, p, seg, vres):
    d = st.d
    n8 = st.t // 8
    per8 = TILE // 8
    prev_spec = pl.BlockSpec((1, 8, d), lambda b, t: (b, jnp.maximum(t * per8 - 1, 0), 0))
    next_spec = pl.BlockSpec((1, 8, d), lambda b, t: (b, jnp.minimum((t + 1) * per8, n8 - 1), 0))
    dir_spec = pl.BlockSpec((2, 1, TILE, d), lambda b, t: (0, b, t, 0))
    dir_shape = jax.ShapeDtypeStruct((2, st.b, st.t, d), F32)
    names = ("mu", "wr", "wk", "wv", "k_k", "k_a", "w0", "w1", "w2", "a0", "a1", "a2", "g1", "g2")
    consts = [p[nm] for nm in names] + list(seg)
    args = [x, x, x, mod, g] + consts
    specs = [st.tile_spec(), prev_spec, next_spec, st.mod_spec(), _const_spec(g.shape)]
    specs += [_const_spec(c.shape) for c in consts]
    if vres is not None:
        v_first, v0, v1, v2 = vres
        args += [v_first, v0, v1, v2]
        specs += [st.tile_spec(), _const_spec(v0.shape), _const_spec(v1.shape), _const_spec(v2.shape)]
    return pl.pallas_call(
        functools.partial(_rwkv_feat_body, n_ctx_tiles=st.n_ctx_tiles, n_tiles=st.n_tiles,
                          with_vres=vres is not None),
        grid=(st.b, st.n_tiles),
        in_specs=specs,
        out_specs=[st.tile_spec()] * 3 + [dir_spec] * 4,
        out_shape=[st.out_shape()] * 3 + [dir_shape] * 4,
        compiler_params=_cparams("parallel", "parallel"),
        name="rwkv_features",
    )(*args)


def _split(x):
    hi = x.astype(BF16)
    return hi, (x - hi.astype(F32)).astype(BF16)


def _dot_split(a, b):
    (ah, al), (bh, bl) = a, b
    mm = functools.partial(jnp.dot, preferred_element_type=F32)
    return mm(ah, bh) + mm(ah, bl) + mm(al, bh)


def _unit_triangular_inverse(a, ri, ci, size):
    t = jnp.where((ri >> 1) == (ci >> 1), a, 0.0) + (ri == ci).astype(F32)
    level = 1
    while (2 << level) <= size:
        off = jnp.logical_and((ri >> (level + 1)) == (ci >> (level + 1)), (ri >> level) != (ci >> level))
        tp = _split(t)
        w = _dot_split(tp, _split(jnp.where(off, a, 0.0)))
        t = t + _dot_split(_split(w), tp)
        level += 1
    return t


def _scan_body(r_ref, lw_ref, k_ref, v_ref, kk_ref, ab_ref, y_ref, s_ref, *, reverse):
    c = CHUNK
    n2 = 2 * c

    @pl.when(pl.program_id(1) == 0)
    def _():
        s_ref[...] = jnp.zeros_like(s_ref)

    r, lw, k, v, kk, ab = r_ref[0], lw_ref[0, 0], k_ref[0, 0], v_ref[0], kk_ref[0], ab_ref[0, 0]
    ti = lax.broadcasted_iota(jnp.int32, (c, c), 0)
    si = lax.broadcasted_iota(jnp.int32, (c, c), 1)
    upto = (si >= ti) if reverse else (si <= ti)
    cum = _dot_f32(upto.astype(F32), lw)
    tot = cum[0:1] if reverse else cum[c - 1:c]
    a_t = -kk * jnp.exp(cum - lw)
    r_t = r * jnp.exp(cum)
    inv = jnp.exp(-cum)
    b_t = ab * inv
    k_t = k * inv
    to_end = jnp.exp(tot - cum)
    b_e = ab * to_end
    k_e = k * to_end
    w_all = jnp.exp(tot)

    ri = lax.broadcasted_iota(jnp.int32, (n2, n2), 0)
    ci = lax.broadcasted_iota(jnp.int32, (n2, n2), 1)
    same = (ri >> 6) == (ci >> 6)
    tt, ss = ri & (c - 1), ci & (c - 1)
    before = (ss > tt) if reverse else (ss < tt)
    strict = jnp.logical_and(same, before)
    incl = jnp.logical_and(same, jnp.logical_or(before, ss == tt))
    incl2 = jnp.concatenate([incl, incl], axis=1)
    even = lax.broadcasted_iota(jnp.int32, (1, LANES), 1) < RWKV_HEAD

    def stack(x):
        return jnp.concatenate([jnp.where(even, x, 0.0), jnp.where(even, 0.0, x)], axis=0)

    for p in range(r.shape[1] // LANES):
        sl = slice(p * LANES, (p + 1) * LANES)
        a_p, r_p, v_p = a_t[:, sl], r_t[:, sl], v[:, sl]
        s_p = s_ref[p]
        prods = _dot_nt(jnp.concatenate([stack(a_p), stack(r_p)], axis=0),
                        jnp.concatenate([stack(b_t[:, sl]), stack(k_t[:, sl])], axis=0))
        a_b = jnp.where(strict, prods[0:n2, 0:n2], 0.0)
        a_k = jnp.where(strict, prods[0:n2, n2:2 * n2], 0.0)
        r_bk = jnp.where(incl2, prods[n2:2 * n2], 0.0)
        from_state = _dot_nt(jnp.concatenate([a_p, r_p], axis=0), s_p)
        v_st = stack(v_p)
        u_st = _dot(_unit_triangular_inverse(a_b, ri, ci, c), stack(from_state[0:c]) + _dot(a_k, v_st))
        y_st = _dot(r_bk, jnp.concatenate([u_st, v_st], axis=0))
        y_ref[0, :, sl] = from_state[c:2 * c] + y_st[0:c] + y_st[c:n2]
        u = u_st[0:c] + u_st[c:n2]
        upd = _dot_tn(jnp.concatenate([u, v_p], axis=0),
                      jnp.concatenate([b_e[:, sl], k_e[:, sl]], axis=0))
        s_ref[p] = s_p * w_all[:, sl] + jnp.where(same, upd, 0.0)


def _rwkv_scan(st, r, lw, kd, v, kk, ab, direction):
    d = st.d
    nc, ncc = st.t // CHUNK, st.ctx // CHUNK
    if direction == 0:
        def cidx(s):
            return s
    else:
        def cidx(s):
            return jnp.where(s < ncc, ncc - 1 - s, nc - 1 - s + ncc)
    shared = pl.BlockSpec((1, CHUNK, d), lambda b, s: (b, cidx(s), 0))
    per_dir = pl.BlockSpec((1, 1, CHUNK, d), lambda b, s: (direction, b, cidx(s), 0))
    return pl.pallas_call(
        functools.partial(_scan_body, reverse=direction == 1),
        grid=(st.b, nc),
        in_specs=[shared, per_dir, per_dir, shared, shared, per_dir],
        out_specs=shared,
        out_shape=st.out_shape(),
        scratch_shapes=[pltpu.VMEM((d // LANES, LANES, LANES), F32)],
        compiler_params=_cparams("parallel", "arbitrary"),
        name="rwkv_scan_bwd" if direction else "rwkv_scan_fwd",
    )(r, lw, kd, v, kk, ab)


def _rwkv_out_body(x_ref, mod_ref, g_ref, yf_ref, yb_ref, r_ref, kd_ref, v_ref, gate_ref,
                   lnw_ref, lnb_ref, rk_ref, wo_ref, e_ref, et_ref, o_ref):
    r, v = r_ref[0], v_ref[0]
    inv_n = 1.0 / RWKV_HEAD
    o = None
    for dr, y_ref in enumerate((yf_ref, yb_ref)):
        y = y_ref[0]
        cen = y - _seg_sum(y, e_ref, et_ref) * inv_n
        var = _seg_sum(cen * cen, e_ref, et_ref) * inv_n
        yn = cen * lax.rsqrt(var + GN_EPS) * lnw_ref[dr] + lnb_ref[dr]
        bonus = _seg_sum(r * kd_ref[dr, 0] * rk_ref[...], e_ref, et_ref) * v
        od = (yn + bonus) * gate_ref[dr, 0]
        o = od if o is None else o + od
    y_mix = _dot(o, wo_ref[...])
    m = mod_ref[0]
    o_ref[0] = x_ref[0] + m[5:6] * _rms(y_mix, g_ref[...][3:4])


def _rwkv_out(st, x, mod, g, y_f, y_b, r, kd, v, gate, p, seg):
    d = st.d
    dir_spec = pl.BlockSpec((2, 1, TILE, d), lambda b, t: (0, b, t, 0))
    consts = [p["ln_w"], p["ln_b"], p["r_k"], p["wo"]] + list(seg)
    return pl.pallas_call(
        _rwkv_out_body,
        grid=(st.b, st.n_tiles),
        in_specs=[st.tile_spec(), st.mod_spec(), _const_spec(g.shape), st.tile_spec(), st.tile_spec(),
                  st.tile_spec(), dir_spec, st.tile_spec(), dir_spec] + [_const_spec(c.shape) for c in consts],
        out_specs=st.tile_spec(),
        out_shape=st.out_shape(),
        compiler_params=_cparams("parallel", "parallel"),
        name="rwkv_readout",
    )(x, mod, g, y_f, y_b, r, kd, v, gate, *consts)


def _qkv_body(x_ref, mod_ref, g_ref, wq_ref, wqs_ref, wk_ref, wks_ref, wv_ref, gq_ref, gk_ref,
              cos_ref, sin_ref, q_out, k_out, v_out, *, qk_norm, q_scale):
    m = mod_ref[0]
    n = _norm_mod(x_ref[0], g_ref[...][2:3], m[4:5], m[3:4]).astype(BF16)
    cos, sin = cos_ref[...], sin_ref[...]

    def rotated(w_ref, ws_ref, gain_ref, h, scale):
        t = jnp.dot(n, w_ref[h], preferred_element_type=F32)
        ts = jnp.dot(n, ws_ref[h], preferred_element_type=F32)
        if qk_norm:
            inv = lax.rsqrt(jnp.mean(t * t, axis=-1, keepdims=True) + NORM_EPS)
            gain = gain_ref[...]
            t, ts = t * inv * gain[0:1], ts * inv * gain[1:2]
        out = t * cos + ts * sin
        return out * scale if scale != 1.0 else out

    for h in range(wq_ref.shape[0]):
        q_out[0, h] = rotated(wq_ref, wqs_ref, gq_ref, h, q_scale).astype(BF16)
    for h in range(wk_ref.shape[0]):
        k_out[0, h] = rotated(wk_ref, wks_ref, gk_ref, h, 1.0).astype(BF16)
        v_out[0, h] = jnp.dot(n, wv_ref[h], preferred_element_type=F32).astype(BF16)


def _qkv_project(st, x, mod, g, a, rope, qk_norm):
    nh, nkv = a["wq"].shape[0], a["wk"].shape[0]
    consts = [a["wq"], a["wq_sw"], a["wk"], a["wk_sw"], a["wv"], a["gq"], a["gk"]]
    rope_spec = pl.BlockSpec((TILE, HEAD_DIM), lambda b, t: (t, 0))

    def head_spec(n):
        return pl.BlockSpec((1, n, TILE, HEAD_DIM), lambda b, t: (b, 0, t, 0))

    def head_shape(n):
        return jax.ShapeDtypeStruct((st.b, n, st.t, HEAD_DIM), BF16)

    return pl.pallas_call(
        functools.partial(_qkv_body, qk_norm=qk_norm, q_scale=HEAD_DIM ** -0.5),
        grid=(st.b, st.n_tiles),
        in_specs=[st.tile_spec(), st.mod_spec(), _const_spec(g.shape)]
                 + [_const_spec(c.shape) for c in consts] + [rope_spec, rope_spec],
        out_specs=[head_spec(nh), head_spec(nkv), head_spec(nkv)],
        out_shape=[head_shape(nh), head_shape(nkv), head_shape(nkv)],
        compiler_params=_cparams("parallel", "parallel"),
        name="qkv_rope",
    )(x, mod, g, *consts, *rope)


def _attn_body(*refs, ctx_len, n_ctx_tiles, total, windowed):
    if windowed:
        sink_ref, q_ref, k_ref, v_ref, o_ref, m_ref, l_ref, acc_ref = refs
    else:
        q_ref, k_ref, v_ref, o_ref, m_ref, l_ref, acc_ref = refs
    group = q_ref.shape[1]
    rows = group * TILE
    t = pl.program_id(2)
    q = q_ref[0].reshape(rows, HEAD_DIM)

    def absorb(k0, size, mask):
        s = _dot_nt(q, k_ref[0, 0, pl.ds(k0, size), :])
        if mask is not None:
            s = jnp.where(mask, s, NEG_INF)
        m_old = m_ref[...]
        m_new = jnp.maximum(m_old, jnp.max(s, axis=-1, keepdims=True))
        p = jnp.exp(s - m_new)
        alpha = jnp.exp(m_old - m_new)
        l_ref[...] = alpha * l_ref[...] + jnp.sum(p, axis=-1, keepdims=True)
        acc_ref[...] = alpha * acc_ref[...] + _dot(p, v_ref[0, 0, pl.ds(k0, size), :])
        m_ref[...] = m_new

    if windowed:
        kv = pl.program_id(1)
        sink = jnp.concatenate([jnp.full((TILE, 1), sink_ref[kv * group + gi], F32) for gi in range(group)], axis=0)
        m_ref[...] = sink
        l_ref[...] = jnp.ones_like(l_ref)
    else:
        m_ref[...] = jnp.full_like(m_ref, NEG_INF)
        l_ref[...] = jnp.zeros_like(l_ref)
    acc_ref[...] = jnp.zeros_like(acc_ref)

    absorb(0, ctx_len, None)

    @pl.when(t >= n_ctx_tiles)
    def _():
        if windowed:
            k0 = jnp.clip((t - 1) * TILE, 0, total - LOCAL_KEYS)
            k0 = pl.multiple_of(k0, TILE)
            k_pos = k0 + lax.broadcasted_iota(jnp.int32, (rows, LOCAL_KEYS), 1)
            q_pos = t * TILE + (lax.broadcasted_iota(jnp.int32, (rows, LOCAL_KEYS), 0) & (TILE - 1))
            mask = jnp.logical_and(jnp.abs(k_pos - q_pos) <= WINDOW, k_pos >= ctx_len)
            absorb(k0, LOCAL_KEYS, mask)
        else:
            for k0 in range(ctx_len, total, KEY_CHUNK):
                absorb(k0, min(KEY_CHUNK, total - k0), None)

    out = acc_ref[...] / l_ref[...]
    o_ref[0] = out.reshape(group, TILE, HEAD_DIM).astype(BF16)


def _attention(st, q, k, v, sink):
    nh, nkv = q.shape[1], k.shape[1]
    group = nh // nkv
    windowed = sink is not None
    assert st.t >= LOCAL_KEYS
    q_spec = pl.BlockSpec((1, group, TILE, HEAD_DIM), lambda b, h, t: (b, h, t, 0))
    kv_spec = pl.BlockSpec((1, 1, st.t, HEAD_DIM), lambda b, h, t: (b, h, 0, 0))
    specs, args = [q_spec, kv_spec, kv_spec], [q, k, v]
    if windowed:
        specs = [pl.BlockSpec(memory_space=pltpu.SMEM)] + specs
        args = [sink] + args
    rows = group * TILE
    return pl.pallas_call(
        functools.partial(_attn_body, ctx_len=st.ctx, n_ctx_tiles=st.n_ctx_tiles, total=st.t,
                          windowed=windowed),
        grid=(st.b, nkv, st.n_tiles),
        in_specs=specs,
        out_specs=q_spec,
        out_shape=jax.ShapeDtypeStruct(q.shape, BF16),
        scratch_shapes=[pltpu.VMEM((rows, 1), F32), pltpu.VMEM((rows, 1), F32),
                        pltpu.VMEM((rows, HEAD_DIM), F32)],
        compiler_params=_cparams("parallel", "parallel", "arbitrary"),
        name="window_attention" if windowed else "global_attention",
    )(*args)


def _attn_out_body(x_ref, mod_ref, g_ref, o_ref, wo_ref, out_ref):
    y = None
    for h in range(o_ref.shape[1]):
        part = jnp.dot(o_ref[0, h], wo_ref[h], preferred_element_type=F32)
        y = part if y is None else y + part
    m = mod_ref[0]
    out_ref[0] = x_ref[0] + m[5:6] * _rms(y, g_ref[...][3:4])


def _attn_out(st, x, mod, g, o, wo):
    nh = o.shape[1]
    return pl.pallas_call(
        _attn_out_body,
        grid=(st.b, st.n_tiles),
        in_specs=[st.tile_spec(), st.mod_spec(), _const_spec(g.shape),
                  pl.BlockSpec((1, nh, TILE, HEAD_DIM), lambda b, t: (b, 0, t, 0)), _const_spec(wo.shape)],
        out_specs=st.tile_spec(),
        out_shape=st.out_shape(),
        compiler_params=_cparams("parallel", "parallel"),
        name="attention_out",
    )(x, mod, g, o, wo)


def _pad_to(w, axis, mult):
    pad = (-w.shape[axis]) % mult
    if pad == 0:
        return w
    widths = [(0, 0)] * w.ndim
    widths[axis] = (0, pad)
    return jnp.pad(w, widths)


def _rope_tables(st):
    rows = (st.t - st.ctx) // GRID_W
    row = jnp.broadcast_to(jnp.arange(rows)[:, None], (rows, GRID_W)).reshape(-1)
    col = jnp.broadcast_to(jnp.arange(GRID_W)[None, :], (rows, GRID_W)).reshape(-1)
    inv_freq = ROPE_THETA ** (-jnp.arange(N_FREQ, dtype=F32) / N_FREQ)
    ang = jnp.stack([row, col], axis=-1).astype(F32)[:, :, None] * inv_freq
    cos, sin = jnp.cos(ang), jnp.sin(ang)
    cos_t = jnp.stack([cos, cos], axis=2).reshape(-1, HEAD_DIM)
    sin_t = jnp.stack([-sin, sin], axis=2).reshape(-1, HEAD_DIM)
    cos_t = jnp.concatenate([jnp.ones((st.ctx, HEAD_DIM), F32), cos_t], axis=0)
    sin_t = jnp.concatenate([jnp.zeros((st.ctx, HEAD_DIM), F32), sin_t], axis=0)
    return cos_t, sin_t


def _partner_perm():
    idx = np.arange(HEAD_DIM).reshape(2, 2, N_FREQ)
    return idx[:, ::-1, :].reshape(-1)


def _head_major(w, n_heads):
    d = w.shape[0]
    return jnp.transpose(w.reshape(d, n_heads, HEAD_DIM), (1, 0, 2)).astype(BF16)


def _attn_params(wq, wk, wv, wo, gq, gk):
    nh, nkv = wq.shape[1] // HEAD_DIM, wk.shape[1] // HEAD_DIM
    perm = _partner_perm()
    wq_h, wk_h = _head_major(wq, nh), _head_major(wk, nkv)
    return {"wq": wq_h, "wq_sw": wq_h[:, :, perm], "wk": wk_h, "wk_sw": wk_h[:, :, perm],
            "wv": _head_major(wv, nkv), "wo": wo.reshape(nh, HEAD_DIM, wo.shape[1]).astype(BF16),
            "gq": jnp.stack([gq, gq[perm]]), "gk": jnp.stack([gk, gk[perm]])}


def _segment_matrices(d):
    head = np.arange(d) // RWKV_HEAD
    e = (head[:, None] == np.arange(LANES)[None, :]).astype(np.float32)
    return jnp.asarray(e), jnp.asarray(e.T)


def kernel(x, c, ctx, c_ctx, mod_w, mod_b, norm_g, ffn_w1, ffn_w3, ffn_w2, rwkv_mu, rwkv_wr, rwkv_wk, rwkv_wv, rwkv_wo, rwkv_k_k, rwkv_k_a, rwkv_r_k, rwkv_w0, rwkv_w1, rwkv_w2, rwkv_a0, rwkv_a1, rwkv_a2, rwkv_g1, rwkv_g2, rwkv_ln_w, rwkv_ln_b, rwkv_v0, rwkv_v1, rwkv_v2, gattn_wq, gattn_wk, gattn_wv, gattn_wo, gattn_q_norm, gattn_k_norm, wattn_wq, wattn_wk, wattn_wv, wattn_wo, wattn_sink):
    batch, seq, d = x.shape
    depth = mod_w.shape[0]
    st = _Stream(batch, ctx.shape[1], seq, d)

    rows = -(-(batch + 1) // 8) * 8
    c_rows = jnp.zeros((rows, d), F32).at[:batch].set(c).at[batch].set(c_ctx)
    mods = _modulation(c_rows, mod_w, mod_b).reshape(depth, rows, N_MOD, d)

    w1, w3, w2 = ffn_w1.astype(BF16), ffn_w3.astype(BF16), ffn_w2.astype(BF16)
    rope = _rope_tables(st)
    seg = _segment_matrices(d)
    bf = lambda w: w.astype(BF16)

    h = jnp.concatenate([ctx, x], axis=1)
    v_first = None
    for i in range(depth):
        kind, j = i % 3, i // 3
        g, mod = norm_g[i], mods[i]
        h = _ffn_half(st, h, mod, g, w1[i, 0], w3[i, 0], w2[i, 0], 0, 0)
        if kind == 0:
            p = {"mu": rwkv_mu[j], "wr": bf(rwkv_wr[j]), "wk": bf(rwkv_wk[j]), "wv": bf(rwkv_wv[j]),
                 "wo": bf(rwkv_wo[j]), "k_k": rwkv_k_k[j][None], "k_a": rwkv_k_a[j][None],
                 "r_k": rwkv_r_k[j].reshape(1, d),
                 "w0": rwkv_w0[j][:, None], "w1": bf(_pad_to(rwkv_w1[j], 2, LANES)), "w2": bf(_pad_to(rwkv_w2[j], 1, LANES)),
                 "a0": rwkv_a0[j][:, None], "a1": bf(_pad_to(rwkv_a1[j], 2, LANES)), "a2": bf(_pad_to(rwkv_a2[j], 1, LANES)),
                 "g1": bf(_pad_to(rwkv_g1[j], 2, LANES)), "g2": bf(_pad_to(rwkv_g2[j], 1, LANES)),
                 "ln_w": rwkv_ln_w[j][:, None], "ln_b": rwkv_ln_b[j][:, None]}
            vres = None
            if j > 0:
                vres = (v_first, rwkv_v0[j - 1][None], bf(_pad_to(rwkv_v1[j - 1], 1, LANES)),
                        bf(_pad_to(rwkv_v2[j - 1], 0, LANES)))
            r, v, kk, lw, kd, ab, gate = _rwkv_features(st, h, mod, g, p, seg, vres)
            if j == 0:
                v_first = v
            y_f = _rwkv_scan(st, r, lw, kd, v, kk, ab, 0)
            y_b = _rwkv_scan(st, r, lw, kd, v, kk, ab, 1)
            h = _rwkv_out(st, h, mod, g, y_f, y_b, r, kd, v, gate, p, seg)
        else:
            if kind == 1:
                a = _attn_params(gattn_wq[j], gattn_wk[j], gattn_wv[j], gattn_wo[j],
                                 gattn_q_norm[j], gattn_k_norm[j])
                sink = None
            else:
                ones = jnp.ones((HEAD_DIM,), F32)
                a = _attn_params(wattn_wq[j], wattn_wk[j], wattn_wv[j], wattn_wo[j], ones, ones)
                sink = wattn_sink[j]
            q, k, v = _qkv_project(st, h, mod, g, a, rope, qk_norm=kind == 1)
            o = _attention(st, q, k, v, sink)
            h = _attn_out(st, h, mod, g, o, a["wo"])
        h = _ffn_half(st, h, mod, g, w1[i, 1], w3[i, 1], w2[i, 1], 6, 4)
    return h[:, st.ctx:]
```

```python
import functools

import numpy as np
import jax
import jax.numpy as jnp
from jax import lax
from jax.experimental import pallas as pl
from jax.experimental.pallas import tpu as pltpu

F32 = jnp.float32
BF16 = jnp.bfloat16
HIGHEST = lax.Precision.HIGHEST

GRID_W = 64
N_MOD = 9
HEAD_DIM = 64
N_KV_HEADS = 4
WINDOW = 128
ROPE_THETA = 10000.0
N_FREQ = HEAD_DIM // 4
RWKV_HEAD = 64
NORM_EPS = 1e-6
GN_EPS = 64e-5
L2_EPS = 1e-12
NEG_INF = -1e30

LANES = 128
TILE = 256
CHUNK = 64
MAX_KEYS = 2304
SEG_TILE = 256
LOCAL_KEYS = 3 * TILE
VMEM_LIMIT = 56 * 1024 * 1024


def _cparams(*sem):
    return pltpu.CompilerParams(dimension_semantics=sem, vmem_limit_bytes=VMEM_LIMIT)


def _const_spec(shape):
    nd = len(shape)
    return pl.BlockSpec(shape, lambda *_: (0,) * nd, pipeline_mode=pl.Buffered(1))


def _dot(a, b):
    return jnp.dot(a.astype(BF16), b.astype(BF16), preferred_element_type=F32)


def _dot_nt(a, b):
    return lax.dot_general(a.astype(BF16), b.astype(BF16), (((1,), (1,)), ((), ())),
                           preferred_element_type=F32)


def _dot_tn(a, b):
    return lax.dot_general(a.astype(BF16), b.astype(BF16), (((0,), (0,)), ((), ())),
                           preferred_element_type=F32)


def _dot_f32(a, b):
    return jnp.dot(a, b, precision=HIGHEST, preferred_element_type=F32)


def _sigmoid(x):
    return 1.0 / (1.0 + jnp.exp(-x))


def _rms(x, g):
    return x * lax.rsqrt(jnp.mean(x * x, axis=-1, keepdims=True) + NORM_EPS) * g


def _norm_mod(x, g, scale, shift):
    return _rms(x, g) * (1.0 + scale) + shift


def _seg_sum(x, ones_ref):
    ones = ones_ref[...]
    hi = x.astype(BF16)
    rest = x - hi.astype(F32)
    mid = rest.astype(BF16)
    lo = (rest - mid.astype(F32)).astype(BF16)
    cols = []
    for j in range(x.shape[1] // SEG_TILE):
        sl = slice(j * SEG_TILE, (j + 1) * SEG_TILE)
        cols.append(sum(jnp.dot(part[:, sl], ones, preferred_element_type=F32) for part in (hi, mid, lo)))
    return jnp.concatenate(cols, axis=1)


def _mod_body(c_ref, w_ref, b_ref, o_ref):
    c = c_ref[...]
    o_ref[0] = _dot_f32(c * _sigmoid(c), w_ref[0]) + b_ref[0]


def _modulation(c_rows, mod_w, mod_b):
    depth, d, nd = mod_w.shape
    rows = c_rows.shape[0]
    tn = nd // 6
    return pl.pallas_call(
        _mod_body,
        grid=(depth, nd // tn),
        in_specs=[_const_spec((rows, d)),
                  pl.BlockSpec((1, d, tn), lambda i, j: (i, 0, j)),
                  pl.BlockSpec((1, 1, tn), lambda i, j: (i, 0, j))],
        out_specs=pl.BlockSpec((1, rows, tn), lambda i, j: (i, 0, j)),
        out_shape=jax.ShapeDtypeStruct((depth, rows, nd), F32),
        compiler_params=_cparams("arbitrary", "arbitrary"),
        name="adaln_rows",
    )(c_rows, mod_w, mod_b.reshape(depth, 1, nd))


class _Stream:
    def __init__(self, batch, ctx_len, seq, d):
        assert ctx_len % TILE == 0 and seq % TILE == 0 and d % LANES == 0
        self.b, self.t, self.d = batch, ctx_len + seq, d
        self.ctx = ctx_len
        self.n_tiles = self.t // TILE
        self.n_ctx_tiles = ctx_len // TILE

    def tile_spec(self, width=None):
        return pl.BlockSpec((1, TILE, width or self.d), lambda b, t: (b, t, 0))

    def mod_spec(self):
        nct, ctx_row = self.n_ctx_tiles, self.b
        return pl.BlockSpec((1, N_MOD, self.d), lambda b, t: (jnp.where(t < nct, ctx_row, b), 0, 0))

    def out_shape(self, width=None, dtype=F32):
        return jax.ShapeDtypeStruct((self.b, self.t, width or self.d), dtype)


def _ffn_body(x_ref, mod_ref, g_ref, w1_ref, w3_ref, w2_ref, o_ref, *, mod_row, g_row):
    x = x_ref[0]
    m = mod_ref[0]
    g = g_ref[...]
    shift, scale, gate = m[mod_row:mod_row + 1], m[mod_row + 1:mod_row + 2], m[mod_row + 2:mod_row + 3]
    n = _norm_mod(x, g[g_row:g_row + 1], scale, shift).astype(BF16)
    a = jnp.dot(n, w1_ref[...], preferred_element_type=F32)
    b = jnp.dot(n, w3_ref[...], preferred_element_type=F32)
    h = (a * _sigmoid(a) * b).astype(BF16)
    y = jnp.dot(h, w2_ref[...], preferred_element_type=F32)
    o_ref[0] = x + 0.5 * gate * _rms(y, g[g_row + 1:g_row + 2])


def _ffn_half(st, x, mod, g, w1, w3, w2, mod_row, g_row):
    d, f = w1.shape
    return pl.pallas_call(
        functools.partial(_ffn_body, mod_row=mod_row, g_row=g_row),
        grid=(st.b, st.n_tiles),
        in_specs=[st.tile_spec(), st.mod_spec(), _const_spec(g.shape),
                  _const_spec((d, f)), _const_spec((d, f)), _const_spec((f, d))],
        out_specs=st.tile_spec(),
        out_shape=st.out_shape(),
        compiler_params=_cparams("parallel", "parallel"),
        name="swiglu_half",
    )(x, mod, g, w1, w3, w2)


def _softplus(y):
    return jnp.maximum(y, 0.0) + jnp.log(1.0 + jnp.exp(-jnp.abs(y)))


def _rwkv_feat_body(*refs, n_ctx_tiles, n_tiles, with_vres):
    (x_ref, xp_ref, xn_ref, mod_ref, g_ref, mu_ref, wr_ref, wk_ref, wv_ref, kk_ref, ka_ref,
     w0_ref, w1_ref, w2_ref, a0_ref, a1_ref, a2_ref, g1_ref, g2_ref, ones_ref) = refs[:20]
    rest = refs[20:]
    if with_vres:
        vf_ref, v0_ref, v1_ref, v2_ref = rest[:4]
        rest = rest[4:]
    r_out, v_out, kk_out, lw_out, kd_out, ab_out, g_out = rest

    t = pl.program_id(1)
    m = mod_ref[0]
    gn = g_ref[...][2:3]
    shift, scale = m[3:4], m[4:5]
    n = _norm_mod(x_ref[0], gn, scale, shift)
    first = jnp.logical_or(t == 0, t == n_ctx_tiles)
    last = jnp.logical_or(t == n_ctx_tiles - 1, t == n_tiles - 1)
    n_before = jnp.where(first, 0.0, _norm_mod(xp_ref[0][7:8], gn, scale, shift))
    n_after = jnp.where(last, 0.0, _norm_mod(xn_ref[0][0:1], gn, scale, shift))
    row = lax.broadcasted_iota(jnp.int32, (TILE, 1), 0)
    n_prev = jnp.where(row == 0, n_before, pltpu.roll(n, 1, 0))
    n_next = jnp.where(row == TILE - 1, n_after, pltpu.roll(n, TILE - 1, 0))
    xx = 0.5 * (n_prev + n_next) - n
    mu = mu_ref[...]
    xr, xw, xk, xv, xa, xg = (n + xx * mu[i:i + 1] for i in range(6))

    r = _dot(xr, wr_ref[...])
    k = _dot(xk, wk_ref[...])
    v = _dot(xv, wv_ref[...])
    if with_vres:
        lora = _dot(_dot(xv, v1_ref[...]), v2_ref[...])
        v = v + (vf_ref[0] - v) * _sigmoid(v0_ref[...] + lora)
    kk = k * kk_ref[...]
    kk = kk / jnp.maximum(jnp.sqrt(_seg_sum(kk * kk, ones_ref)), L2_EPS)
    r_out[0] = r
    v_out[0] = v
    kk_out[0] = kk
    k_a = ka_ref[...]
    for dr in range(2):
        wl = w0_ref[dr] + _dot(jnp.tanh(_dot(xw, w1_ref[dr])), w2_ref[dr])
        w = -_softplus(-wl) - 0.5
        a = _sigmoid(a0_ref[dr] + _dot(_dot(xa, a1_ref[dr]), a2_ref[dr]))
        lw_out[dr, 0] = -jnp.exp(w)
        kd_out[dr, 0] = k * (1.0 + (a - 1.0) * k_a)
        ab_out[dr, 0] = kk * a
        g_out[dr, 0] = _dot(_sigmoid(_dot(xg, g1_ref[dr])), g2_ref[dr])


def _rwkv_features(st, x, mod, g, p, seg, vres):
    d = st.d
    n8 = st.t // 8
    per8 = TILE // 8
    prev_spec = pl.BlockSpec((1, 8, d), lambda b, t: (b, jnp.maximum(t * per8 - 1, 0), 0))
    next_spec = pl.BlockSpec((1, 8, d), lambda b, t: (b, jnp.minimum((t + 1) * per8, n8 - 1), 0))
    dir_spec = pl.BlockSpec((2, 1, TILE, d), lambda b, t: (0, b, t, 0))
    dir_shape = jax.ShapeDtypeStruct((2, st.b, st.t, d), F32)
    names = ("mu", "wr", "wk", "wv", "k_k", "k_a", "w0", "w1", "w2", "a0", "a1", "a2", "g1", "g2")
    consts = [p[nm] for nm in names] + [seg]
    args = [x, x, x, mod, g] + consts
    specs = [st.tile_spec(), prev_spec, next_spec, st.mod_spec(), _const_spec(g.shape)]
    specs += [_const_spec(c.shape) for c in consts]
    if vres is not None:
        v_first, v0, v1, v2 = vres
        args += [v_first, v0, v1, v2]
        specs += [st.tile_spec(), _const_spec(v0.shape), _const_spec(v1.shape), _const_spec(v2.shape)]
    return pl.pallas_call(
        functools.partial(_rwkv_feat_body, n_ctx_tiles=st.n_ctx_tiles, n_tiles=st.n_tiles,
                          with_vres=vres is not None),
        grid=(st.b, st.n_tiles),
        in_specs=specs,
        out_specs=[st.tile_spec()] * 3 + [dir_spec] * 4,
        out_shape=[st.out_shape()] * 3 + [dir_shape] * 4,
        compiler_params=_cparams("parallel", "parallel"),
        name="rwkv_features",
    )(*args)


def _split(x):
    hi = x.astype(BF16)
    return hi, (x - hi.astype(F32)).astype(BF16)


def _dot_split(a, b):
    (ah, al), (bh, bl) = a, b
    mm = functools.partial(jnp.dot, preferred_element_type=F32)
    return mm(ah, bh) + mm(ah, bl) + mm(al, bh)


def _unit_triangular_inverses(mats, ri, ci, size):
    eye = (ri == ci).astype(F32)
    ts = [jnp.where((ri >> 1) == (ci >> 1), a, 0.0) + eye for a in mats]
    level = 1
    while (2 << level) <= size:
        off = jnp.logical_and((ri >> (level + 1)) == (ci >> (level + 1)), (ri >> level) != (ci >> level))
        tps = [_split(t) for t in ts]
        ws = [_dot_split(tp, _split(jnp.where(off, a, 0.0))) for tp, a in zip(tps, mats)]
        ts = [t + _dot_split(_split(w), tp) for t, w, tp in zip(ts, ws, tps)]
        level += 1
    return ts


def _scan_body(r_ref, lw_ref, k_ref, v_ref, kk_ref, ab_ref, y_ref, s_ref, *, reverse):
    c = CHUNK
    n2 = 2 * c

    @pl.when(pl.program_id(1) == 0)
    def _():
        s_ref[...] = jnp.zeros_like(s_ref)

    r, lw, k, v, kk, ab = r_ref[0], lw_ref[0, 0], k_ref[0, 0], v_ref[0], kk_ref[0], ab_ref[0, 0]
    ti = lax.broadcasted_iota(jnp.int32, (c, c), 0)
    si = lax.broadcasted_iota(jnp.int32, (c, c), 1)
    upto = (si >= ti) if reverse else (si <= ti)
    cum = _dot_f32(upto.astype(F32), lw)
    tot = cum[0:1] if reverse else cum[c - 1:c]
    a_t = -kk * jnp.exp(cum - lw)
    r_t = r * jnp.exp(cum)
    inv = jnp.exp(-cum)
    b_t = ab * inv
    k_t = k * inv
    to_end = jnp.exp(tot - cum)
    b_e = ab * to_end
    k_e = k * to_end
    w_all = jnp.exp(tot)

    ri = lax.broadcasted_iota(jnp.int32, (n2, n2), 0)
    ci = lax.broadcasted_iota(jnp.int32, (n2, n2), 1)
    same = (ri >> 6) == (ci >> 6)
    tt, ss = ri & (c - 1), ci & (c - 1)
    before = (ss > tt) if reverse else (ss < tt)
    strict = jnp.logical_and(same, before)
    incl = jnp.logical_and(same, jnp.logical_or(before, ss == tt))
    incl2 = jnp.concatenate([incl, incl], axis=1)
    even = lax.broadcasted_iota(jnp.int32, (1, LANES), 1) < RWKV_HEAD

    def stack(x):
        return jnp.concatenate([jnp.where(even, x, 0.0), jnp.where(even, 0.0, x)], axis=0)

    pairs = range(r.shape[1] // LANES)
    lanes = [slice(p * LANES, (p + 1) * LANES) for p in pairs]
    prods = [_dot_nt(jnp.concatenate([stack(a_t[:, sl]), stack(r_t[:, sl])], axis=0),
                     jnp.concatenate([stack(b_t[:, sl]), stack(k_t[:, sl])], axis=0)) for sl in lanes]
    a_b = [jnp.where(strict, pr[0:n2, 0:n2], 0.0) for pr in prods]
    t_inv = _unit_triangular_inverses(a_b, ri, ci, c)
    v_st = [stack(v[:, sl]) for sl in lanes]
    u_own = [_dot(jnp.where(strict, pr[0:n2, n2:2 * n2], 0.0), vs) for pr, vs in zip(prods, v_st)]
    from_state = [_dot_nt(jnp.concatenate([a_t[:, sl], r_t[:, sl]], axis=0), s_ref[p])
                  for p, sl in zip(pairs, lanes)]
    u_st = [_dot(t, stack(fs[0:c]) + uo) for t, fs, uo in zip(t_inv, from_state, u_own)]
    y_st = [_dot(jnp.where(incl2, pr[n2:2 * n2], 0.0), jnp.concatenate([us, vs], axis=0))
            for pr, us, vs in zip(prods, u_st, v_st)]
    for p, sl in zip(pairs, lanes):
        y_ref[0, :, sl] = from_state[p][c:2 * c] + y_st[p][0:c] + y_st[p][c:n2]
        u = u_st[p][0:c] + u_st[p][c:n2]
        upd = _dot_tn(jnp.concatenate([u, v[:, sl]], axis=0),
                      jnp.concatenate([b_e[:, sl], k_e[:, sl]], axis=0))
        s_ref[p] = s_ref[p] * w_all[:, sl] + jnp.where(same, upd, 0.0)


def _rwkv_scan(st, r, lw, kd, v, kk, ab, direction):
    d = st.d
    nc, ncc = st.t // CHUNK, st.ctx // CHUNK
    if direction == 0:
        def cidx(s):
            return s
    else:
        def cidx(s):
            return jnp.where(s < ncc, ncc - 1 - s, nc - 1 - s + ncc)
    shared = pl.BlockSpec((1, CHUNK, d), lambda b, s: (b, cidx(s), 0))
    per_dir = pl.BlockSpec((1, 1, CHUNK, d), lambda b, s: (direction, b, cidx(s), 0))
    return pl.pallas_call(
        functools.partial(_scan_body, reverse=direction == 1),
        grid=(st.b, nc),
        in_specs=[shared, per_dir, per_dir, shared, shared, per_dir],
        out_specs=shared,
        out_shape=st.out_shape(),
        scratch_shapes=[pltpu.VMEM((d // LANES, LANES, LANES), F32)],
        compiler_params=_cparams("parallel", "arbitrary"),
        name="rwkv_scan_bwd" if direction else "rwkv_scan_fwd",
    )(r, lw, kd, v, kk, ab)


def _rwkv_out_body(x_ref, mod_ref, g_ref, yf_ref, yb_ref, r_ref, kd_ref, v_ref, gate_ref,
                   lnw_ref, lnb_ref, rk_ref, wo_ref, ones_ref, o_ref):
    r, v = r_ref[0], v_ref[0]
    inv_n = 1.0 / RWKV_HEAD
    o = None
    for dr, y_ref in enumerate((yf_ref, yb_ref)):
        y = y_ref[0]
        cen = y - _seg_sum(y, ones_ref) * inv_n
        var = _seg_sum(cen * cen, ones_ref) * inv_n
        yn = cen * lax.rsqrt(var + GN_EPS) * lnw_ref[dr] + lnb_ref[dr]
        bonus = _seg_sum(r * kd_ref[dr, 0] * rk_ref[...], ones_ref) * v
        od = (yn + bonus) * gate_ref[dr, 0]
        o = od if o is None else o + od
    y_mix = _dot(o, wo_ref[...])
    m = mod_ref[0]
    o_ref[0] = x_ref[0] + m[5:6] * _rms(y_mix, g_ref[...][3:4])


def _rwkv_out(st, x, mod, g, y_f, y_b, r, kd, v, gate, p, seg):
    d = st.d
    dir_spec = pl.BlockSpec((2, 1, TILE, d), lambda b, t: (0, b, t, 0))
    consts = [p["ln_w"], p["ln_b"], p["r_k"], p["wo"]] + [seg]
    return pl.pallas_call(
        _rwkv_out_body,
        grid=(st.b, st.n_tiles),
        in_specs=[st.tile_spec(), st.mod_spec(), _const_spec(g.shape), st.tile_spec(), st.tile_spec(),
                  st.tile_spec(), dir_spec, st.tile_spec(), dir_spec] + [_const_spec(c.shape) for c in consts],
        out_specs=st.tile_spec(),
        out_shape=st.out_shape(),
        compiler_params=_cparams("parallel", "parallel"),
        name="rwkv_readout",
    )(x, mod, g, y_f, y_b, r, kd, v, gate, *consts)


def _qkv_body(x_ref, mod_ref, g_ref, wq_ref, wqs_ref, wk_ref, wks_ref, wv_ref, gq_ref, gk_ref,
              cos_ref, sin_ref, q_out, k_out, v_out, *, qk_norm, q_scale):
    m = mod_ref[0]
    n = _norm_mod(x_ref[0], g_ref[...][2:3], m[4:5], m[3:4]).astype(BF16)
    cos, sin = cos_ref[...], sin_ref[...]

    def rotated(w_ref, ws_ref, gain_ref, h, scale):
        t = jnp.dot(n, w_ref[h], preferred_element_type=F32)
        ts = jnp.dot(n, ws_ref[h], preferred_element_type=F32)
        if qk_norm:
            inv = lax.rsqrt(jnp.mean(t * t, axis=-1, keepdims=True) + NORM_EPS)
            gain = gain_ref[...]
            t, ts = t * inv * gain[0:1], ts * inv * gain[1:2]
        out = t * cos + ts * sin
        return out * scale if scale != 1.0 else out

    for h in range(wq_ref.shape[0]):
        q_out[0, h] = rotated(wq_ref, wqs_ref, gq_ref, h, q_scale).astype(BF16)
    for h in range(wk_ref.shape[0]):
        k_out[0, h] = rotated(wk_ref, wks_ref, gk_ref, h, 1.0).astype(BF16)
        v_out[0, h] = jnp.dot(n, wv_ref[h], preferred_element_type=F32).astype(BF16)


def _qkv_project(st, x, mod, g, a, rope, qk_norm):
    nh, nkv = a["wq"].shape[0], a["wk"].shape[0]
    consts = [a["wq"], a["wq_sw"], a["wk"], a["wk_sw"], a["wv"], a["gq"], a["gk"]]
    rope_spec = pl.BlockSpec((TILE, HEAD_DIM), lambda b, t: (t, 0))

    def head_spec(n):
        return pl.BlockSpec((1, n, TILE, HEAD_DIM), lambda b, t: (b, 0, t, 0))

    def head_shape(n):
        return jax.ShapeDtypeStruct((st.b, n, st.t, HEAD_DIM), BF16)

    return pl.pallas_call(
        functools.partial(_qkv_body, qk_norm=qk_norm, q_scale=HEAD_DIM ** -0.5),
        grid=(st.b, st.n_tiles),
        in_specs=[st.tile_spec(), st.mod_spec(), _const_spec(g.shape)]
                 + [_const_spec(c.shape) for c in consts] + [rope_spec, rope_spec],
        out_specs=[head_spec(nh), head_spec(nkv), head_spec(nkv)],
        out_shape=[head_shape(nh), head_shape(nkv), head_shape(nkv)],
        compiler_params=_cparams("parallel", "parallel"),
        name="qkv_rope",
    )(x, mod, g, *consts, *rope)


def _key_chunks(k0, k1):
    n = -(-(k1 - k0) // MAX_KEYS)
    size = -(-(k1 - k0) // (n * LANES)) * LANES
    return [(a, min(size, k1 - a)) for a in range(k0, k1, size)]


def _attn_body(*refs, ctx_len, n_ctx_tiles, total, windowed):
    if windowed:
        sink_ref, q_ref, k_ref, v_ref, o_ref = refs
    else:
        q_ref, k_ref, v_ref, o_ref = refs
    group = q_ref.shape[1]
    rows = group * TILE
    t = pl.program_id(2)
    q = q_ref[0].reshape(rows, HEAD_DIM)

    def absorb(state, k0, size, mask=None):
        s = _dot_nt(q, k_ref[0, 0, pl.ds(k0, size), :])
        if mask is not None:
            s = jnp.where(mask, s, NEG_INF)
        m_chunk = jnp.max(s, axis=-1, keepdims=True)
        if state is None:
            p = jnp.exp(s - m_chunk)
            return m_chunk, jnp.sum(p, axis=-1, keepdims=True), _dot(p, v_ref[0, 0, pl.ds(k0, size), :])
        m_old, l_old, acc = state
        m_new = jnp.maximum(m_old, m_chunk)
        p = jnp.exp(s - m_new)
        alpha = jnp.exp(m_old - m_new)
        pv = _dot(p, v_ref[0, 0, pl.ds(k0, size), :])
        acc = pv if acc is None else alpha * acc + pv
        return m_new, alpha * l_old + jnp.sum(p, axis=-1, keepdims=True), acc

    if windowed:
        kv = pl.program_id(1)
        sink = jnp.concatenate([jnp.full((TILE, 1), sink_ref[kv * group + gi], F32) for gi in range(group)], axis=0)
        start = (sink, jnp.ones_like(sink), None)
    else:
        start = None

    def finish(state):
        _, l, acc = state
        return acc / l

    def context_queries():
        return finish(absorb(start, 0, ctx_len))

    def latent_queries():
        if windowed:
            state = absorb(start, 0, ctx_len)
            k0 = pl.multiple_of(jnp.clip((t - 1) * TILE, 0, total - LOCAL_KEYS), TILE)
            k_pos = k0 + lax.broadcasted_iota(jnp.int32, (rows, LOCAL_KEYS), 1)
            q_pos = t * TILE + (lax.broadcasted_iota(jnp.int32, (rows, LOCAL_KEYS), 0) & (TILE - 1))
            mask = jnp.logical_and(jnp.abs(k_pos - q_pos) <= WINDOW, k_pos >= ctx_len)
            return finish(absorb(state, k0, LOCAL_KEYS, mask))
        state = start
        for k0, size in _key_chunks(0, total):
            state = absorb(state, k0, size)
        return finish(state)

    out = lax.cond(t < n_ctx_tiles, context_queries, latent_queries)
    o_ref[0] = out.reshape(group, TILE, HEAD_DIM).astype(BF16)


def _attention(st, q, k, v, sink):
    nh, nkv = q.shape[1], k.shape[1]
    group = nh // nkv
    windowed = sink is not None
    assert st.t >= LOCAL_KEYS
    q_spec = pl.BlockSpec((1, group, TILE, HEAD_DIM), lambda b, h, t: (b, h, t, 0))
    kv_spec = pl.BlockSpec((1, 1, st.t, HEAD_DIM), lambda b, h, t: (b, h, 0, 0))
    specs, args = [q_spec, kv_spec, kv_spec], [q, k, v]
    if windowed:
        specs = [pl.BlockSpec(memory_space=pltpu.SMEM)] + specs
        args = [sink] + args
    return pl.pallas_call(
        functools.partial(_attn_body, ctx_len=st.ctx, n_ctx_tiles=st.n_ctx_tiles, total=st.t,
                          windowed=windowed),
        grid=(st.b, nkv, st.n_tiles),
        in_specs=specs,
        out_specs=q_spec,
        out_shape=jax.ShapeDtypeStruct(q.shape, BF16),
        compiler_params=_cparams("parallel", "parallel", "parallel"),
        name="window_attention" if windowed else "global_attention",
    )(*args)


def _attn_out_body(x_ref, mod_ref, g_ref, o_ref, wo_ref, out_ref):
    y = None
    for h in range(o_ref.shape[1]):
        part = jnp.dot(o_ref[0, h], wo_ref[h], preferred_element_type=F32)
        y = part if y is None else y + part
    m = mod_ref[0]
    out_ref[0] = x_ref[0] + m[5:6] * _rms(y, g_ref[...][3:4])


def _attn_out(st, x, mod, g, o, wo):
    nh = o.shape[1]
    return pl.pallas_call(
        _attn_out_body,
        grid=(st.b, st.n_tiles),
        in_specs=[st.tile_spec(), st.mod_spec(), _const_spec(g.shape),
                  pl.BlockSpec((1, nh, TILE, HEAD_DIM), lambda b, t: (b, 0, t, 0)), _const_spec(wo.shape)],
        out_specs=st.tile_spec(),
        out_shape=st.out_shape(),
        compiler_params=_cparams("parallel", "parallel"),
        name="attention_out",
    )(x, mod, g, o, wo)


def _pad_to(w, axis, mult):
    pad = (-w.shape[axis]) % mult
    if pad == 0:
        return w
    widths = [(0, 0)] * w.ndim
    widths[axis] = (0, pad)
    return jnp.pad(w, widths)


def _rope_tables(st):
    rows = (st.t - st.ctx) // GRID_W
    row = jnp.broadcast_to(jnp.arange(rows)[:, None], (rows, GRID_W)).reshape(-1)
    col = jnp.broadcast_to(jnp.arange(GRID_W)[None, :], (rows, GRID_W)).reshape(-1)
    inv_freq = ROPE_THETA ** (-jnp.arange(N_FREQ, dtype=F32) / N_FREQ)
    ang = jnp.stack([row, col], axis=-1).astype(F32)[:, :, None] * inv_freq
    cos, sin = jnp.cos(ang), jnp.sin(ang)
    cos_t = jnp.stack([cos, cos], axis=2).reshape(-1, HEAD_DIM)
    sin_t = jnp.stack([-sin, sin], axis=2).reshape(-1, HEAD_DIM)
    cos_t = jnp.concatenate([jnp.ones((st.ctx, HEAD_DIM), F32), cos_t], axis=0)
    sin_t = jnp.concatenate([jnp.zeros((st.ctx, HEAD_DIM), F32), sin_t], axis=0)
    return cos_t, sin_t


def _partner_perm():
    idx = np.arange(HEAD_DIM).reshape(2, 2, N_FREQ)
    return idx[:, ::-1, :].reshape(-1)


def _head_major(w, n_heads):
    d = w.shape[0]
    return jnp.transpose(w.reshape(d, n_heads, HEAD_DIM), (1, 0, 2)).astype(BF16)


def _attn_params(wq, wk, wv, wo, gq, gk):
    nh, nkv = wq.shape[1] // HEAD_DIM, wk.shape[1] // HEAD_DIM
    perm = _partner_perm()
    wq_h, wk_h = _head_major(wq, nh), _head_major(wk, nkv)
    return {"wq": wq_h, "wq_sw": wq_h[:, :, perm], "wk": wk_h, "wk_sw": wk_h[:, :, perm],
            "wv": _head_major(wv, nkv), "wo": wo.reshape(nh, HEAD_DIM, wo.shape[1]).astype(BF16),
            "gq": jnp.stack([gq, gq[perm]]), "gk": jnp.stack([gk, gk[perm]])}


def _segment_ones():
    head = np.arange(SEG_TILE) // RWKV_HEAD
    return jnp.asarray(head[:, None] == head[None, :], dtype=BF16)


def kernel(x, c, ctx, c_ctx, mod_w, mod_b, norm_g, ffn_w1, ffn_w3, ffn_w2, rwkv_mu, rwkv_wr, rwkv_wk, rwkv_wv, rwkv_wo, rwkv_k_k, rwkv_k_a, rwkv_r_k, rwkv_w0, rwkv_w1, rwkv_w2, rwkv_a0, rwkv_a1, rwkv_a2, rwkv_g1, rwkv_g2, rwkv_ln_w, rwkv_ln_b, rwkv_v0, rwkv_v1, rwkv_v2, gattn_wq, gattn_wk, gattn_wv, gattn_wo, gattn_q_norm, gattn_k_norm, wattn_wq, wattn_wk, wattn_wv, wattn_wo, wattn_sink):
    batch, seq, d = x.shape
    depth = mod_w.shape[0]
    st = _Stream(batch, ctx.shape[1], seq, d)

    rows = -(-(batch + 1) // 8) * 8
    c_rows = jnp.zeros((rows, d), F32).at[:batch].set(c).at[batch].set(c_ctx)
    mods = _modulation(c_rows, mod_w, mod_b).reshape(depth, rows, N_MOD, d)

    w1, w3, w2 = ffn_w1.astype(BF16), ffn_w3.astype(BF16), ffn_w2.astype(BF16)
    rope = _rope_tables(st)
    seg = _segment_ones()
    bf = lambda w: w.astype(BF16)

    h = jnp.concatenate([ctx, x], axis=1)
    v_first = None
    for i in range(depth):
        kind, j = i % 3, i // 3
        g, mod = norm_g[i], mods[i]
        h = _ffn_half(st, h, mod, g, w1[i, 0], w3[i, 0], w2[i, 0], 0, 0)
        if kind == 0:
            p = {"mu": rwkv_mu[j], "wr": bf(rwkv_wr[j]), "wk": bf(rwkv_wk[j]), "wv": bf(rwkv_wv[j]),
                 "wo": bf(rwkv_wo[j]), "k_k": rwkv_k_k[j][None], "k_a": rwkv_k_a[j][None],
                 "r_k": rwkv_r_k[j].reshape(1, d),
                 "w0": rwkv_w0[j][:, None], "w1": bf(_pad_to(rwkv_w1[j], 2, LANES)), "w2": bf(_pad_to(rwkv_w2[j], 1, LANES)),
                 "a0": rwkv_a0[j][:, None], "a1": bf(_pad_to(rwkv_a1[j], 2, LANES)), "a2": bf(_pad_to(rwkv_a2[j], 1, LANES)),
                 "g1": bf(_pad_to(rwkv_g1[j], 2, LANES)), "g2": bf(_pad_to(rwkv_g2[j], 1, LANES)),
                 "ln_w": rwkv_ln_w[j][:, None], "ln_b": rwkv_ln_b[j][:, None]}
            vres = None
            if j > 0:
                vres = (v_first, rwkv_v0[j - 1][None], bf(_pad_to(rwkv_v1[j - 1], 1, LANES)),
                        bf(_pad_to(rwkv_v2[j - 1], 0, LANES)))
            r, v, kk, lw, kd, ab, gate = _rwkv_features(st, h, mod, g, p, seg, vres)
            if j == 0:
                v_first = v
            y_f = _rwkv_scan(st, r, lw, kd, v, kk, ab, 0)
            y_b = _rwkv_scan(st, r, lw, kd, v, kk, ab, 1)
            h = _rwkv_out(st, h, mod, g, y_f, y_b, r, kd, v, gate, p, seg)
        else:
            if kind == 1:
                a = _attn_params(gattn_wq[j], gattn_wk[j], gattn_wv[j], gattn_wo[j],
                                 gattn_q_norm[j], gattn_k_norm[j])
                sink = None
            else:
                ones = jnp.ones((HEAD_DIM,), F32)
                a = _attn_params(wattn_wq[j], wattn_wk[j], wattn_wv[j], wattn_wo[j], ones, ones)
                sink = wattn_sink[j]
            q, k, v = _qkv_project(st, h, mod, g, a, rope, qk_norm=kind == 1)
            o = _attention(st, q, k, v, sink)
            h = _attn_out(st, h, mod, g, o, a["wo"])
        h = _ffn_half(st, h, mod, g, w1[i, 1], w3[i, 1], w2[i, 1], 6, 4)
    return h[:, st.ctx:]
```

```python
import functools

import numpy as np
import jax
import jax.numpy as jnp
from jax import lax
from jax.experimental import pallas as pl
from jax.experimental.pallas import tpu as pltpu

F32 = jnp.float32
BF16 = jnp.bfloat16
HIGHEST = lax.Precision.HIGHEST

GRID_W = 64
N_MOD = 9
HEAD_DIM = 64
N_KV_HEADS = 4
WINDOW = 128
ROPE_THETA = 10000.0
N_FREQ = HEAD_DIM // 4
RWKV_HEAD = 64
NORM_EPS = 1e-6
GN_EPS = 64e-5
L2_EPS = 1e-12
NEG_INF = -1e30
LOG2_E = 1.4426950408889634

LANES = 128
TILE = 256
CHUNK = 64
MAX_KEYS = 2304
SEG_TILE = 256
LOCAL_KEYS = 3 * TILE
VMEM_LIMIT = 56 * 1024 * 1024


def _cparams(*sem):
    return pltpu.CompilerParams(dimension_semantics=sem, vmem_limit_bytes=VMEM_LIMIT)


def _const_spec(shape):
    nd = len(shape)
    return pl.BlockSpec(shape, lambda *_: (0,) * nd, pipeline_mode=pl.Buffered(1))


def _dot(a, b):
    return jnp.dot(a.astype(BF16), b.astype(BF16), preferred_element_type=F32)


def _dot_nt(a, b):
    return lax.dot_general(a.astype(BF16), b.astype(BF16), (((1,), (1,)), ((), ())),
                           preferred_element_type=F32)


def _dot_tn(a, b):
    return lax.dot_general(a.astype(BF16), b.astype(BF16), (((0,), (0,)), ((), ())),
                           preferred_element_type=F32)


def _dot_f32(a, b):
    return jnp.dot(a, b, precision=HIGHEST, preferred_element_type=F32)


def _sigmoid(x):
    return 1.0 / (1.0 + jnp.exp(-x))


def _rms(x, g):
    return x * lax.rsqrt(jnp.mean(x * x, axis=-1, keepdims=True) + NORM_EPS) * g


def _norm_mod(x, g, scale, shift):
    return _rms(x, g) * (1.0 + scale) + shift


def _seg_sum(x, ones_ref):
    ones = ones_ref[...]
    hi = x.astype(BF16)
    rest = x - hi.astype(F32)
    mid = rest.astype(BF16)
    lo = (rest - mid.astype(F32)).astype(BF16)
    cols = []
    for j in range(x.shape[1] // SEG_TILE):
        sl = slice(j * SEG_TILE, (j + 1) * SEG_TILE)
        cols.append(sum(jnp.dot(part[:, sl], ones, preferred_element_type=F32) for part in (hi, mid, lo)))
    return jnp.concatenate(cols, axis=1)


def _mod_body(c_ref, w_ref, b_ref, o_ref):
    c = c_ref[...]
    o_ref[0] = _dot_f32(c * _sigmoid(c), w_ref[0]) + b_ref[0]


def _modulation(c_rows, mod_w, mod_b):
    depth, d, nd = mod_w.shape
    rows = c_rows.shape[0]
    tn = nd // 6
    return pl.pallas_call(
        _mod_body,
        grid=(depth, nd // tn),
        in_specs=[_const_spec((rows, d)),
                  pl.BlockSpec((1, d, tn), lambda i, j: (i, 0, j)),
                  pl.BlockSpec((1, 1, tn), lambda i, j: (i, 0, j))],
        out_specs=pl.BlockSpec((1, rows, tn), lambda i, j: (i, 0, j)),
        out_shape=jax.ShapeDtypeStruct((depth, rows, nd), F32),
        compiler_params=_cparams("arbitrary", "arbitrary"),
        name="adaln_rows",
    )(c_rows, mod_w, mod_b.reshape(depth, 1, nd))


class _Stream:
    def __init__(self, batch, ctx_len, seq, d):
        assert ctx_len % TILE == 0 and seq % TILE == 0 and d % LANES == 0
        self.b, self.t, self.d = batch, ctx_len + seq, d
        self.ctx = ctx_len
        self.n_tiles = self.t // TILE
        self.n_ctx_tiles = ctx_len // TILE

    def tile_spec(self, width=None):
        return pl.BlockSpec((1, TILE, width or self.d), lambda b, t: (b, t, 0))

    def mod_spec(self):
        nct, ctx_row = self.n_ctx_tiles, self.b
        return pl.BlockSpec((1, N_MOD, self.d), lambda b, t: (jnp.where(t < nct, ctx_row, b), 0, 0))

    def out_shape(self, width=None, dtype=F32):
        return jax.ShapeDtypeStruct((self.b, self.t, width or self.d), dtype)


def _ffn_body(x_ref, mod_ref, g_ref, w1_ref, w3_ref, w2_ref, o_ref, *, mod_row, g_row):
    x = x_ref[0]
    m = mod_ref[0]
    g = g_ref[...]
    shift, scale, gate = m[mod_row:mod_row + 1], m[mod_row + 1:mod_row + 2], m[mod_row + 2:mod_row + 3]
    n = _norm_mod(x, g[g_row:g_row + 1], scale, shift).astype(BF16)
    a = jnp.dot(n, w1_ref[...], preferred_element_type=F32)
    b = jnp.dot(n, w3_ref[...], preferred_element_type=F32)
    h = (a * _sigmoid(a) * b).astype(BF16)
    y = jnp.dot(h, w2_ref[...], preferred_element_type=F32)
    o_ref[0] = x + 0.5 * gate * _rms(y, g[g_row + 1:g_row + 2])


def _ffn_half(st, x, mod, g, w1, w3, w2, mod_row, g_row):
    d, f = w1.shape
    return pl.pallas_call(
        functools.partial(_ffn_body, mod_row=mod_row, g_row=g_row),
        grid=(st.b, st.n_tiles),
        in_specs=[st.tile_spec(), st.mod_spec(), _const_spec(g.shape),
                  _const_spec((d, f)), _const_spec((d, f)), _const_spec((f, d))],
        out_specs=st.tile_spec(),
        out_shape=st.out_shape(),
        compiler_params=_cparams("parallel", "parallel"),
        name="swiglu_half",
    )(x, mod, g, w1, w3, w2)


def _softplus(y):
    return jnp.maximum(y, 0.0) + jnp.log(1.0 + jnp.exp(-jnp.abs(y)))


def _rwkv_feat_body(*refs, n_ctx_tiles, n_tiles, with_vres):
    (x_ref, xp_ref, xn_ref, mod_ref, g_ref, mu_ref, wr_ref, wk_ref, wv_ref, kk_ref, ka_ref,
     w0_ref, w1_ref, w2_ref, a0_ref, a1_ref, a2_ref, g1_ref, g2_ref, ones_ref) = refs[:20]
    rest = refs[20:]
    if with_vres:
        vf_ref, v0_ref, v1_ref, v2_ref = rest[:4]
        rest = rest[4:]
    r_out, v_out, kk_out, lw_out, kd_out, ab_out, g_out = rest

    t = pl.program_id(1)
    m = mod_ref[0]
    gn = g_ref[...][2:3]
    shift, scale = m[3:4], m[4:5]
    n = _norm_mod(x_ref[0], gn, scale, shift)
    first = jnp.logical_or(t == 0, t == n_ctx_tiles)
    last = jnp.logical_or(t == n_ctx_tiles - 1, t == n_tiles - 1)
    n_before = jnp.where(first, 0.0, _norm_mod(xp_ref[0][7:8], gn, scale, shift))
    n_after = jnp.where(last, 0.0, _norm_mod(xn_ref[0][0:1], gn, scale, shift))
    row = lax.broadcasted_iota(jnp.int32, (TILE, 1), 0)
    n_prev = jnp.where(row == 0, n_before, pltpu.roll(n, 1, 0))
    n_next = jnp.where(row == TILE - 1, n_after, pltpu.roll(n, TILE - 1, 0))
    xx = 0.5 * (n_prev + n_next) - n
    mu = mu_ref[...]
    xr, xw, xk, xv, xa, xg = (n + xx * mu[i:i + 1] for i in range(6))

    r = _dot(xr, wr_ref[...])
    k = _dot(xk, wk_ref[...])
    v = _dot(xv, wv_ref[...])
    if with_vres:
        lora = _dot(_dot(xv, v1_ref[...]), v2_ref[...])
        v = v + (vf_ref[0] - v) * _sigmoid(v0_ref[...] + lora)
    kk = k * kk_ref[...]
    kk = kk / jnp.maximum(jnp.sqrt(_seg_sum(kk * kk, ones_ref)), L2_EPS)
    r_out[0] = r
    v_out[0] = v
    kk_out[0] = kk
    k_a = ka_ref[...]
    for dr in range(2):
        wl = w0_ref[dr] + _dot(jnp.tanh(_dot(xw, w1_ref[dr])), w2_ref[dr])
        w = -_softplus(-wl) - 0.5
        a = _sigmoid(a0_ref[dr] + _dot(_dot(xa, a1_ref[dr]), a2_ref[dr]))
        lw_out[dr, 0] = -jnp.exp(w)
        kd_out[dr, 0] = k * (1.0 + (a - 1.0) * k_a)
        ab_out[dr, 0] = kk * a
        g_out[dr, 0] = _dot(_sigmoid(_dot(xg, g1_ref[dr])), g2_ref[dr])


def _rwkv_features(st, x, mod, g, p, seg, vres):
    d = st.d
    n8 = st.t // 8
    per8 = TILE // 8
    prev_spec = pl.BlockSpec((1, 8, d), lambda b, t: (b, jnp.maximum(t * per8 - 1, 0), 0))
    next_spec = pl.BlockSpec((1, 8, d), lambda b, t: (b, jnp.minimum((t + 1) * per8, n8 - 1), 0))
    dir_spec = pl.BlockSpec((2, 1, TILE, d), lambda b, t: (0, b, t, 0))
    dir_shape = jax.ShapeDtypeStruct((2, st.b, st.t, d), F32)
    names = ("mu", "wr", "wk", "wv", "k_k", "k_a", "w0", "w1", "w2", "a0", "a1", "a2", "g1", "g2")
    consts = [p[nm] for nm in names] + [seg]
    args = [x, x, x, mod, g] + consts
    specs = [st.tile_spec(), prev_spec, next_spec, st.mod_spec(), _const_spec(g.shape)]
    specs += [_const_spec(c.shape) for c in consts]
    if vres is not None:
        v_first, v0, v1, v2 = vres
        args += [v_first, v0, v1, v2]
        specs += [st.tile_spec(), _const_spec(v0.shape), _const_spec(v1.shape), _const_spec(v2.shape)]
    return pl.pallas_call(
        functools.partial(_rwkv_feat_body, n_ctx_tiles=st.n_ctx_tiles, n_tiles=st.n_tiles,
                          with_vres=vres is not None),
        grid=(st.b, st.n_tiles),
        in_specs=specs,
        out_specs=[st.tile_spec()] * 3 + [dir_spec] * 4,
        out_shape=[st.out_shape()] * 3 + [dir_shape] * 4,
        compiler_params=_cparams("parallel", "parallel"),
        name="rwkv_features",
    )(*args)


def _unit_triangular_inverses(mats, ri, ci, size):
    eye = (ri == ci).astype(F32)
    ts = [jnp.where((ri >> 1) == (ci >> 1), a, 0.0) + eye for a in mats]
    level = 1
    while (2 << level) <= size:
        off = jnp.logical_and((ri >> (level + 1)) == (ci >> (level + 1)), (ri >> level) != (ci >> level))
        ws = [_dot(t, jnp.where(off, a, 0.0)) for t, a in zip(ts, mats)]
        ts = [t + _dot(w, t) for t, w in zip(ts, ws)]
        level += 1
    return ts


def _scan_body(r_ref, lw_ref, k_ref, v_ref, kk_ref, ab_ref, y_ref, s_ref, *, reverse):
    c = CHUNK
    n2 = 2 * c

    @pl.when(pl.program_id(1) == 0)
    def _():
        s_ref[...] = jnp.zeros_like(s_ref)

    r, lw, k, v, kk, ab = r_ref[0], lw_ref[0, 0], k_ref[0, 0], v_ref[0], kk_ref[0], ab_ref[0, 0]
    ti = lax.broadcasted_iota(jnp.int32, (c, c), 0)
    si = lax.broadcasted_iota(jnp.int32, (c, c), 1)
    upto = (si >= ti) if reverse else (si <= ti)
    cum = _dot_f32(upto.astype(F32), lw)
    tot = cum[0:1] if reverse else cum[c - 1:c]
    a_t = -kk * jnp.exp(cum - lw)
    r_t = r * jnp.exp(cum)
    inv = jnp.exp(-cum)
    b_t = ab * inv
    k_t = k * inv
    to_end = jnp.exp(tot - cum)
    b_e = ab * to_end
    k_e = k * to_end
    w_all = jnp.exp(tot)

    ri = lax.broadcasted_iota(jnp.int32, (n2, n2), 0)
    ci = lax.broadcasted_iota(jnp.int32, (n2, n2), 1)
    same = (ri >> 6) == (ci >> 6)
    tt, ss = ri & (c - 1), ci & (c - 1)
    before = (ss > tt) if reverse else (ss < tt)
    strict = jnp.logical_and(same, before)
    incl = jnp.logical_and(same, jnp.logical_or(before, ss == tt))
    incl2 = jnp.concatenate([incl, incl], axis=1)
    even = lax.broadcasted_iota(jnp.int32, (1, LANES), 1) < RWKV_HEAD

    def stack(x):
        return jnp.concatenate([jnp.where(even, x, 0.0), jnp.where(even, 0.0, x)], axis=0)

    pairs = range(r.shape[1] // LANES)
    lanes = [slice(p * LANES, (p + 1) * LANES) for p in pairs]
    prods = [_dot_nt(jnp.concatenate([stack(a_t[:, sl]), stack(r_t[:, sl])], axis=0),
                     jnp.concatenate([stack(b_t[:, sl]), stack(k_t[:, sl])], axis=0)) for sl in lanes]
    a_b = [jnp.where(strict, pr[0:n2, 0:n2], 0.0) for pr in prods]
    t_inv = _unit_triangular_inverses(a_b, ri, ci, c)
    v_st = [stack(v[:, sl]) for sl in lanes]
    u_own = [_dot(jnp.where(strict, pr[0:n2, n2:2 * n2], 0.0), vs) for pr, vs in zip(prods, v_st)]
    from_state = [_dot_nt(jnp.concatenate([a_t[:, sl], r_t[:, sl]], axis=0), s_ref[p])
                  for p, sl in zip(pairs, lanes)]
    u_st = [_dot(t, stack(fs[0:c]) + uo) for t, fs, uo in zip(t_inv, from_state, u_own)]
    y_st = [_dot(jnp.where(incl2, pr[n2:2 * n2], 0.0), jnp.concatenate([us, vs], axis=0))
            for pr, us, vs in zip(prods, u_st, v_st)]
    for p, sl in zip(pairs, lanes):
        y_ref[0, :, sl] = from_state[p][c:2 * c] + y_st[p][0:c] + y_st[p][c:n2]
        u = u_st[p][0:c] + u_st[p][c:n2]
        upd = _dot_tn(jnp.concatenate([u, v[:, sl]], axis=0),
                      jnp.concatenate([b_e[:, sl], k_e[:, sl]], axis=0))
        s_ref[p] = s_ref[p] * w_all[:, sl] + jnp.where(same, upd, 0.0)


def _rwkv_scan(st, r, lw, kd, v, kk, ab, direction):
    d = st.d
    nc, ncc = st.t // CHUNK, st.ctx // CHUNK
    if direction == 0:
        def cidx(s):
            return s
    else:
        def cidx(s):
            return jnp.where(s < ncc, ncc - 1 - s, nc - 1 - s + ncc)
    shared = pl.BlockSpec((1, CHUNK, d), lambda b, s: (b, cidx(s), 0))
    per_dir = pl.BlockSpec((1, 1, CHUNK, d), lambda b, s: (direction, b, cidx(s), 0))
    return pl.pallas_call(
        functools.partial(_scan_body, reverse=direction == 1),
        grid=(st.b, nc),
        in_specs=[shared, per_dir, per_dir, shared, shared, per_dir],
        out_specs=shared,
        out_shape=st.out_shape(),
        scratch_shapes=[pltpu.VMEM((d // LANES, LANES, LANES), F32)],
        compiler_params=_cparams("parallel", "arbitrary"),
        name="rwkv_scan_bwd" if direction else "rwkv_scan_fwd",
    )(r, lw, kd, v, kk, ab)


def _rwkv_out_body(x_ref, mod_ref, g_ref, yf_ref, yb_ref, r_ref, kd_ref, v_ref, gate_ref,
                   lnw_ref, lnb_ref, rk_ref, wo_ref, ones_ref, o_ref):
    r, v = r_ref[0], v_ref[0]
    inv_n = 1.0 / RWKV_HEAD
    o = None
    for dr, y_ref in enumerate((yf_ref, yb_ref)):
        y = y_ref[0]
        cen = y - _seg_sum(y, ones_ref) * inv_n
        var = _seg_sum(cen * cen, ones_ref) * inv_n
        yn = cen * lax.rsqrt(var + GN_EPS) * lnw_ref[dr] + lnb_ref[dr]
        bonus = _seg_sum(r * kd_ref[dr, 0] * rk_ref[...], ones_ref) * v
        od = (yn + bonus) * gate_ref[dr, 0]
        o = od if o is None else o + od
    y_mix = _dot(o, wo_ref[...])
    m = mod_ref[0]
    o_ref[0] = x_ref[0] + m[5:6] * _rms(y_mix, g_ref[...][3:4])


def _rwkv_out(st, x, mod, g, y_f, y_b, r, kd, v, gate, p, seg):
    d = st.d
    dir_spec = pl.BlockSpec((2, 1, TILE, d), lambda b, t: (0, b, t, 0))
    consts = [p["ln_w"], p["ln_b"], p["r_k"], p["wo"]] + [seg]
    return pl.pallas_call(
        _rwkv_out_body,
        grid=(st.b, st.n_tiles),
        in_specs=[st.tile_spec(), st.mod_spec(), _const_spec(g.shape), st.tile_spec(), st.tile_spec(),
                  st.tile_spec(), dir_spec, st.tile_spec(), dir_spec] + [_const_spec(c.shape) for c in consts],
        out_specs=st.tile_spec(),
        out_shape=st.out_shape(),
        compiler_params=_cparams("parallel", "parallel"),
        name="rwkv_readout",
    )(x, mod, g, y_f, y_b, r, kd, v, gate, *consts)


def _qkv_body(x_ref, mod_ref, g_ref, wq_ref, wqs_ref, wk_ref, wks_ref, wv_ref, gq_ref, gk_ref,
              cos_ref, sin_ref, q_out, k_out, v_out, *, qk_norm, q_scale):
    m = mod_ref[0]
    n = _norm_mod(x_ref[0], g_ref[...][2:3], m[4:5], m[3:4]).astype(BF16)
    cos, sin = cos_ref[...], sin_ref[...]

    def rotated(w_ref, ws_ref, gain_ref, out_ref, scale):
        t_all = jnp.dot(n, w_ref[...], preferred_element_type=F32)
        ts_all = jnp.dot(n, ws_ref[...], preferred_element_type=F32)
        for h in range(out_ref.shape[1]):
            t = t_all[:, h * LANES:h * LANES + HEAD_DIM]
            ts = ts_all[:, h * LANES:h * LANES + HEAD_DIM]
            if qk_norm:
                inv = lax.rsqrt(jnp.mean(t * t, axis=-1, keepdims=True) + NORM_EPS)
                gain = gain_ref[...]
                t, ts = t * inv * gain[0:1], ts * inv * gain[1:2]
            out = t * cos + ts * sin
            out_ref[0, h] = (out * scale if scale != 1.0 else out).astype(BF16)

    rotated(wq_ref, wqs_ref, gq_ref, q_out, q_scale)
    rotated(wk_ref, wks_ref, gk_ref, k_out, 1.0)
    v_all = jnp.dot(n, wv_ref[...], preferred_element_type=F32)
    ones_tail = (lax.broadcasted_iota(jnp.int32, (1, LANES), 1) >= HEAD_DIM).astype(F32)
    for h in range(v_out.shape[1]):
        v_out[0, h] = (v_all[:, h * LANES:(h + 1) * LANES] + ones_tail).astype(BF16)


def _qkv_project(st, x, mod, g, a, rope, qk_norm):
    nh, nkv = a["wq"].shape[1] // LANES, a["wk"].shape[1] // LANES
    consts = [a["wq"], a["wq_sw"], a["wk"], a["wk_sw"], a["wv"], a["gq"], a["gk"]]
    rope_spec = pl.BlockSpec((TILE, HEAD_DIM), lambda b, t: (t, 0))

    def head_spec(n, width):
        return pl.BlockSpec((1, n, TILE, width), lambda b, t: (b, 0, t, 0))

    def head_shape(n, width):
        return jax.ShapeDtypeStruct((st.b, n, st.t, width), BF16)

    return pl.pallas_call(
        functools.partial(_qkv_body, qk_norm=qk_norm, q_scale=HEAD_DIM ** -0.5 * LOG2_E),
        grid=(st.b, st.n_tiles),
        in_specs=[st.tile_spec(), st.mod_spec(), _const_spec(g.shape)]
                 + [_const_spec(c.shape) for c in consts] + [rope_spec, rope_spec],
        out_specs=[head_spec(nh, HEAD_DIM), head_spec(nkv, HEAD_DIM), head_spec(nkv, LANES)],
        out_shape=[head_shape(nh, HEAD_DIM), head_shape(nkv, HEAD_DIM), head_shape(nkv, LANES)],
        compiler_params=_cparams("parallel", "parallel"),
        name="qkv_rope",
    )(x, mod, g, *consts, *rope)


def _key_chunks(k0, k1):
    n = -(-(k1 - k0) // MAX_KEYS)
    size = -(-(k1 - k0) // (n * LANES)) * LANES
    return [(a, min(size, k1 - a)) for a in range(k0, k1, size)]


def _attn_body(*refs, ctx_len, n_ctx_tiles, total, windowed):
    if windowed:
        sink_ref, q_ref, k_ref, v_ref, o_ref = refs
    else:
        q_ref, k_ref, v_ref, o_ref = refs
    group = q_ref.shape[1]
    rows = group * TILE
    t = pl.program_id(2)
    q = q_ref[0].reshape(rows, HEAD_DIM)
    tail = lax.broadcasted_iota(jnp.int32, (1, LANES), 1) >= HEAD_DIM

    def scores(k0, size):
        return _dot_nt(q, k_ref[0, 0, pl.ds(k0, size), :])

    def weigh(p, k0, size):
        return jnp.dot(p.astype(BF16), v_ref[0, 0, pl.ds(k0, size), :], preferred_element_type=F32)

    def finish(acc):
        inv = 1.0 / jnp.where(tail, acc, 1.0)
        return (acc * pltpu.roll(inv, HEAD_DIM, 1))[:, 0:HEAD_DIM]

    if windowed:
        kv = pl.program_id(1)
        sink = jnp.concatenate([jnp.full((TILE, 1), sink_ref[kv * group + gi], F32) for gi in range(group)],
                               axis=0) * LOG2_E

        def with_sink(s_parts, spans):
            m = sink
            for s in s_parts:
                m = jnp.maximum(m, jnp.max(s, axis=-1, keepdims=True))
            acc = jnp.where(tail, jnp.exp2(sink - m), 0.0)
            for s, (k0, size) in zip(s_parts, spans):
                acc = acc + weigh(jnp.exp2(s - m), k0, size)
            return finish(acc)

        def context_queries():
            return with_sink([scores(0, ctx_len)], [(0, ctx_len)])

        def latent_queries():
            k0 = pl.multiple_of(jnp.clip((t - 1) * TILE, 0, total - LOCAL_KEYS), TILE)
            k_pos = k0 + lax.broadcasted_iota(jnp.int32, (TILE, LOCAL_KEYS), 1)
            q_pos = t * TILE + lax.broadcasted_iota(jnp.int32, (TILE, LOCAL_KEYS), 0)
            mask = jnp.logical_and(jnp.abs(k_pos - q_pos) <= WINDOW, k_pos >= ctx_len)
            s_loc = scores(k0, LOCAL_KEYS).reshape(group, TILE, LOCAL_KEYS)
            s_loc = jnp.where(mask[None], s_loc, NEG_INF).reshape(rows, LOCAL_KEYS)
            return with_sink([scores(0, ctx_len), s_loc], [(0, ctx_len), (k0, LOCAL_KEYS)])
    else:
        def online(spans):
            m = acc = None
            for k0, size in spans:
                s = scores(k0, size)
                m_chunk = jnp.max(s, axis=-1, keepdims=True)
                if m is None:
                    m, acc = m_chunk, weigh(jnp.exp2(s - m_chunk), k0, size)
                else:
                    m_new = jnp.maximum(m, m_chunk)
                    acc = jnp.exp2(m - m_new) * acc + weigh(jnp.exp2(s - m_new), k0, size)
                    m = m_new
            return finish(acc)

        def context_queries():
            return online([(0, ctx_len)])

        def latent_queries():
            return online(_key_chunks(0, total))

    out = lax.cond(t < n_ctx_tiles, context_queries, latent_queries)
    o_ref[0] = out.reshape(group, TILE, HEAD_DIM).astype(BF16)


def _attention(st, q, k, v, sink):
    nh, nkv = q.shape[1], k.shape[1]
    group = nh // nkv
    windowed = sink is not None
    assert st.t >= LOCAL_KEYS
    q_spec = pl.BlockSpec((1, group, TILE, HEAD_DIM), lambda b, h, t: (b, h, t, 0))
    k_spec = pl.BlockSpec((1, 1, st.t, HEAD_DIM), lambda b, h, t: (b, h, 0, 0))
    v_spec = pl.BlockSpec((1, 1, st.t, LANES), lambda b, h, t: (b, h, 0, 0))
    specs, args = [q_spec, k_spec, v_spec], [q, k, v]
    if windowed:
        specs = [pl.BlockSpec(memory_space=pltpu.SMEM)] + specs
        args = [sink] + args
    return pl.pallas_call(
        functools.partial(_attn_body, ctx_len=st.ctx, n_ctx_tiles=st.n_ctx_tiles, total=st.t,
                          windowed=windowed),
        grid=(st.b, nkv, st.n_tiles),
        in_specs=specs,
        out_specs=q_spec,
        out_shape=jax.ShapeDtypeStruct(q.shape, BF16),
        compiler_params=_cparams("parallel", "parallel", "parallel"),
        name="window_attention" if windowed else "global_attention",
    )(*args)


def _attn_out_body(x_ref, mod_ref, g_ref, o_ref, wo_ref, out_ref):
    y = None
    for h in range(o_ref.shape[1]):
        part = jnp.dot(o_ref[0, h], wo_ref[h], preferred_element_type=F32)
        y = part if y is None else y + part
    m = mod_ref[0]
    out_ref[0] = x_ref[0] + m[5:6] * _rms(y, g_ref[...][3:4])


def _attn_out(st, x, mod, g, o, wo):
    nh = o.shape[1]
    return pl.pallas_call(
        _attn_out_body,
        grid=(st.b, st.n_tiles),
        in_specs=[st.tile_spec(), st.mod_spec(), _const_spec(g.shape),
                  pl.BlockSpec((1, nh, TILE, HEAD_DIM), lambda b, t: (b, 0, t, 0)), _const_spec(wo.shape)],
        out_specs=st.tile_spec(),
        out_shape=st.out_shape(),
        compiler_params=_cparams("parallel", "parallel"),
        name="attention_out",
    )(x, mod, g, o, wo)


def _pad_to(w, axis, mult):
    pad = (-w.shape[axis]) % mult
    if pad == 0:
        return w
    widths = [(0, 0)] * w.ndim
    widths[axis] = (0, pad)
    return jnp.pad(w, widths)


def _rope_tables(st):
    rows = (st.t - st.ctx) // GRID_W
    row = jnp.broadcast_to(jnp.arange(rows)[:, None], (rows, GRID_W)).reshape(-1)
    col = jnp.broadcast_to(jnp.arange(GRID_W)[None, :], (rows, GRID_W)).reshape(-1)
    inv_freq = ROPE_THETA ** (-jnp.arange(N_FREQ, dtype=F32) / N_FREQ)
    ang = jnp.stack([row, col], axis=-1).astype(F32)[:, :, None] * inv_freq
    cos, sin = jnp.cos(ang), jnp.sin(ang)
    cos_t = jnp.stack([cos, cos], axis=2).reshape(-1, HEAD_DIM)
    sin_t = jnp.stack([-sin, sin], axis=2).reshape(-1, HEAD_DIM)
    cos_t = jnp.concatenate([jnp.ones((st.ctx, HEAD_DIM), F32), cos_t], axis=0)
    sin_t = jnp.concatenate([jnp.zeros((st.ctx, HEAD_DIM), F32), sin_t], axis=0)
    return cos_t, sin_t


def _partner_perm():
    idx = np.arange(HEAD_DIM).reshape(2, 2, N_FREQ)
    return idx[:, ::-1, :].reshape(-1)


def _lane_groups(w):
    d = w.shape[0]
    w = w.reshape(d, -1, HEAD_DIM)
    return jnp.pad(w, ((0, 0), (0, 0), (0, LANES - HEAD_DIM))).reshape(d, -1).astype(BF16)


def _attn_params(wq, wk, wv, wo, gq, gk):
    nh = wq.shape[1] // HEAD_DIM
    perm = _partner_perm()

    def swapped(w):
        return w.reshape(w.shape[0], -1, HEAD_DIM)[:, :, perm].reshape(w.shape)

    return {"wq": _lane_groups(wq), "wq_sw": _lane_groups(swapped(wq)),
            "wk": _lane_groups(wk), "wk_sw": _lane_groups(swapped(wk)), "wv": _lane_groups(wv),
            "wo": wo.reshape(nh, HEAD_DIM, wo.shape[1]).astype(BF16),
            "gq": jnp.stack([gq, gq[perm]]), "gk": jnp.stack([gk, gk[perm]])}


def _segment_ones():
    head = np.arange(SEG_TILE) // RWKV_HEAD
    return jnp.asarray(head[:, None] == head[None, :], dtype=BF16)


def kernel(x, c, ctx, c_ctx, mod_w, mod_b, norm_g, ffn_w1, ffn_w3, ffn_w2, rwkv_mu, rwkv_wr, rwkv_wk, rwkv_wv, rwkv_wo, rwkv_k_k, rwkv_k_a, rwkv_r_k, rwkv_w0, rwkv_w1, rwkv_w2, rwkv_a0, rwkv_a1, rwkv_a2, rwkv_g1, rwkv_g2, rwkv_ln_w, rwkv_ln_b, rwkv_v0, rwkv_v1, rwkv_v2, gattn_wq, gattn_wk, gattn_wv, gattn_wo, gattn_q_norm, gattn_k_norm, wattn_wq, wattn_wk, wattn_wv, wattn_wo, wattn_sink):
    batch, seq, d = x.shape
    depth = mod_w.shape[0]
    st = _Stream(batch, ctx.shape[1], seq, d)

    rows = -(-(batch + 1) // 8) * 8
    c_rows = jnp.zeros((rows, d), F32).at[:batch].set(c).at[batch].set(c_ctx)
    mods = _modulation(c_rows, mod_w, mod_b).reshape(depth, rows, N_MOD, d)

    w1, w3, w2 = ffn_w1.astype(BF16), ffn_w3.astype(BF16), ffn_w2.astype(BF16)
    rope = _rope_tables(st)
    seg = _segment_ones()
    bf = lambda w: w.astype(BF16)

    h = jnp.concatenate([ctx, x], axis=1)
    v_first = None
    for i in range(depth):
        kind, j = i % 3, i // 3
        g, mod = norm_g[i], mods[i]
        h = _ffn_half(st, h, mod, g, w1[i, 0], w3[i, 0], w2[i, 0], 0, 0)
        if kind == 0:
            p = {"mu": rwkv_mu[j], "wr": bf(rwkv_wr[j]), "wk": bf(rwkv_wk[j]), "wv": bf(rwkv_wv[j]),
                 "wo": bf(rwkv_wo[j]), "k_k": rwkv_k_k[j][None], "k_a": rwkv_k_a[j][None],
                 "r_k": rwkv_r_k[j].reshape(1, d),
                 "w0": rwkv_w0[j][:, None], "w1": bf(_pad_to(rwkv_w1[j], 2, LANES)), "w2": bf(_pad_to(rwkv_w2[j], 1, LANES)),
                 "a0": rwkv_a0[j][:, None], "a1": bf(_pad_to(rwkv_a1[j], 2, LANES)), "a2": bf(_pad_to(rwkv_a2[j], 1, LANES)),
                 "g1": bf(_pad_to(rwkv_g1[j], 2, LANES)), "g2": bf(_pad_to(rwkv_g2[j], 1, LANES)),
                 "ln_w": rwkv_ln_w[j][:, None], "ln_b": rwkv_ln_b[j][:, None]}
            vres = None
            if j > 0:
                vres = (v_first, rwkv_v0[j - 1][None], bf(_pad_to(rwkv_v1[j - 1], 1, LANES)),
                        bf(_pad_to(rwkv_v2[j - 1], 0, LANES)))
            r, v, kk, lw, kd, ab, gate = _rwkv_features(st, h, mod, g, p, seg, vres)
            if j == 0:
                v_first = v
            y_f = _rwkv_scan(st, r, lw, kd, v, kk, ab, 0)
            y_b = _rwkv_scan(st, r, lw, kd, v, kk, ab, 1)
            h = _rwkv_out(st, h, mod, g, y_f, y_b, r, kd, v, gate, p, seg)
        else:
            if kind == 1:
                a = _attn_params(gattn_wq[j], gattn_wk[j], gattn_wv[j], gattn_wo[j],
                                 gattn_q_norm[j], gattn_k_norm[j])
                sink = None
            else:
                ones = jnp.ones((HEAD_DIM,), F32)
                a = _attn_params(wattn_wq[j], wattn_wk[j], wattn_wv[j], wattn_wo[j], ones, ones)
                sink = wattn_sink[j]
            q, k, v = _qkv_project(st, h, mod, g, a, rope, qk_norm=kind == 1)
            o = _attention(st, q, k, v, sink)
            h = _attn_out(st, h, mod, g, o, a["wo"])
        h = _ffn_half(st, h, mod, g, w1[i, 1], w3[i, 1], w2[i, 1], 6, 4)
    return h[:, st.ctx:]
```

```python
import functools

import numpy as np
import jax
import jax.numpy as jnp
from jax import lax
from jax.experimental import pallas as pl
from jax.experimental.pallas import tpu as pltpu

F32 = jnp.float32
BF16 = jnp.bfloat16
HIGHEST = lax.Precision.HIGHEST

GRID_W = 64
N_MOD = 9
HEAD_DIM = 64
N_KV_HEADS = 4
WINDOW = 128
ROPE_THETA = 10000.0
N_FREQ = HEAD_DIM // 4
RWKV_HEAD = 64
NORM_EPS = 1e-6
GN_EPS = 64e-5
L2_EPS = 1e-12
NEG_INF = -1e30
LOG2_E = 1.4426950408889634

LANES = 128
TILE = 256
FFN_SEQS = 2
CHUNK = 64
MAX_KEYS = 1152
SEG_TILE = 256
LOCAL_KEYS = 3 * TILE
VMEM_LIMIT = 56 * 1024 * 1024


def _cparams(*sem):
    return pltpu.CompilerParams(dimension_semantics=sem, vmem_limit_bytes=VMEM_LIMIT)


def _const_spec(shape):
    nd = len(shape)
    return pl.BlockSpec(shape, lambda *_: (0,) * nd, pipeline_mode=pl.Buffered(1))


def _dot(a, b):
    return jnp.dot(a.astype(BF16), b.astype(BF16), preferred_element_type=F32)


def _dot_nt(a, b):
    return lax.dot_general(a.astype(BF16), b.astype(BF16), (((1,), (1,)), ((), ())),
                           preferred_element_type=F32)


def _dot_tn(a, b):
    return lax.dot_general(a.astype(BF16), b.astype(BF16), (((0,), (0,)), ((), ())),
                           preferred_element_type=F32)


def _dot_f32(a, b):
    return jnp.dot(a, b, precision=HIGHEST, preferred_element_type=F32)


def _sigmoid(x):
    return 1.0 / (1.0 + jnp.exp(-x))


def _rms(x, g):
    return x * lax.rsqrt(jnp.mean(x * x, axis=-1, keepdims=True) + NORM_EPS) * g


def _norm_mod(x, g, scale, shift):
    return _rms(x, g) * (1.0 + scale) + shift


def _seg_sum(x, ones_ref):
    ones = ones_ref[...]
    hi = x.astype(BF16)
    rest = x - hi.astype(F32)
    mid = rest.astype(BF16)
    lo = (rest - mid.astype(F32)).astype(BF16)
    cols = []
    for j in range(x.shape[1] // SEG_TILE):
        sl = slice(j * SEG_TILE, (j + 1) * SEG_TILE)
        cols.append(sum(jnp.dot(part[:, sl], ones, preferred_element_type=F32) for part in (hi, mid, lo)))
    return jnp.concatenate(cols, axis=1)


def _mod_body(c_ref, w_ref, b_ref, o_ref):
    c = c_ref[...]
    o_ref[0] = _dot_f32(c * _sigmoid(c), w_ref[0]) + b_ref[0]


def _modulation(c_rows, mod_w, mod_b):
    depth, d, nd = mod_w.shape
    rows = c_rows.shape[0]
    tn = nd // 6
    return pl.pallas_call(
        _mod_body,
        grid=(depth, nd // tn),
        in_specs=[_const_spec((rows, d)),
                  pl.BlockSpec((1, d, tn), lambda i, j: (i, 0, j)),
                  pl.BlockSpec((1, 1, tn), lambda i, j: (i, 0, j))],
        out_specs=pl.BlockSpec((1, rows, tn), lambda i, j: (i, 0, j)),
        out_shape=jax.ShapeDtypeStruct((depth, rows, nd), F32),
        compiler_params=_cparams("arbitrary", "arbitrary"),
        name="adaln_rows",
    )(c_rows, mod_w, mod_b.reshape(depth, 1, nd))


class _Stream:
    def __init__(self, batch, ctx_len, seq, d):
        assert ctx_len % TILE == 0 and seq % TILE == 0 and d % LANES == 0
        self.b, self.t, self.d = batch, ctx_len + seq, d
        self.ctx = ctx_len
        self.n_tiles = self.t // TILE
        self.n_ctx_tiles = ctx_len // TILE

    def tile_spec(self, width=None):
        return pl.BlockSpec((1, TILE, width or self.d), lambda b, t: (b, t, 0))

    def mod_spec(self):
        nct, ctx_row = self.n_ctx_tiles, self.b
        return pl.BlockSpec((1, N_MOD, self.d), lambda b, t: (jnp.where(t < nct, ctx_row, b), 0, 0))

    def out_shape(self, width=None, dtype=F32):
        return jax.ShapeDtypeStruct((self.b, self.t, width or self.d), dtype)


def _ffn_body(x_ref, mod_ref, g_ref, w1_ref, w3_ref, w2_ref, o_ref, *, mod_row, g_row):
    x = x_ref[...]
    m = mod_ref[...]
    g = g_ref[...]
    shift, scale, gate = m[:, mod_row:mod_row + 1], m[:, mod_row + 1:mod_row + 2], m[:, mod_row + 2:mod_row + 3]
    n = _norm_mod(x, g[g_row:g_row + 1], scale, shift).astype(BF16).reshape(-1, x.shape[-1])
    a = jnp.dot(n, w1_ref[...], preferred_element_type=F32)
    b = jnp.dot(n, w3_ref[...], preferred_element_type=F32)
    h = (a * _sigmoid(a) * b).astype(BF16)
    y = jnp.dot(h, w2_ref[...], preferred_element_type=F32).reshape(x.shape)
    o_ref[...] = x + 0.5 * gate * _rms(y, g[g_row + 1:g_row + 2])


def _ffn_half(st, x, mod, g, w1, w3, w2, mod_row, g_row, latent_only=False):
    d, f = w1.shape
    nb = FFN_SEQS if st.b % FFN_SEQS == 0 else 1
    nct, ctx_blk = st.n_ctx_tiles, st.b // nb
    skip = nct if latent_only else 0
    x_spec = pl.BlockSpec((nb, TILE, d), lambda b, t: (b, t + skip, 0))
    mod_spec = pl.BlockSpec((nb, N_MOD, d), lambda b, t: (jnp.where(t + skip < nct, ctx_blk, b), 0, 0))
    out_t = st.t - skip * TILE
    return pl.pallas_call(
        functools.partial(_ffn_body, mod_row=mod_row, g_row=g_row),
        grid=(st.b // nb, st.n_tiles - skip),
        in_specs=[x_spec, mod_spec, _const_spec(g.shape),
                  _const_spec((d, f)), _const_spec((d, f)), _const_spec((f, d))],
        out_specs=pl.BlockSpec((nb, TILE, d), lambda b, t: (b, t, 0)),
        out_shape=jax.ShapeDtypeStruct((st.b, out_t, d), F32),
        compiler_params=_cparams("parallel", "parallel"),
        name="swiglu_half",
    )(x, mod, g, w1, w3, w2)


def _softplus(y):
    return jnp.maximum(y, 0.0) + jnp.log(1.0 + jnp.exp(-jnp.abs(y)))


def _rwkv_feat_body(*refs, n_ctx_tiles, n_tiles, with_vres):
    (x_ref, xp_ref, xn_ref, mod_ref, g_ref, mu_ref, wr_ref, wk_ref, wv_ref, kk_ref, ka_ref,
     w0_ref, w1_ref, w2_ref, a0_ref, a1_ref, a2_ref, g1_ref, g2_ref, ones_ref) = refs[:20]
    rest = refs[20:]
    if with_vres:
        vf_ref, v0_ref, v1_ref, v2_ref = rest[:4]
        rest = rest[4:]
    r_out, v_out, kk_out, lw_out, kd_out, ab_out, g_out = rest

    t = pl.program_id(1)
    m = mod_ref[0]
    gn = g_ref[...][2:3]
    shift, scale = m[3:4], m[4:5]
    n = _norm_mod(x_ref[0], gn, scale, shift)
    first = jnp.logical_or(t == 0, t == n_ctx_tiles)
    last = jnp.logical_or(t == n_ctx_tiles - 1, t == n_tiles - 1)
    n_before = jnp.where(first, 0.0, _norm_mod(xp_ref[0][7:8], gn, scale, shift))
    n_after = jnp.where(last, 0.0, _norm_mod(xn_ref[0][0:1], gn, scale, shift))
    row = lax.broadcasted_iota(jnp.int32, (TILE, 1), 0)
    n_prev = jnp.where(row == 0, n_before, pltpu.roll(n, 1, 0))
    n_next = jnp.where(row == TILE - 1, n_after, pltpu.roll(n, TILE - 1, 0))
    xx = 0.5 * (n_prev + n_next) - n
    mu = mu_ref[...]
    xr, xw, xk, xv, xa, xg = (n + xx * mu[i:i + 1] for i in range(6))

    r = _dot(xr, wr_ref[...])
    k = _dot(xk, wk_ref[...])
    v = _dot(xv, wv_ref[...])
    if with_vres:
        lora = _dot(_dot(xv, v1_ref[...]), v2_ref[...])
        v = v + (vf_ref[0] - v) * _sigmoid(v0_ref[...] + lora)
    kk = k * kk_ref[...]
    kk = kk / jnp.maximum(jnp.sqrt(_seg_sum(kk * kk, ones_ref)), L2_EPS)
    r_out[0] = r.astype(BF16)
    v_out[0] = v
    kk_out[0] = kk.astype(BF16)
    k_a = ka_ref[...]
    for dr in range(2):
        wl = w0_ref[dr] + _dot(jnp.tanh(_dot(xw, w1_ref[dr])), w2_ref[dr])
        w = -_softplus(-wl) - 0.5
        a = _sigmoid(a0_ref[dr] + _dot(_dot(xa, a1_ref[dr]), a2_ref[dr]))
        lw_out[dr, 0] = -jnp.exp(w)
        kd_out[dr, 0] = (k * (1.0 + (a - 1.0) * k_a)).astype(BF16)
        ab_out[dr, 0] = (kk * a).astype(BF16)
        g_out[dr, 0] = _dot(_sigmoid(_dot(xg, g1_ref[dr])), g2_ref[dr]).astype(BF16)


def _rwkv_features(st, x, mod, g, p, seg, vres):
    d = st.d
    n8 = st.t // 8
    per8 = TILE // 8
    prev_spec = pl.BlockSpec((1, 8, d), lambda b, t: (b, jnp.maximum(t * per8 - 1, 0), 0))
    next_spec = pl.BlockSpec((1, 8, d), lambda b, t: (b, jnp.minimum((t + 1) * per8, n8 - 1), 0))
    dir_spec = pl.BlockSpec((2, 1, TILE, d), lambda b, t: (0, b, t, 0))
    dir_f32 = jax.ShapeDtypeStruct((2, st.b, st.t, d), F32)
    dir_bf16 = jax.ShapeDtypeStruct((2, st.b, st.t, d), BF16)
    names = ("mu", "wr", "wk", "wv", "k_k", "k_a", "w0", "w1", "w2", "a0", "a1", "a2", "g1", "g2")
    consts = [p[nm] for nm in names] + [seg]
    args = [x, x, x, mod, g] + consts
    specs = [st.tile_spec(), prev_spec, next_spec, st.mod_spec(), _const_spec(g.shape)]
    specs += [_const_spec(c.shape) for c in consts]
    if vres is not None:
        v_first, v0, v1, v2 = vres
        args += [v_first, v0, v1, v2]
        specs += [st.tile_spec(), _const_spec(v0.shape), _const_spec(v1.shape), _const_spec(v2.shape)]
    return pl.pallas_call(
        functools.partial(_rwkv_feat_body, n_ctx_tiles=st.n_ctx_tiles, n_tiles=st.n_tiles,
                          with_vres=vres is not None),
        grid=(st.b, st.n_tiles),
        in_specs=specs,
        out_specs=[st.tile_spec()] * 3 + [dir_spec] * 4,
        out_shape=[st.out_shape(dtype=BF16), st.out_shape(), st.out_shape(dtype=BF16),
                   dir_f32, dir_bf16, dir_bf16, dir_bf16],
        compiler_params=_cparams("parallel", "parallel"),
        name="rwkv_features",
    )(*args)


def _unit_triangular_inverses(mats, ri, ci, size):
    eye = (ri == ci).astype(F32)
    ts = [jnp.where((ri >> 1) == (ci >> 1), a, 0.0) + eye for a in mats]
    level = 1
    while (2 << level) <= size:
        off = jnp.logical_and((ri >> (level + 1)) == (ci >> (level + 1)), (ri >> level) != (ci >> level))
        ws = [_dot(t, jnp.where(off, a, 0.0)) for t, a in zip(ts, mats)]
        ts = [t + _dot(w, t) for t, w in zip(ts, ws)]
        level += 1
    return ts


def _scan_body(r_ref, lw_ref, k_ref, v_ref, kk_ref, ab_ref, y_ref, s_ref, *, reverse):
    c = CHUNK
    n2 = 2 * c

    @pl.when(pl.program_id(1) == 0)
    def _():
        s_ref[...] = jnp.zeros_like(s_ref)

    r, k, kk, ab = (ref[...].reshape(ref.shape[-2:]).astype(F32) for ref in (r_ref, k_ref, kk_ref, ab_ref))
    lw, v = lw_ref[0, 0], v_ref[0]
    ti = lax.broadcasted_iota(jnp.int32, (c, c), 0)
    si = lax.broadcasted_iota(jnp.int32, (c, c), 1)
    upto = (si >= ti) if reverse else (si <= ti)
    cum = _dot_f32(upto.astype(F32), lw)
    tot = cum[0:1] if reverse else cum[c - 1:c]
    a_t = -kk * jnp.exp(cum - lw)
    r_t = r * jnp.exp(cum)
    inv = jnp.exp(-cum)
    b_t = ab * inv
    k_t = k * inv
    to_end = jnp.exp(tot - cum)
    b_e = ab * to_end
    k_e = k * to_end
    w_all = jnp.exp(tot)

    ri = lax.broadcasted_iota(jnp.int32, (n2, n2), 0)
    ci = lax.broadcasted_iota(jnp.int32, (n2, n2), 1)
    same = (ri >> 6) == (ci >> 6)
    tt, ss = ri & (c - 1), ci & (c - 1)
    before = (ss > tt) if reverse else (ss < tt)
    strict = jnp.logical_and(same, before)
    incl = jnp.logical_and(same, jnp.logical_or(before, ss == tt))
    incl2 = jnp.concatenate([incl, incl], axis=1)
    even = lax.broadcasted_iota(jnp.int32, (1, LANES), 1) < RWKV_HEAD

    def stack(x):
        return jnp.concatenate([jnp.where(even, x, 0.0), jnp.where(even, 0.0, x)], axis=0)

    pairs = range(r.shape[1] // LANES)
    lanes = [slice(p * LANES, (p + 1) * LANES) for p in pairs]
    prods = [_dot_nt(jnp.concatenate([stack(a_t[:, sl]), stack(r_t[:, sl])], axis=0),
                     jnp.concatenate([stack(b_t[:, sl]), stack(k_t[:, sl])], axis=0)) for sl in lanes]
    a_b = [jnp.where(strict, pr[0:n2, 0:n2], 0.0) for pr in prods]
    t_inv = _unit_triangular_inverses(a_b, ri, ci, c)
    v_st = [stack(v[:, sl]) for sl in lanes]
    u_own = [_dot(jnp.where(strict, pr[0:n2, n2:2 * n2], 0.0), vs) for pr, vs in zip(prods, v_st)]
    from_state = [_dot_nt(jnp.concatenate([a_t[:, sl], r_t[:, sl]], axis=0), s_ref[p])
                  for p, sl in zip(pairs, lanes)]
    u_st = [_dot(t, stack(fs[0:c]) + uo) for t, fs, uo in zip(t_inv, from_state, u_own)]
    y_st = [_dot(jnp.where(incl2, pr[n2:2 * n2], 0.0), jnp.concatenate([us, vs], axis=0))
            for pr, us, vs in zip(prods, u_st, v_st)]
    for p, sl in zip(pairs, lanes):
        y_ref[0, :, sl] = from_state[p][c:2 * c] + y_st[p][0:c] + y_st[p][c:n2]
        u = u_st[p][0:c] + u_st[p][c:n2]
        upd = _dot_tn(jnp.concatenate([u, v[:, sl]], axis=0),
                      jnp.concatenate([b_e[:, sl], k_e[:, sl]], axis=0))
        s_ref[p] = s_ref[p] * w_all[:, sl] + jnp.where(same, upd, 0.0)


def _rwkv_scan(st, r, lw, kd, v, kk, ab, direction):
    d = st.d
    nc, ncc = st.t // CHUNK, st.ctx // CHUNK
    if direction == 0:
        def cidx(s):
            return s
    else:
        def cidx(s):
            return jnp.where(s < ncc, ncc - 1 - s, nc - 1 - s + ncc)
    shared = pl.BlockSpec((1, CHUNK, d), lambda b, s: (b, cidx(s), 0))
    per_dir = pl.BlockSpec((1, 1, CHUNK, d), lambda b, s: (direction, b, cidx(s), 0))
    return pl.pallas_call(
        functools.partial(_scan_body, reverse=direction == 1),
        grid=(st.b, nc),
        in_specs=[shared, per_dir, per_dir, shared, shared, per_dir],
        out_specs=shared,
        out_shape=st.out_shape(),
        scratch_shapes=[pltpu.VMEM((d // LANES, LANES, LANES), F32)],
        compiler_params=_cparams("parallel", "arbitrary"),
        name="rwkv_scan_bwd" if direction else "rwkv_scan_fwd",
    )(r, lw, kd, v, kk, ab)


def _rwkv_out_body(x_ref, mod_ref, g_ref, yf_ref, yb_ref, r_ref, kd_ref, v_ref, gate_ref,
                   lnw_ref, lnb_ref, rk_ref, wo_ref, ones_ref, o_ref):
    r, v = r_ref[0].astype(F32), v_ref[0]
    inv_n = 1.0 / RWKV_HEAD
    o = None
    for dr, y_ref in enumerate((yf_ref, yb_ref)):
        y = y_ref[0]
        cen = y - _seg_sum(y, ones_ref) * inv_n
        var = _seg_sum(cen * cen, ones_ref) * inv_n
        yn = cen * lax.rsqrt(var + GN_EPS) * lnw_ref[dr] + lnb_ref[dr]
        bonus = _seg_sum(r * kd_ref[dr, 0].astype(F32) * rk_ref[...], ones_ref) * v
        od = (yn + bonus) * gate_ref[dr, 0].astype(F32)
        o = od if o is None else o + od
    y_mix = _dot(o, wo_ref[...])
    m = mod_ref[0]
    o_ref[0] = x_ref[0] + m[5:6] * _rms(y_mix, g_ref[...][3:4])


def _rwkv_out(st, x, mod, g, y_f, y_b, r, kd, v, gate, p, seg):
    d = st.d
    dir_spec = pl.BlockSpec((2, 1, TILE, d), lambda b, t: (0, b, t, 0))
    consts = [p["ln_w"], p["ln_b"], p["r_k"], p["wo"]] + [seg]
    return pl.pallas_call(
        _rwkv_out_body,
        grid=(st.b, st.n_tiles),
        in_specs=[st.tile_spec(), st.mod_spec(), _const_spec(g.shape), st.tile_spec(), st.tile_spec(),
                  st.tile_spec(), dir_spec, st.tile_spec(), dir_spec] + [_const_spec(c.shape) for c in consts],
        out_specs=st.tile_spec(),
        out_shape=st.out_shape(),
        compiler_params=_cparams("parallel", "parallel"),
        name="rwkv_readout",
    )(x, mod, g, y_f, y_b, r, kd, v, gate, *consts)


def _qkv_body(x_ref, mod_ref, g_ref, wq_ref, wqs_ref, wk_ref, wks_ref, wv_ref, gq_ref, gk_ref,
              cos_ref, sin_ref, q_out, k_out, v_out, *, qk_norm, q_scale):
    m = mod_ref[0]
    n = _norm_mod(x_ref[0], g_ref[...][2:3], m[4:5], m[3:4]).astype(BF16)
    cos, sin = cos_ref[...], sin_ref[...]

    def rotated(w_ref, ws_ref, gain_ref, out_ref, scale):
        t_all = jnp.dot(n, w_ref[...], preferred_element_type=F32)
        ts_all = jnp.dot(n, ws_ref[...], preferred_element_type=F32)
        for h in range(out_ref.shape[1]):
            t = t_all[:, h * LANES:h * LANES + HEAD_DIM]
            ts = ts_all[:, h * LANES:h * LANES + HEAD_DIM]
            if qk_norm:
                inv = lax.rsqrt(jnp.mean(t * t, axis=-1, keepdims=True) + NORM_EPS)
                gain = gain_ref[...]
                t, ts = t * inv * gain[0:1], ts * inv * gain[1:2]
            out = t * cos + ts * sin
            out_ref[0, h] = (out * scale if scale != 1.0 else out).astype(BF16)

    rotated(wq_ref, wqs_ref, gq_ref, q_out, q_scale)
    rotated(wk_ref, wks_ref, gk_ref, k_out, 1.0)
    v_all = jnp.dot(n, wv_ref[...], preferred_element_type=F32)
    ones_tail = (lax.broadcasted_iota(jnp.int32, (1, LANES), 1) >= HEAD_DIM).astype(F32)
    for h in range(v_out.shape[1]):
        v_out[0, h] = (v_all[:, h * LANES:(h + 1) * LANES] + ones_tail).astype(BF16)


def _qkv_project(st, x, mod, g, a, rope, qk_norm):
    nh, nkv = a["wq"].shape[1] // LANES, a["wk"].shape[1] // LANES
    consts = [a["wq"], a["wq_sw"], a["wk"], a["wk_sw"], a["wv"], a["gq"], a["gk"]]
    rope_spec = pl.BlockSpec((TILE, HEAD_DIM), lambda b, t: (t, 0))

    def head_spec(n, width):
        return pl.BlockSpec((1, n, TILE, width), lambda b, t: (b, 0, t, 0))

    def head_shape(n, width):
        return jax.ShapeDtypeStruct((st.b, n, st.t, width), BF16)

    return pl.pallas_call(
        functools.partial(_qkv_body, qk_norm=qk_norm, q_scale=HEAD_DIM ** -0.5 * LOG2_E),
        grid=(st.b, st.n_tiles),
        in_specs=[st.tile_spec(), st.mod_spec(), _const_spec(g.shape)]
                 + [_const_spec(c.shape) for c in consts] + [rope_spec, rope_spec],
        out_specs=[head_spec(nh, HEAD_DIM), head_spec(nkv, HEAD_DIM), head_spec(nkv, LANES)],
        out_shape=[head_shape(nh, HEAD_DIM), head_shape(nkv, HEAD_DIM), head_shape(nkv, LANES)],
        compiler_params=_cparams("parallel", "parallel"),
        name="qkv_rope",
    )(x, mod, g, *consts, *rope)


def _key_chunks(k0, k1):
    n = -(-(k1 - k0) // MAX_KEYS)
    size = -(-(k1 - k0) // (n * LANES)) * LANES
    return [(a, min(size, k1 - a)) for a in range(k0, k1, size)]


def _attn_body(*refs, ctx_len, n_ctx_tiles, total, windowed):
    if windowed:
        sink_ref, q_ref, k_ref, v_ref, o_ref = refs
    else:
        q_ref, k_ref, v_ref, o_ref = refs
    group = q_ref.shape[1]
    rows = group * TILE
    t = pl.program_id(2)
    q = q_ref[0].reshape(rows, HEAD_DIM)
    tail = lax.broadcasted_iota(jnp.int32, (1, LANES), 1) >= HEAD_DIM

    def scores(k0, size):
        return _dot_nt(q, k_ref[0, 0, pl.ds(k0, size), :])

    def weigh(p, k0, size):
        return jnp.dot(p.astype(BF16), v_ref[0, 0, pl.ds(k0, size), :], preferred_element_type=F32)

    def finish(acc):
        inv = 1.0 / jnp.where(tail, acc, 1.0)
        cols = []
        for g0 in range(0, group, 2):
            lo = slice(g0 * TILE, (g0 + 1) * TILE)
            hi = slice((g0 + 1) * TILE, (g0 + 2) * TILE)
            first = acc[lo] * pltpu.roll(inv[lo], HEAD_DIM, 1)
            second = pltpu.roll(acc[hi], HEAD_DIM, 1) * inv[hi]
            cols.append(jnp.where(tail, second, first))
        return jnp.concatenate(cols, axis=1)

    if windowed:
        kv = pl.program_id(1)
        sink = jnp.concatenate([jnp.full((TILE, 1), sink_ref[kv * group + gi], F32) for gi in range(group)],
                               axis=0) * LOG2_E

        def with_sink(s_parts, spans):
            m = sink
            for s in s_parts:
                m = jnp.maximum(m, jnp.max(s, axis=-1, keepdims=True))
            acc = jnp.where(tail, jnp.exp2(sink - m), 0.0)
            for s, (k0, size) in zip(s_parts, spans):
                acc = acc + weigh(jnp.exp2(s - m), k0, size)
            return finish(acc)

        def context_queries():
            return with_sink([scores(0, ctx_len)], [(0, ctx_len)])

        def latent_queries():
            k0 = pl.multiple_of(jnp.clip((t - 1) * TILE, 0, total - LOCAL_KEYS), TILE)
            k_pos = k0 + lax.broadcasted_iota(jnp.int32, (TILE, LOCAL_KEYS), 1)
            q_pos = t * TILE + lax.broadcasted_iota(jnp.int32, (TILE, LOCAL_KEYS), 0)
            mask = jnp.logical_and(jnp.abs(k_pos - q_pos) <= WINDOW, k_pos >= ctx_len)
            s_loc = scores(k0, LOCAL_KEYS).reshape(group, TILE, LOCAL_KEYS)
            s_loc = jnp.where(mask[None], s_loc, NEG_INF).reshape(rows, LOCAL_KEYS)
            return with_sink([scores(0, ctx_len), s_loc], [(0, ctx_len), (k0, LOCAL_KEYS)])
    else:
        def online(spans):
            m = acc = None
            for k0, size in spans:
                s = scores(k0, size)
                m_chunk = jnp.max(s, axis=-1, keepdims=True)
                if m is None:
                    m, acc = m_chunk, weigh(jnp.exp2(s - m_chunk), k0, size)
                else:
                    m_new = jnp.maximum(m, m_chunk)
                    acc = jnp.exp2(m - m_new) * acc + weigh(jnp.exp2(s - m_new), k0, size)
                    m = m_new
            return finish(acc)

        def context_queries():
            return online([(0, ctx_len)])

        def latent_queries():
            return online(_key_chunks(0, total))

    o_ref[0] = lax.cond(t < n_ctx_tiles, context_queries, latent_queries).astype(BF16)


def _attention(st, q, k, v, sink):
    nh, nkv = q.shape[1], k.shape[1]
    group = nh // nkv
    windowed = sink is not None
    assert st.t >= LOCAL_KEYS
    q_spec = pl.BlockSpec((1, group, TILE, HEAD_DIM), lambda b, h, t: (b, h, t, 0))
    k_spec = pl.BlockSpec((1, 1, st.t, HEAD_DIM), lambda b, h, t: (b, h, 0, 0))
    v_spec = pl.BlockSpec((1, 1, st.t, LANES), lambda b, h, t: (b, h, 0, 0))
    specs, args = [q_spec, k_spec, v_spec], [q, k, v]
    if windowed:
        specs = [pl.BlockSpec(memory_space=pltpu.SMEM)] + specs
        args = [sink] + args
    return pl.pallas_call(
        functools.partial(_attn_body, ctx_len=st.ctx, n_ctx_tiles=st.n_ctx_tiles, total=st.t,
                          windowed=windowed),
        grid=(st.b, nkv, st.n_tiles),
        in_specs=specs,
        out_specs=pl.BlockSpec((1, TILE, group * HEAD_DIM), lambda b, h, t: (b, t, h)),
        out_shape=jax.ShapeDtypeStruct((st.b, st.t, nh * HEAD_DIM), BF16),
        compiler_params=_cparams("parallel", "parallel", "parallel"),
        name="window_attention" if windowed else "global_attention",
    )(*args)


def _attn_out_body(x_ref, mod_ref, g_ref, o_ref, wo_ref, out_ref):
    y = jnp.dot(o_ref[0], wo_ref[...], preferred_element_type=F32)
    m = mod_ref[0]
    out_ref[0] = x_ref[0] + m[5:6] * _rms(y, g_ref[...][3:4])


def _attn_out(st, x, mod, g, o, wo):
    return pl.pallas_call(
        _attn_out_body,
        grid=(st.b, st.n_tiles),
        in_specs=[st.tile_spec(), st.mod_spec(), _const_spec(g.shape),
                  st.tile_spec(o.shape[-1]), _const_spec(wo.shape)],
        out_specs=st.tile_spec(),
        out_shape=st.out_shape(),
        compiler_params=_cparams("parallel", "parallel"),
        name="attention_out",
    )(x, mod, g, o, wo)


def _pad_to(w, axis, mult):
    pad = (-w.shape[axis]) % mult
    if pad == 0:
        return w
    widths = [(0, 0)] * w.ndim
    widths[axis] = (0, pad)
    return jnp.pad(w, widths)


def _rope_tables(st):
    rows = (st.t - st.ctx) // GRID_W
    row = jnp.broadcast_to(jnp.arange(rows)[:, None], (rows, GRID_W)).reshape(-1)
    col = jnp.broadcast_to(jnp.arange(GRID_W)[None, :], (rows, GRID_W)).reshape(-1)
    inv_freq = ROPE_THETA ** (-jnp.arange(N_FREQ, dtype=F32) / N_FREQ)
    ang = jnp.stack([row, col], axis=-1).astype(F32)[:, :, None] * inv_freq
    cos, sin = jnp.cos(ang), jnp.sin(ang)
    cos_t = jnp.stack([cos, cos], axis=2).reshape(-1, HEAD_DIM)
    sin_t = jnp.stack([-sin, sin], axis=2).reshape(-1, HEAD_DIM)
    cos_t = jnp.concatenate([jnp.ones((st.ctx, HEAD_DIM), F32), cos_t], axis=0)
    sin_t = jnp.concatenate([jnp.zeros((st.ctx, HEAD_DIM), F32), sin_t], axis=0)
    return cos_t, sin_t


def _partner_perm():
    idx = np.arange(HEAD_DIM).reshape(2, 2, N_FREQ)
    return idx[:, ::-1, :].reshape(-1)


def _lane_groups(w):
    d = w.shape[0]
    w = w.reshape(d, -1, HEAD_DIM)
    return jnp.pad(w, ((0, 0), (0, 0), (0, LANES - HEAD_DIM))).reshape(d, -1).astype(BF16)


def _attn_params(wq, wk, wv, wo, gq, gk):
    perm = _partner_perm()

    def swapped(w):
        return w.reshape(w.shape[0], -1, HEAD_DIM)[:, :, perm].reshape(w.shape)

    return {"wq": _lane_groups(wq), "wq_sw": _lane_groups(swapped(wq)),
            "wk": _lane_groups(wk), "wk_sw": _lane_groups(swapped(wk)), "wv": _lane_groups(wv),
            "wo": wo.astype(BF16),
            "gq": jnp.stack([gq, gq[perm]]), "gk": jnp.stack([gk, gk[perm]])}


def _segment_ones():
    head = np.arange(SEG_TILE) // RWKV_HEAD
    return jnp.asarray(head[:, None] == head[None, :], dtype=BF16)


def kernel(x, c, ctx, c_ctx, mod_w, mod_b, norm_g, ffn_w1, ffn_w3, ffn_w2, rwkv_mu, rwkv_wr, rwkv_wk, rwkv_wv, rwkv_wo, rwkv_k_k, rwkv_k_a, rwkv_r_k, rwkv_w0, rwkv_w1, rwkv_w2, rwkv_a0, rwkv_a1, rwkv_a2, rwkv_g1, rwkv_g2, rwkv_ln_w, rwkv_ln_b, rwkv_v0, rwkv_v1, rwkv_v2, gattn_wq, gattn_wk, gattn_wv, gattn_wo, gattn_q_norm, gattn_k_norm, wattn_wq, wattn_wk, wattn_wv, wattn_wo, wattn_sink):
    batch, seq, d = x.shape
    depth = mod_w.shape[0]
    st = _Stream(batch, ctx.shape[1], seq, d)

    rows = -(-(batch + FFN_SEQS) // 8) * 8
    c_rows = jnp.zeros((rows, d), F32).at[:batch].set(c).at[batch:batch + FFN_SEQS].set(c_ctx)
    mods = _modulation(c_rows, mod_w, mod_b).reshape(depth, rows, N_MOD, d)

    w1, w3, w2 = ffn_w1.astype(BF16), ffn_w3.astype(BF16), ffn_w2.astype(BF16)
    rope = _rope_tables(st)
    seg = _segment_ones()
    bf = lambda w: w.astype(BF16)

    h = jnp.concatenate([ctx, x], axis=1)
    v_first = None
    for i in range(depth):
        kind, j = i % 3, i // 3
        g, mod = norm_g[i], mods[i]
        h = _ffn_half(st, h, mod, g, w1[i, 0], w3[i, 0], w2[i, 0], 0, 0)
        if kind == 0:
            p = {"mu": rwkv_mu[j], "wr": bf(rwkv_wr[j]), "wk": bf(rwkv_wk[j]), "wv": bf(rwkv_wv[j]),
                 "wo": bf(rwkv_wo[j]), "k_k": rwkv_k_k[j][None], "k_a": rwkv_k_a[j][None],
                 "r_k": rwkv_r_k[j].reshape(1, d),
                 "w0": rwkv_w0[j][:, None], "w1": bf(_pad_to(rwkv_w1[j], 2, LANES)), "w2": bf(_pad_to(rwkv_w2[j], 1, LANES)),
                 "a0": rwkv_a0[j][:, None], "a1": bf(_pad_to(rwkv_a1[j], 2, LANES)), "a2": bf(_pad_to(rwkv_a2[j], 1, LANES)),
                 "g1": bf(_pad_to(rwkv_g1[j], 2, LANES)), "g2": bf(_pad_to(rwkv_g2[j], 1, LANES)),
                 "ln_w": rwkv_ln_w[j][:, None], "ln_b": rwkv_ln_b[j][:, None]}
            vres = None
            if j > 0:
                vres = (v_first, rwkv_v0[j - 1][None], bf(_pad_to(rwkv_v1[j - 1], 1, LANES)),
                        bf(_pad_to(rwkv_v2[j - 1], 0, LANES)))
            r, v, kk, lw, kd, ab, gate = _rwkv_features(st, h, mod, g, p, seg, vres)
            if j == 0:
                v_first = v
            y_f = _rwkv_scan(st, r, lw, kd, v, kk, ab, 0)
            y_b = _rwkv_scan(st, r, lw, kd, v, kk, ab, 1)
            h = _rwkv_out(st, h, mod, g, y_f, y_b, r, kd, v, gate, p, seg)
        else:
            if kind == 1:
                a = _attn_params(gattn_wq[j], gattn_wk[j], gattn_wv[j], gattn_wo[j],
                                 gattn_q_norm[j], gattn_k_norm[j])
                sink = None
            else:
                ones = jnp.ones((HEAD_DIM,), F32)
                a = _attn_params(wattn_wq[j], wattn_wk[j], wattn_wv[j], wattn_wo[j], ones, ones)
                sink = wattn_sink[j]
            q, k, v = _qkv_project(st, h, mod, g, a, rope, qk_norm=kind == 1)
            o = _attention(st, q, k, v, sink)
            h = _attn_out(st, h, mod, g, o, a["wo"])
        h = _ffn_half(st, h, mod, g, w1[i, 1], w3[i, 1], w2[i, 1], 6, 4, latent_only=i == depth - 1)
    return h
```

```python
import functools

import numpy as np
import jax
import jax.numpy as jnp
from jax import lax
from jax.experimental import pallas as pl
from jax.experimental.pallas import tpu as pltpu

F32 = jnp.float32
BF16 = jnp.bfloat16
HIGHEST = lax.Precision.HIGHEST

GRID_W = 64
N_MOD = 9
HEAD_DIM = 64
N_KV_HEADS = 4
WINDOW = 128
ROPE_THETA = 10000.0
N_FREQ = HEAD_DIM // 4
RWKV_HEAD = 64
NORM_EPS = 1e-6
GN_EPS = 64e-5
L2_EPS = 1e-12
NEG_INF = -1e30
LOG2_E = 1.4426950408889634

LANES = 128
TILE = 256
FFN_SEQS = 2
CHUNK = 64
MAX_KEYS = 1152
SEG_TILE = 256
LOCAL_KEYS = 3 * TILE
VMEM_LIMIT = 56 * 1024 * 1024


def _cparams(*sem):
    return pltpu.CompilerParams(dimension_semantics=sem, vmem_limit_bytes=VMEM_LIMIT)


def _const_spec(shape):
    nd = len(shape)
    return pl.BlockSpec(shape, lambda *_: (0,) * nd, pipeline_mode=pl.Buffered(1))


def _dot(a, b):
    return jnp.dot(a.astype(BF16), b.astype(BF16), preferred_element_type=F32)


def _dot_nt(a, b):
    return lax.dot_general(a.astype(BF16), b.astype(BF16), (((1,), (1,)), ((), ())),
                           preferred_element_type=F32)


def _dot_tn(a, b):
    return lax.dot_general(a.astype(BF16), b.astype(BF16), (((0,), (0,)), ((), ())),
                           preferred_element_type=F32)


def _dot_f32(a, b):
    return jnp.dot(a, b, precision=HIGHEST, preferred_element_type=F32)


def _sigmoid(x):
    return 1.0 / (1.0 + jnp.exp(-x))


def _rms(x, g):
    return x * lax.rsqrt(jnp.mean(x * x, axis=-1, keepdims=True) + NORM_EPS) * g


def _norm_mod(x, g, scale, shift):
    return _rms(x, g) * (1.0 + scale) + shift


def _seg_sum(x, ones_ref, pieces):
    ones = ones_ref[...]
    parts, rest = [], x
    for i in range(pieces):
        part = rest.astype(BF16)
        parts.append(part)
        if i + 1 < pieces:
            rest = rest - part.astype(F32)
    cols = []
    for j in range(x.shape[1] // SEG_TILE):
        sl = slice(j * SEG_TILE, (j + 1) * SEG_TILE)
        cols.append(sum(jnp.dot(part[:, sl], ones, preferred_element_type=F32) for part in parts))
    return jnp.concatenate(cols, axis=1)


def _mod_body(c_ref, w_ref, b_ref, o_ref):
    c = c_ref[...]
    o_ref[0] = _dot_f32(c * _sigmoid(c), w_ref[0]) + b_ref[0]


def _modulation(c_rows, mod_w, mod_b):
    depth, d, nd = mod_w.shape
    rows = c_rows.shape[0]
    tn = nd // 6
    return pl.pallas_call(
        _mod_body,
        grid=(depth, nd // tn),
        in_specs=[_const_spec((rows, d)),
                  pl.BlockSpec((1, d, tn), lambda i, j: (i, 0, j)),
                  pl.BlockSpec((1, 1, tn), lambda i, j: (i, 0, j))],
        out_specs=pl.BlockSpec((1, rows, tn), lambda i, j: (i, 0, j)),
        out_shape=jax.ShapeDtypeStruct((depth, rows, nd), F32),
        compiler_params=_cparams("arbitrary", "arbitrary"),
        name="adaln_rows",
    )(c_rows, mod_w, mod_b.reshape(depth, 1, nd))


class _Stream:
    def __init__(self, batch, ctx_len, seq, d):
        assert ctx_len % TILE == 0 and seq % TILE == 0 and d % LANES == 0
        self.b, self.t, self.d = batch, ctx_len + seq, d
        self.ctx = ctx_len
        self.n_tiles = self.t // TILE
        self.n_ctx_tiles = ctx_len // TILE

    def tile_spec(self, width=None):
        return pl.BlockSpec((1, TILE, width or self.d), lambda b, t: (b, t, 0))

    def mod_spec(self):
        nct, ctx_row = self.n_ctx_tiles, self.b
        return pl.BlockSpec((1, N_MOD, self.d), lambda b, t: (jnp.where(t < nct, ctx_row, b), 0, 0))

    def out_shape(self, width=None, dtype=F32):
        return jax.ShapeDtypeStruct((self.b, self.t, width or self.d), dtype)


def _ffn_body(x_ref, mod_ref, g_ref, w1_ref, w3_ref, w2_ref, o_ref, *, mod_row, g_row):
    x = x_ref[...]
    m = mod_ref[...]
    g = g_ref[...]
    shift, scale, gate = m[:, mod_row:mod_row + 1], m[:, mod_row + 1:mod_row + 2], m[:, mod_row + 2:mod_row + 3]
    n = _norm_mod(x, g[g_row:g_row + 1], scale, shift).astype(BF16).reshape(-1, x.shape[-1])
    a = jnp.dot(n, w1_ref[...], preferred_element_type=F32)
    b = jnp.dot(n, w3_ref[...], preferred_element_type=F32)
    h = (a * _sigmoid(a) * b).astype(BF16)
    y = jnp.dot(h, w2_ref[...], preferred_element_type=F32).reshape(x.shape)
    o_ref[...] = x + 0.5 * gate * _rms(y, g[g_row + 1:g_row + 2])


def _ffn_half(st, x, mod, g, w1, w3, w2, mod_row, g_row, latent_only=False):
    d, f = w1.shape
    nb = FFN_SEQS if st.b % FFN_SEQS == 0 else 1
    nct, ctx_blk = st.n_ctx_tiles, st.b // nb
    skip = nct if latent_only else 0
    x_spec = pl.BlockSpec((nb, TILE, d), lambda b, t: (b, t + skip, 0))
    mod_spec = pl.BlockSpec((nb, N_MOD, d), lambda b, t: (jnp.where(t + skip < nct, ctx_blk, b), 0, 0))
    out_t = st.t - skip * TILE
    return pl.pallas_call(
        functools.partial(_ffn_body, mod_row=mod_row, g_row=g_row),
        grid=(st.b // nb, st.n_tiles - skip),
        in_specs=[x_spec, mod_spec, _const_spec(g.shape),
                  _const_spec((d, f)), _const_spec((d, f)), _const_spec((f, d))],
        out_specs=pl.BlockSpec((nb, TILE, d), lambda b, t: (b, t, 0)),
        out_shape=jax.ShapeDtypeStruct((st.b, out_t, d), F32),
        compiler_params=_cparams("parallel", "parallel"),
        name="swiglu_half",
    )(x, mod, g, w1, w3, w2)


def _softplus(y):
    return jnp.maximum(y, 0.0) + jnp.log(1.0 + jnp.exp(-jnp.abs(y)))


def _rwkv_feat_body(*refs, n_ctx_tiles, n_tiles, with_vres):
    (x_ref, xp_ref, xn_ref, mod_ref, g_ref, mu_ref, wr_ref, wk_ref, wv_ref, kk_ref, ka_ref,
     w0_ref, w1_ref, w2_ref, a0_ref, a1_ref, a2_ref, g1_ref, g2_ref, ones_ref) = refs[:20]
    rest = refs[20:]
    if with_vres:
        vf_ref, v0_ref, v1_ref, v2_ref = rest[:4]
        rest = rest[4:]
    r_out, v_out, kk_out, lw_out, kd_out, ab_out, g_out = rest

    t = pl.program_id(1)
    m = mod_ref[0]
    gn = g_ref[...][2:3]
    shift, scale = m[3:4], m[4:5]
    n = _norm_mod(x_ref[0], gn, scale, shift)
    first = jnp.logical_or(t == 0, t == n_ctx_tiles)
    last = jnp.logical_or(t == n_ctx_tiles - 1, t == n_tiles - 1)
    n_before = jnp.where(first, 0.0, _norm_mod(xp_ref[0][7:8], gn, scale, shift))
    n_after = jnp.where(last, 0.0, _norm_mod(xn_ref[0][0:1], gn, scale, shift))
    row = lax.broadcasted_iota(jnp.int32, (TILE, 1), 0)
    n_prev = jnp.where(row == 0, n_before, pltpu.roll(n, 1, 0))
    n_next = jnp.where(row == TILE - 1, n_after, pltpu.roll(n, TILE - 1, 0))
    xx = 0.5 * (n_prev + n_next) - n
    mu = mu_ref[...]
    xr, xw, xk, xv, xa, xg = (n + xx * mu[i:i + 1] for i in range(6))

    r = _dot(xr, wr_ref[...])
    k = _dot(xk, wk_ref[...])
    v = _dot(xv, wv_ref[...])
    if with_vres:
        lora = _dot(_dot(xv, v1_ref[...]), v2_ref[...])
        v = v + (vf_ref[0] - v) * _sigmoid(v0_ref[...] + lora)
    kk = k * kk_ref[...]
    kk = kk / jnp.maximum(jnp.sqrt(_seg_sum(kk * kk, ones_ref, 2)), L2_EPS)
    r_out[0] = r
    v_out[0] = v
    kk_out[0] = kk
    k_a = ka_ref[...]
    for dr in range(2):
        wl = w0_ref[dr] + _dot(jnp.tanh(_dot(xw, w1_ref[dr])), w2_ref[dr])
        w = -_softplus(-wl) - 0.5
        a = _sigmoid(a0_ref[dr] + _dot(_dot(xa, a1_ref[dr]), a2_ref[dr]))
        lw_out[dr, 0] = -jnp.exp(w)
        kd_out[dr, 0] = k * (1.0 + (a - 1.0) * k_a)
        ab_out[dr, 0] = kk * a
        g_out[dr, 0] = _dot(_sigmoid(_dot(xg, g1_ref[dr])), g2_ref[dr])


def _rwkv_features(st, x, mod, g, p, seg, vres):
    d = st.d
    n8 = st.t // 8
    per8 = TILE // 8
    prev_spec = pl.BlockSpec((1, 8, d), lambda b, t: (b, jnp.maximum(t * per8 - 1, 0), 0))
    next_spec = pl.BlockSpec((1, 8, d), lambda b, t: (b, jnp.minimum((t + 1) * per8, n8 - 1), 0))
    dir_spec = pl.BlockSpec((2, 1, TILE, d), lambda b, t: (0, b, t, 0))
    dir_shape = jax.ShapeDtypeStruct((2, st.b, st.t, d), F32)
    names = ("mu", "wr", "wk", "wv", "k_k", "k_a", "w0", "w1", "w2", "a0", "a1", "a2", "g1", "g2")
    consts = [p[nm] for nm in names] + [seg]
    args = [x, x, x, mod, g] + consts
    specs = [st.tile_spec(), prev_spec, next_spec, st.mod_spec(), _const_spec(g.shape)]
    specs += [_const_spec(c.shape) for c in consts]
    if vres is not None:
        v_first, v0, v1, v2 = vres
        args += [v_first, v0, v1, v2]
        specs += [st.tile_spec(), _const_spec(v0.shape), _const_spec(v1.shape), _const_spec(v2.shape)]
    return pl.pallas_call(
        functools.partial(_rwkv_feat_body, n_ctx_tiles=st.n_ctx_tiles, n_tiles=st.n_tiles,
                          with_vres=vres is not None),
        grid=(st.b, st.n_tiles),
        in_specs=specs,
        out_specs=[st.tile_spec()] * 3 + [dir_spec] * 4,
        out_shape=[st.out_shape()] * 3 + [dir_shape] * 4,
        compiler_params=_cparams("parallel", "parallel"),
        name="rwkv_features",
    )(*args)


def _unit_triangular_inverses(mats, ri, ci, size):
    eye = (ri == ci).astype(F32)
    ts = [jnp.where((ri >> 1) == (ci >> 1), a, 0.0) + eye for a in mats]
    level = 1
    while (2 << level) <= size:
        off = jnp.logical_and((ri >> (level + 1)) == (ci >> (level + 1)), (ri >> level) != (ci >> level))
        ws = [_dot(t, jnp.where(off, a, 0.0)) for t, a in zip(ts, mats)]
        ts = [t + _dot(w, t) for t, w in zip(ts, ws)]
        level += 1
    return ts


def _scan_body(r_ref, lw_ref, k_ref, v_ref, kk_ref, ab_ref, y_ref, s_ref, *, reverse):
    c = CHUNK
    n2 = 2 * c

    @pl.when(pl.program_id(1) == 0)
    def _():
        s_ref[...] = jnp.zeros_like(s_ref)

    r, lw, k, v, kk, ab = r_ref[0], lw_ref[0, 0], k_ref[0, 0], v_ref[0], kk_ref[0], ab_ref[0, 0]
    ti = lax.broadcasted_iota(jnp.int32, (c, c), 0)
    si = lax.broadcasted_iota(jnp.int32, (c, c), 1)
    upto = (si >= ti) if reverse else (si <= ti)
    cum = _dot_f32(upto.astype(F32), lw)
    tot = cum[0:1] if reverse else cum[c - 1:c]
    a_t = -kk * jnp.exp(cum - lw)
    r_t = r * jnp.exp(cum)
    inv = jnp.exp(-cum)
    b_t = ab * inv
    k_t = k * inv
    to_end = jnp.exp(tot - cum)
    b_e = ab * to_end
    k_e = k * to_end
    w_all = jnp.exp(tot)

    ri = lax.broadcasted_iota(jnp.int32, (n2, n2), 0)
    ci = lax.broadcasted_iota(jnp.int32, (n2, n2), 1)
    same = (ri >> 6) == (ci >> 6)
    tt, ss = ri & (c - 1), ci & (c - 1)
    before = (ss > tt) if reverse else (ss < tt)
    strict = jnp.logical_and(same, before)
    incl = jnp.logical_and(same, jnp.logical_or(before, ss == tt))
    incl2 = jnp.concatenate([incl, incl], axis=1)
    even = lax.broadcasted_iota(jnp.int32, (1, LANES), 1) < RWKV_HEAD

    def stack(x):
        return jnp.concatenate([jnp.where(even, x, 0.0), jnp.where(even, 0.0, x)], axis=0)

    pairs = range(r.shape[1] // LANES)
    lanes = [slice(p * LANES, (p + 1) * LANES) for p in pairs]
    prods = [_dot_nt(jnp.concatenate([stack(a_t[:, sl]), stack(r_t[:, sl])], axis=0),
                     jnp.concatenate([stack(b_t[:, sl]), stack(k_t[:, sl])], axis=0)) for sl in lanes]
    a_b = [jnp.where(strict, pr[0:n2, 0:n2], 0.0) for pr in prods]
    t_inv = _unit_triangular_inverses(a_b, ri, ci, c)
    v_st = [stack(v[:, sl]) for sl in lanes]
    u_own = [_dot(jnp.where(strict, pr[0:n2, n2:2 * n2], 0.0), vs) for pr, vs in zip(prods, v_st)]
    from_state = [_dot_nt(jnp.concatenate([a_t[:, sl], r_t[:, sl]], axis=0), s_ref[p])
                  for p, sl in zip(pairs, lanes)]
    u_st = [_dot(t, stack(fs[0:c]) + uo) for t, fs, uo in zip(t_inv, from_state, u_own)]
    y_st = [_dot(jnp.where(incl2, pr[n2:2 * n2], 0.0), jnp.concatenate([us, vs], axis=0))
            for pr, us, vs in zip(prods, u_st, v_st)]
    for p, sl in zip(pairs, lanes):
        y_ref[0, :, sl] = from_state[p][c:2 * c] + y_st[p][0:c] + y_st[p][c:n2]
        u = u_st[p][0:c] + u_st[p][c:n2]
        upd = _dot_tn(jnp.concatenate([u, v[:, sl]], axis=0),
                      jnp.concatenate([b_e[:, sl], k_e[:, sl]], axis=0))
        s_ref[p] = s_ref[p] * w_all[:, sl] + jnp.where(same, upd, 0.0)


def _rwkv_scan(st, r, lw, kd, v, kk, ab, direction):
    d = st.d
    nc, ncc = st.t // CHUNK, st.ctx // CHUNK
    if direction == 0:
        def cidx(s):
            return s
    else:
        def cidx(s):
            return jnp.where(s < ncc, ncc - 1 - s, nc - 1 - s + ncc)
    shared = pl.BlockSpec((1, CHUNK, d), lambda b, s: (b, cidx(s), 0))
    per_dir = pl.BlockSpec((1, 1, CHUNK, d), lambda b, s: (direction, b, cidx(s), 0))
    return pl.pallas_call(
        functools.partial(_scan_body, reverse=direction == 1),
        grid=(st.b, nc),
        in_specs=[shared, per_dir, per_dir, shared, shared, per_dir],
        out_specs=shared,
        out_shape=st.out_shape(),
        scratch_shapes=[pltpu.VMEM((d // LANES, LANES, LANES), F32)],
        compiler_params=_cparams("parallel", "arbitrary"),
        name="rwkv_scan_bwd" if direction else "rwkv_scan_fwd",
    )(r, lw, kd, v, kk, ab)


def _rwkv_out_body(x_ref, mod_ref, g_ref, yf_ref, yb_ref, r_ref, kd_ref, v_ref, gate_ref,
                   lnw_ref, lnb_ref, rk_ref, wo_ref, ones_ref, o_ref):
    r, v = r_ref[0], v_ref[0]
    inv_n = 1.0 / RWKV_HEAD
    o = None
    for dr, y_ref in enumerate((yf_ref, yb_ref)):
        y = y_ref[0]
        cen = y - _seg_sum(y, ones_ref, 3) * inv_n
        var = _seg_sum(cen * cen, ones_ref, 1) * inv_n
        yn = cen * lax.rsqrt(var + GN_EPS) * lnw_ref[dr] + lnb_ref[dr]
        bonus = _seg_sum(r * kd_ref[dr, 0] * rk_ref[...], ones_ref, 1) * v
        od = (yn + bonus) * gate_ref[dr, 0]
        o = od if o is None else o + od
    y_mix = _dot(o, wo_ref[...])
    m = mod_ref[0]
    o_ref[0] = x_ref[0] + m[5:6] * _rms(y_mix, g_ref[...][3:4])


def _rwkv_out(st, x, mod, g, y_f, y_b, r, kd, v, gate, p, seg):
    d = st.d
    dir_spec = pl.BlockSpec((2, 1, TILE, d), lambda b, t: (0, b, t, 0))
    consts = [p["ln_w"], p["ln_b"], p["r_k"], p["wo"]] + [seg]
    return pl.pallas_call(
        _rwkv_out_body,
        grid=(st.b, st.n_tiles),
        in_specs=[st.tile_spec(), st.mod_spec(), _const_spec(g.shape), st.tile_spec(), st.tile_spec(),
                  st.tile_spec(), dir_spec, st.tile_spec(), dir_spec] + [_const_spec(c.shape) for c in consts],
        out_specs=st.tile_spec(),
        out_shape=st.out_shape(),
        compiler_params=_cparams("parallel", "parallel"),
        name="rwkv_readout",
    )(x, mod, g, y_f, y_b, r, kd, v, gate, *consts)


def _qkv_body(x_ref, mod_ref, g_ref, wq_ref, wqs_ref, wk_ref, wks_ref, wv_ref, gq_ref, gk_ref,
              cos_ref, sin_ref, q_out, k_out, v_out, *, qk_norm, q_scale):
    m = mod_ref[0]
    n = _norm_mod(x_ref[0], g_ref[...][2:3], m[4:5], m[3:4]).astype(BF16)
    cos, sin = cos_ref[...], sin_ref[...]

    def rotated(w_ref, ws_ref, gain_ref, out_ref, scale):
        t_all = jnp.dot(n, w_ref[...], preferred_element_type=F32)
        ts_all = jnp.dot(n, ws_ref[...], preferred_element_type=F32)
        for h in range(out_ref.shape[1]):
            t = t_all[:, h * LANES:h * LANES + HEAD_DIM]
            ts = ts_all[:, h * LANES:h * LANES + HEAD_DIM]
            if qk_norm:
                inv = lax.rsqrt(jnp.mean(t * t, axis=-1, keepdims=True) + NORM_EPS)
                gain = gain_ref[...]
                t, ts = t * inv * gain[0:1], ts * inv * gain[1:2]
            out = t * cos + ts * sin
            out_ref[0, h] = (out * scale if scale != 1.0 else out).astype(BF16)

    rotated(wq_ref, wqs_ref, gq_ref, q_out, q_scale)
    rotated(wk_ref, wks_ref, gk_ref, k_out, 1.0)
    v_all = jnp.dot(n, wv_ref[...], preferred_element_type=F32)
    ones_tail = (lax.broadcasted_iota(jnp.int32, (1, LANES), 1) >= HEAD_DIM).astype(F32)
    for h in range(v_out.shape[1]):
        v_out[0, h] = (v_all[:, h * LANES:(h + 1) * LANES] + ones_tail).astype(BF16)


def _qkv_project(st, x, mod, g, a, rope, qk_norm):
    nh, nkv = a["wq"].shape[1] // LANES, a["wk"].shape[1] // LANES
    consts = [a["wq"], a["wq_sw"], a["wk"], a["wk_sw"], a["wv"], a["gq"], a["gk"]]
    rope_spec = pl.BlockSpec((TILE, HEAD_DIM), lambda b, t: (t, 0))

    def head_spec(n, width):
        return pl.BlockSpec((1, n, TILE, width), lambda b, t: (b, 0, t, 0))

    def head_shape(n, width):
        return jax.ShapeDtypeStruct((st.b, n, st.t, width), BF16)

    return pl.pallas_call(
        functools.partial(_qkv_body, qk_norm=qk_norm, q_scale=HEAD_DIM ** -0.5 * LOG2_E),
        grid=(st.b, st.n_tiles),
        in_specs=[st.tile_spec(), st.mod_spec(), _const_spec(g.shape)]
                 + [_const_spec(c.shape) for c in consts] + [rope_spec, rope_spec],
        out_specs=[head_spec(nh, HEAD_DIM), head_spec(nkv, HEAD_DIM), head_spec(nkv, LANES)],
        out_shape=[head_shape(nh, HEAD_DIM), head_shape(nkv, HEAD_DIM), head_shape(nkv, LANES)],
        compiler_params=_cparams("parallel", "parallel"),
        name="qkv_rope",
    )(x, mod, g, *consts, *rope)


def _key_chunks(k0, k1):
    n = -(-(k1 - k0) // MAX_KEYS)
    size = -(-(k1 - k0) // (n * LANES)) * LANES
    return [(a, min(size, k1 - a)) for a in range(k0, k1, size)]


def _attn_body(*refs, ctx_len, n_ctx_tiles, total, windowed):
    if windowed:
        sink_ref, q_ref, k_ref, v_ref, o_ref = refs
    else:
        q_ref, k_ref, v_ref, o_ref = refs
    group = q_ref.shape[1]
    t = pl.program_id(2)
    tail = lax.broadcasted_iota(jnp.int32, (1, LANES), 1) >= HEAD_DIM
    chains = range(group)
    qs = [q_ref[0, c] for c in chains]

    def scores(k0, size):
        return [_dot_nt(q, k_ref[0, 0, pl.ds(k0, size), :]) for q in qs]

    def weigh(ps, k0, size):
        return [jnp.dot(p.astype(BF16), v_ref[0, 0, pl.ds(k0, size), :], preferred_element_type=F32) for p in ps]

    def finish(accs):
        invs = [1.0 / jnp.where(tail, acc, 1.0) for acc in accs]
        cols = []
        for c in range(0, group, 2):
            first = accs[c] * pltpu.roll(invs[c], HEAD_DIM, 1)
            second = pltpu.roll(accs[c + 1], HEAD_DIM, 1) * invs[c + 1]
            cols.append(jnp.where(tail, second, first))
        return jnp.concatenate(cols, axis=1)

    def row_max(xs):
        return [jnp.max(x, axis=-1, keepdims=True) for x in xs]

    if windowed:
        kv = pl.program_id(1)
        sinks = [jnp.full((TILE, 1), sink_ref[kv * group + c], F32) * LOG2_E for c in chains]

        def with_sink(s_parts, spans):
            ms = sinks
            for part in s_parts:
                ms = [jnp.maximum(m, pm) for m, pm in zip(ms, row_max(part))]
            accs = [jnp.where(tail, jnp.exp2(sk - m), 0.0) for sk, m in zip(sinks, ms)]
            for part, (k0, size) in zip(s_parts, spans):
                pvs = weigh([jnp.exp2(s - m) for s, m in zip(part, ms)], k0, size)
                accs = [acc + pv for acc, pv in zip(accs, pvs)]
            return finish(accs)

        def context_queries():
            return with_sink([scores(0, ctx_len)], [(0, ctx_len)])

        def latent_queries():
            k0 = pl.multiple_of(jnp.clip((t - 1) * TILE, 0, total - LOCAL_KEYS), TILE)
            k_pos = k0 + lax.broadcasted_iota(jnp.int32, (TILE, LOCAL_KEYS), 1)
            q_pos = t * TILE + lax.broadcasted_iota(jnp.int32, (TILE, LOCAL_KEYS), 0)
            mask = jnp.logical_and(jnp.abs(k_pos - q_pos) <= WINDOW, k_pos >= ctx_len)
            s_loc = [jnp.where(mask, s, NEG_INF) for s in scores(k0, LOCAL_KEYS)]
            return with_sink([scores(0, ctx_len), s_loc], [(0, ctx_len), (k0, LOCAL_KEYS)])
    else:
        def online(spans):
            ms = accs = None
            for k0, size in spans:
                ss = scores(k0, size)
                m_chunk = row_max(ss)
                if ms is None:
                    ms, accs = m_chunk, weigh([jnp.exp2(s - m) for s, m in zip(ss, m_chunk)], k0, size)
                else:
                    m_new = [jnp.maximum(m, mc) for m, mc in zip(ms, m_chunk)]
                    pvs = weigh([jnp.exp2(s - m) for s, m in zip(ss, m_new)], k0, size)
                    accs = [jnp.exp2(m - mn) * acc + pv for m, mn, acc, pv in zip(ms, m_new, accs, pvs)]
                    ms = m_new
            return finish(accs)

        def context_queries():
            return online([(0, ctx_len)])

        def latent_queries():
            return online(_key_chunks(0, total))

    o_ref[0] = lax.cond(t < n_ctx_tiles, context_queries, latent_queries).astype(BF16)


def _attention(st, q, k, v, sink):
    nh, nkv = q.shape[1], k.shape[1]
    group = nh // nkv
    windowed = sink is not None
    assert st.t >= LOCAL_KEYS
    q_spec = pl.BlockSpec((1, group, TILE, HEAD_DIM), lambda b, h, t: (b, h, t, 0))
    k_spec = pl.BlockSpec((1, 1, st.t, HEAD_DIM), lambda b, h, t: (b, h, 0, 0))
    v_spec = pl.BlockSpec((1, 1, st.t, LANES), lambda b, h, t: (b, h, 0, 0))
    specs, args = [q_spec, k_spec, v_spec], [q, k, v]
    if windowed:
        specs = [pl.BlockSpec(memory_space=pltpu.SMEM)] + specs
        args = [sink] + args
    return pl.pallas_call(
        functools.partial(_attn_body, ctx_len=st.ctx, n_ctx_tiles=st.n_ctx_tiles, total=st.t,
                          windowed=windowed),
        grid=(st.b, nkv, st.n_tiles),
        in_specs=specs,
        out_specs=pl.BlockSpec((1, TILE, group * HEAD_DIM), lambda b, h, t: (b, t, h)),
        out_shape=jax.ShapeDtypeStruct((st.b, st.t, nh * HEAD_DIM), BF16),
        compiler_params=_cparams("parallel", "parallel", "parallel"),
        name="window_attention" if windowed else "global_attention",
    )(*args)


def _attn_out_body(x_ref, mod_ref, g_ref, o_ref, wo_ref, out_ref):
    y = jnp.dot(o_ref[0], wo_ref[...], preferred_element_type=F32)
    m = mod_ref[0]
    out_ref[0] = x_ref[0] + m[5:6] * _rms(y, g_ref[...][3:4])


def _attn_out(st, x, mod, g, o, wo):
    return pl.pallas_call(
        _attn_out_body,
        grid=(st.b, st.n_tiles),
        in_specs=[st.tile_spec(), st.mod_spec(), _const_spec(g.shape),
                  st.tile_spec(o.shape[-1]), _const_spec(wo.shape)],
        out_specs=st.tile_spec(),
        out_shape=st.out_shape(),
        compiler_params=_cparams("parallel", "parallel"),
        name="attention_out",
    )(x, mod, g, o, wo)


def _pad_to(w, axis, mult):
    pad = (-w.shape[axis]) % mult
    if pad == 0:
        return w
    widths = [(0, 0)] * w.ndim
    widths[axis] = (0, pad)
    return jnp.pad(w, widths)


def _rope_tables(st):
    rows = (st.t - st.ctx) // GRID_W
    row = jnp.broadcast_to(jnp.arange(rows)[:, None], (rows, GRID_W)).reshape(-1)
    col = jnp.broadcast_to(jnp.arange(GRID_W)[None, :], (rows, GRID_W)).reshape(-1)
    inv_freq = ROPE_THETA ** (-jnp.arange(N_FREQ, dtype=F32) / N_FREQ)
    ang = jnp.stack([row, col], axis=-1).astype(F32)[:, :, None] * inv_freq
    cos, sin = jnp.cos(ang), jnp.sin(ang)
    cos_t = jnp.stack([cos, cos], axis=2).reshape(-1, HEAD_DIM)
    sin_t = jnp.stack([-sin, sin], axis=2).reshape(-1, HEAD_DIM)
    cos_t = jnp.concatenate([jnp.ones((st.ctx, HEAD_DIM), F32), cos_t], axis=0)
    sin_t = jnp.concatenate([jnp.zeros((st.ctx, HEAD_DIM), F32), sin_t], axis=0)
    return cos_t, sin_t


def _partner_perm():
    idx = np.arange(HEAD_DIM).reshape(2, 2, N_FREQ)
    return idx[:, ::-1, :].reshape(-1)


def _lane_groups(w):
    d = w.shape[0]
    w = w.reshape(d, -1, HEAD_DIM)
    return jnp.pad(w, ((0, 0), (0, 0), (0, LANES - HEAD_DIM))).reshape(d, -1).astype(BF16)


def _attn_params(wq, wk, wv, wo, gq, gk):
    perm = _partner_perm()

    def swapped(w):
        return w.reshape(w.shape[0], -1, HEAD_DIM)[:, :, perm].reshape(w.shape)

    return {"wq": _lane_groups(wq), "wq_sw": _lane_groups(swapped(wq)),
            "wk": _lane_groups(wk), "wk_sw": _lane_groups(swapped(wk)), "wv": _lane_groups(wv),
            "wo": wo.astype(BF16),
            "gq": jnp.stack([gq, gq[perm]]), "gk": jnp.stack([gk, gk[perm]])}


def _segment_ones():
    head = np.arange(SEG_TILE) // RWKV_HEAD
    return jnp.asarray(head[:, None] == head[None, :], dtype=BF16)


def kernel(x, c, ctx, c_ctx, mod_w, mod_b, norm_g, ffn_w1, ffn_w3, ffn_w2, rwkv_mu, rwkv_wr, rwkv_wk, rwkv_wv, rwkv_wo, rwkv_k_k, rwkv_k_a, rwkv_r_k, rwkv_w0, rwkv_w1, rwkv_w2, rwkv_a0, rwkv_a1, rwkv_a2, rwkv_g1, rwkv_g2, rwkv_ln_w, rwkv_ln_b, rwkv_v0, rwkv_v1, rwkv_v2, gattn_wq, gattn_wk, gattn_wv, gattn_wo, gattn_q_norm, gattn_k_norm, wattn_wq, wattn_wk, wattn_wv, wattn_wo, wattn_sink):
    batch, seq, d = x.shape
    depth = mod_w.shape[0]
    st = _Stream(batch, ctx.shape[1], seq, d)

    rows = -(-(batch + FFN_SEQS) // 8) * 8
    c_rows = jnp.zeros((rows, d), F32).at[:batch].set(c).at[batch:batch + FFN_SEQS].set(c_ctx)
    mods = _modulation(c_rows, mod_w, mod_b).reshape(depth, rows, N_MOD, d)

    w1, w3, w2 = ffn_w1.astype(BF16), ffn_w3.astype(BF16), ffn_w2.astype(BF16)
    rope = _rope_tables(st)
    seg = _segment_ones()
    bf = lambda w: w.astype(BF16)

    h = jnp.concatenate([ctx, x], axis=1)
    v_first = None
    for i in range(depth):
        kind, j = i % 3, i // 3
        g, mod = norm_g[i], mods[i]
        h = _ffn_half(st, h, mod, g, w1[i, 0], w3[i, 0], w2[i, 0], 0, 0)
        if kind == 0:
            p = {"mu": rwkv_mu[j], "wr": bf(rwkv_wr[j]), "wk": bf(rwkv_wk[j]), "wv": bf(rwkv_wv[j]),
                 "wo": bf(rwkv_wo[j]), "k_k": rwkv_k_k[j][None], "k_a": rwkv_k_a[j][None],
                 "r_k": rwkv_r_k[j].reshape(1, d),
                 "w0": rwkv_w0[j][:, None], "w1": bf(_pad_to(rwkv_w1[j], 2, LANES)), "w2": bf(_pad_to(rwkv_w2[j], 1, LANES)),
                 "a0": rwkv_a0[j][:, None], "a1": bf(_pad_to(rwkv_a1[j], 2, LANES)), "a2": bf(_pad_to(rwkv_a2[j], 1, LANES)),
                 "g1": bf(_pad_to(rwkv_g1[j], 2, LANES)), "g2": bf(_pad_to(rwkv_g2[j], 1, LANES)),
                 "ln_w": rwkv_ln_w[j][:, None], "ln_b": rwkv_ln_b[j][:, None]}
            vres = None
            if j > 0:
                vres = (v_first, rwkv_v0[j - 1][None], bf(_pad_to(rwkv_v1[j - 1], 1, LANES)),
                        bf(_pad_to(rwkv_v2[j - 1], 0, LANES)))
            r, v, kk, lw, kd, ab, gate = _rwkv_features(st, h, mod, g, p, seg, vres)
            if j == 0:
                v_first = v
            y_f = _rwkv_scan(st, r, lw, kd, v, kk, ab, 0)
            y_b = _rwkv_scan(st, r, lw, kd, v, kk, ab, 1)
            h = _rwkv_out(st, h, mod, g, y_f, y_b, r, kd, v, gate, p, seg)
        else:
            if kind == 1:
                a = _attn_params(gattn_wq[j], gattn_wk[j], gattn_wv[j], gattn_wo[j],
                                 gattn_q_norm[j], gattn_k_norm[j])
                sink = None
            else:
                ones = jnp.ones((HEAD_DIM,), F32)
                a = _attn_params(wattn_wq[j], wattn_wk[j], wattn_wv[j], wattn_wo[j], ones, ones)
                sink = wattn_sink[j]
            q, k, v = _qkv_project(st, h, mod, g, a, rope, qk_norm=kind == 1)
            o = _attention(st, q, k, v, sink)
            h = _attn_out(st, h, mod, g, o, a["wo"])
        h = _ffn_half(st, h, mod, g, w1[i, 1], w3[i, 1], w2[i, 1], 6, 4, latent_only=i == depth - 1)
    return h
```

```python
import functools

import numpy as np
import jax
import jax.numpy as jnp
from jax import lax
from jax.experimental import pallas as pl
from jax.experimental.pallas import tpu as pltpu

F32 = jnp.float32
BF16 = jnp.bfloat16
HIGHEST = lax.Precision.HIGHEST

GRID_W = 64
N_MOD = 9
HEAD_DIM = 64
N_KV_HEADS = 4
WINDOW = 128
ROPE_THETA = 10000.0
N_FREQ = HEAD_DIM // 4
RWKV_HEAD = 64
NORM_EPS = 1e-6
GN_EPS = 64e-5
L2_EPS = 1e-12
NEG_INF = -1e30
LOG2_E = 1.4426950408889634
EXP_NEG_HALF = 0.6065306597126334

LANES = 128
SUBLANES = 8
TILE = 256
FFN_SEQS = 2
SCAN_SEQS = 4
CHUNK = 64
MAX_KEYS = 1152
SEG_TILE = 256
LOCAL_KEYS = 3 * TILE
VMEM_LIMIT = 56 * 1024 * 1024


def _cparams(*sem):
    return pltpu.CompilerParams(dimension_semantics=sem, vmem_limit_bytes=VMEM_LIMIT)


def _const_spec(shape):
    nd = len(shape)
    return pl.BlockSpec(shape, lambda *_: (0,) * nd, pipeline_mode=pl.Buffered(1))


def _dot(a, b):
    return jnp.dot(a.astype(BF16), b.astype(BF16), preferred_element_type=F32)


def _dot_nt(a, b):
    return lax.dot_general(a.astype(BF16), b.astype(BF16), (((1,), (1,)), ((), ())),
                           preferred_element_type=F32)


def _dot_tn(a, b):
    return lax.dot_general(a.astype(BF16), b.astype(BF16), (((0,), (0,)), ((), ())),
                           preferred_element_type=F32)


def _dot_f32(a, b):
    return jnp.dot(a, b, precision=HIGHEST, preferred_element_type=F32)


def _sigmoid(x):
    return 1.0 / (1.0 + jnp.exp(-x))


def _rms(x, g):
    return x * lax.rsqrt(jnp.mean(x * x, axis=-1, keepdims=True) + NORM_EPS) * g


def _norm_mod(x, g, scale, shift):
    return _rms(x, g) * (1.0 + scale) + shift


def _seg_sum(x, ones_ref, pieces):
    ones = ones_ref[...]
    parts, rest = [], x
    for i in range(pieces):
        part = rest.astype(BF16)
        parts.append(part)
        if i + 1 < pieces:
            rest = rest - part.astype(F32)
    cols = []
    for j in range(x.shape[1] // SEG_TILE):
        sl = slice(j * SEG_TILE, (j + 1) * SEG_TILE)
        cols.append(sum(jnp.dot(part[:, sl], ones, preferred_element_type=F32) for part in parts))
    return jnp.concatenate(cols, axis=1)


def _mod_body(c_ref, w_ref, b_ref, o_ref):
    c = c_ref[...]
    o_ref[0] = _dot_f32(c * _sigmoid(c), w_ref[0]) + b_ref[0]


def _modulation(c_rows, mod_w, mod_b):
    depth, d, nd = mod_w.shape
    rows = c_rows.shape[0]
    tn = nd // 6
    return pl.pallas_call(
        _mod_body,
        grid=(depth, nd // tn),
        in_specs=[_const_spec((rows, d)),
                  pl.BlockSpec((1, d, tn), lambda i, j: (i, 0, j)),
                  pl.BlockSpec((1, 1, tn), lambda i, j: (i, 0, j))],
        out_specs=pl.BlockSpec((1, rows, tn), lambda i, j: (i, 0, j)),
        out_shape=jax.ShapeDtypeStruct((depth, rows, nd), F32),
        compiler_params=_cparams("arbitrary", "arbitrary"),
        name="adaln_rows",
    )(c_rows, mod_w, mod_b.reshape(depth, 1, nd))


class _Stream:
    def __init__(self, batch, ctx_len, seq, d):
        assert ctx_len % TILE == 0 and seq % TILE == 0 and d % LANES == 0
        self.b, self.t, self.d = batch, ctx_len + seq, d
        self.ctx = ctx_len
        self.n_tiles = self.t // TILE
        self.n_ctx_tiles = ctx_len // TILE

    def tile_spec(self, width=None):
        return pl.BlockSpec((1, TILE, width or self.d), lambda b, t: (b, t, 0))

    def mod_spec(self):
        nct, ctx_row = self.n_ctx_tiles, self.b
        return pl.BlockSpec((1, N_MOD, self.d), lambda b, t: (jnp.where(t < nct, ctx_row, b), 0, 0))

    def out_shape(self, width=None, dtype=F32):
        return jax.ShapeDtypeStruct((self.b, self.t, width or self.d), dtype)


def _ffn_body(x_ref, mod_ref, g_ref, w1_ref, w3_ref, w2_ref, o_ref, *, mod_row, g_row):
    x = x_ref[...]
    m = mod_ref[...]
    g = g_ref[...]
    shift, scale, gate = m[:, mod_row:mod_row + 1], m[:, mod_row + 1:mod_row + 2], m[:, mod_row + 2:mod_row + 3]
    n = _norm_mod(x, g[g_row:g_row + 1], scale, shift).astype(BF16).reshape(-1, x.shape[-1])
    a = jnp.dot(n, w1_ref[...], preferred_element_type=F32)
    b = jnp.dot(n, w3_ref[...], preferred_element_type=F32)
    h = (a * _sigmoid(a) * b).astype(BF16)
    y = jnp.dot(h, w2_ref[...], preferred_element_type=F32).reshape(x.shape)
    o_ref[...] = x + 0.5 * gate * _rms(y, g[g_row + 1:g_row + 2])


def _ffn_half(st, x, mod, g, w1, w3, w2, mod_row, g_row, latent_only=False):
    d, f = w1.shape
    nb = FFN_SEQS if st.b % FFN_SEQS == 0 else 1
    nct, ctx_blk = st.n_ctx_tiles, st.b // nb
    skip = nct if latent_only else 0
    x_spec = pl.BlockSpec((nb, TILE, d), lambda b, t: (b, t + skip, 0))
    mod_spec = pl.BlockSpec((nb, N_MOD, d), lambda b, t: (jnp.where(t + skip < nct, ctx_blk, b), 0, 0))
    out_t = st.t - skip * TILE
    return pl.pallas_call(
        functools.partial(_ffn_body, mod_row=mod_row, g_row=g_row),
        grid=(st.b // nb, st.n_tiles - skip),
        in_specs=[x_spec, mod_spec, _const_spec(g.shape),
                  _const_spec((d, f)), _const_spec((d, f)), _const_spec((f, d))],
        out_specs=pl.BlockSpec((nb, TILE, d), lambda b, t: (b, t, 0)),
        out_shape=jax.ShapeDtypeStruct((st.b, out_t, d), F32),
        compiler_params=_cparams("parallel", "parallel"),
        name="swiglu_half",
    )(x, mod, g, w1, w3, w2)


def _rwkv_feat_body(*refs, n_ctx_tiles, n_tiles, with_vres):
    (x_ref, xp_ref, xn_ref, mod_ref, g_ref, mu_ref, wr_ref, wk_ref, wv_ref, kk_ref, ka_ref,
     w0_ref, w1_ref, w2_ref, a0_ref, a1_ref, a2_ref, g1_ref, g2_ref, ones_ref) = refs[:20]
    rest = refs[20:]
    if with_vres:
        vf_ref, v0_ref, v1_ref, v2_ref = rest[:4]
        rest = rest[4:]
    r_out, v_out, kk_out, lw_out, kd_out, ab_out, g_out = rest

    t = pl.program_id(1)
    m = mod_ref[0]
    gn = g_ref[...][2:3]
    shift, scale = m[3:4], m[4:5]
    n = _norm_mod(x_ref[0], gn, scale, shift)
    first = jnp.logical_or(t == 0, t == n_ctx_tiles)
    last = jnp.logical_or(t == n_ctx_tiles - 1, t == n_tiles - 1)
    n_before = jnp.where(first, 0.0, _norm_mod(xp_ref[0][7:8], gn, scale, shift))
    n_after = jnp.where(last, 0.0, _norm_mod(xn_ref[0][0:1], gn, scale, shift))
    row = lax.broadcasted_iota(jnp.int32, (TILE, 1), 0)
    n_prev = jnp.where(row == 0, n_before, pltpu.roll(n, 1, 0))
    n_next = jnp.where(row == TILE - 1, n_after, pltpu.roll(n, TILE - 1, 0))
    xx = 0.5 * (n_prev + n_next) - n
    mu = mu_ref[...]
    xr, xw, xk, xv, xa, xg = (n + xx * mu[i:i + 1] for i in range(6))

    r = _dot(xr, wr_ref[...])
    k = _dot(xk, wk_ref[...])
    v = _dot(xv, wv_ref[...])
    if with_vres:
        lora = _dot(_dot(xv, v1_ref[...]), v2_ref[...])
        v = v + (vf_ref[0] - v) * _sigmoid(v0_ref[...] + lora)
    kk = k * kk_ref[...]
    kk = kk / jnp.maximum(jnp.sqrt(_seg_sum(kk * kk, ones_ref, 2)), L2_EPS)
    r_out[0] = r
    v_out[0] = v
    kk_out[0] = kk
    k_a = ka_ref[...]
    for dr in range(2):
        wl = w0_ref[dr] + _dot(jnp.tanh(_dot(xw, w1_ref[dr])), w2_ref[dr])
        a = _sigmoid(a0_ref[dr] + _dot(_dot(xa, a1_ref[dr]), a2_ref[dr]))
        lw_out[dr, 0] = -EXP_NEG_HALF * _sigmoid(wl)
        kd_out[dr, 0] = k * (1.0 + (a - 1.0) * k_a)
        ab_out[dr, 0] = kk * a
        g_out[dr, 0] = _dot(_sigmoid(_dot(xg, g1_ref[dr])), g2_ref[dr])


def _rwkv_features(st, x, mod, g, p, seg, vres):
    d = st.d
    n8 = st.t // 8
    per8 = TILE // 8
    prev_spec = pl.BlockSpec((1, 8, d), lambda b, t: (b, jnp.maximum(t * per8 - 1, 0), 0))
    next_spec = pl.BlockSpec((1, 8, d), lambda b, t: (b, jnp.minimum((t + 1) * per8, n8 - 1), 0))
    dir_spec = pl.BlockSpec((2, 1, TILE, d), lambda b, t: (0, b, t, 0))
    dir_shape = jax.ShapeDtypeStruct((2, st.b, st.t, d), F32)
    names = ("mu", "wr", "wk", "wv", "k_k", "k_a", "w0", "w1", "w2", "a0", "a1", "a2", "g1", "g2")
    consts = [p[nm] for nm in names] + [seg]
    args = [x, x, x, mod, g] + consts
    specs = [st.tile_spec(), prev_spec, next_spec, st.mod_spec(), _const_spec(g.shape)]
    specs += [_const_spec(c.shape) for c in consts]
    if vres is not None:
        v_first, v0, v1, v2 = vres
        args += [v_first, v0, v1, v2]
        specs += [st.tile_spec(), _const_spec(v0.shape), _const_spec(v1.shape), _const_spec(v2.shape)]
    return pl.pallas_call(
        functools.partial(_rwkv_feat_body, n_ctx_tiles=st.n_ctx_tiles, n_tiles=st.n_tiles,
                          with_vres=vres is not None),
        grid=(st.b, st.n_tiles),
        in_specs=specs,
        out_specs=[st.tile_spec()] * 3 + [dir_spec] * 4,
        out_shape=[st.out_shape()] * 3 + [dir_shape] * 4,
        compiler_params=_cparams("parallel", "parallel"),
        name="rwkv_features",
    )(*args)


def _unit_triangular_inverses(mats, ri, ci, size, upper):
    n = mats[0].shape[0]
    eye = (ri == ci).astype(F32)
    ts = [jnp.where((ri >> 1) == (ci >> 1), a, 0.0) + eye for a in mats]
    level = 1
    while (2 << level) <= size:
        m = 1 << level
        off = jnp.logical_and((ri >> (level + 1)) == (ci >> (level + 1)), (ri >> level) != (ci >> level))
        a_off = [jnp.where(off, a, 0.0) for a in mats]
        if m % SUBLANES == 0:
            moving = range(0 if upper else 1, n // m, 2)

            def rows(x):
                return jnp.concatenate([x[b * m:(b + 1) * m] for b in moving], axis=0)

            zs = [_dot(_dot(rows(t), a), t) for t, a in zip(ts, a_off)]
            ts = [jnp.concatenate([t[b * m:(b + 1) * m] + z[(b // 2) * m:(b // 2 + 1) * m] if b in moving
                                   else t[b * m:(b + 1) * m] for b in range(n // m)], axis=0)
                  for t, z in zip(ts, zs)]
        else:
            ts = [t + _dot(_dot(t, a), t) for t, a in zip(ts, a_off)]
        level += 1
    return ts


def _scan_body(r_ref, lw_ref, k_ref, v_ref, kk_ref, ab_ref, y_ref, s_ref, *, reverse):
    c = CHUNK
    n2 = 2 * c

    @pl.when(pl.program_id(1) == 0)
    def _():
        s_ref[...] = jnp.zeros_like(s_ref)

    ti = lax.broadcasted_iota(jnp.int32, (c, c), 0)
    si = lax.broadcasted_iota(jnp.int32, (c, c), 1)
    upto = ((si >= ti) if reverse else (si <= ti)).astype(F32)
    ri = lax.broadcasted_iota(jnp.int32, (n2, n2), 0)
    ci = lax.broadcasted_iota(jnp.int32, (n2, n2), 1)
    same = (ri >> 6) == (ci >> 6)
    tt, ss = ri & (c - 1), ci & (c - 1)
    before = (ss > tt) if reverse else (ss < tt)
    strict = jnp.logical_and(same, before)
    incl = jnp.logical_and(same, jnp.logical_or(before, ss == tt))
    incl2 = jnp.concatenate([incl, incl], axis=1)
    even = lax.broadcasted_iota(jnp.int32, (1, LANES), 1) < RWKV_HEAD

    def stack(x):
        return jnp.concatenate([jnp.where(even, x, 0.0), jnp.where(even, 0.0, x)], axis=0)

    n_pairs = r_ref.shape[-1] // LANES
    chains = [(q, p) for q in range(r_ref.shape[0]) for p in range(n_pairs)]
    cols = {}
    for q in range(r_ref.shape[0]):
        r, lw, k, v, kk, ab = r_ref[q], lw_ref[0, q], k_ref[0, q], v_ref[q], kk_ref[q], ab_ref[0, q]
        cum = _dot_f32(upto, lw)
        tot = cum[0:1] if reverse else cum[c - 1:c]
        inv = jnp.exp(-cum)
        to_end = jnp.exp(tot - cum)
        full = {"a": -kk * jnp.exp(cum - lw), "r": r * jnp.exp(cum), "b": ab * inv, "k": k * inv,
                "be": ab * to_end, "ke": k * to_end, "w": jnp.exp(tot), "v": v}
        for p in range(n_pairs):
            cols[q, p] = {nm: x[:, p * LANES:(p + 1) * LANES] for nm, x in full.items()}

    prods = [_dot_nt(jnp.concatenate([stack(cols[ch]["a"]), stack(cols[ch]["r"])], axis=0),
                     jnp.concatenate([stack(cols[ch]["b"]), stack(cols[ch]["k"])], axis=0)) for ch in chains]
    a_b = [jnp.where(strict, pr[0:n2, 0:n2], 0.0) for pr in prods]
    t_inv = _unit_triangular_inverses(a_b, ri, ci, c, reverse)
    v_st = [stack(cols[ch]["v"]) for ch in chains]
    u_own = [_dot(jnp.where(strict, pr[0:n2, n2:2 * n2], 0.0), vs) for pr, vs in zip(prods, v_st)]
    from_state = [_dot_nt(jnp.concatenate([cols[ch]["a"], cols[ch]["r"]], axis=0), s_ref[ch])
                  for ch in chains]
    u_st = [_dot(t, stack(fs[0:c]) + uo) for t, fs, uo in zip(t_inv, from_state, u_own)]
    y_st = [_dot(jnp.where(incl2, pr[n2:2 * n2], 0.0), jnp.concatenate([us, vs], axis=0))
            for pr, us, vs in zip(prods, u_st, v_st)]
    for i, (q, p) in enumerate(chains):
        y_ref[q, :, p * LANES:(p + 1) * LANES] = from_state[i][c:2 * c] + y_st[i][0:c] + y_st[i][c:n2]
        u = u_st[i][0:c] + u_st[i][c:n2]
        upd = _dot_tn(jnp.concatenate([u, cols[q, p]["v"]], axis=0),
                      jnp.concatenate([cols[q, p]["be"], cols[q, p]["ke"]], axis=0))
        s_ref[q, p] = s_ref[q, p] * cols[q, p]["w"] + jnp.where(same, upd, 0.0)


def _rwkv_scan(st, r, lw, kd, v, kk, ab, direction):
    d = st.d
    nq = SCAN_SEQS if st.b % SCAN_SEQS == 0 else 1
    nc, ncc = st.t // CHUNK, st.ctx // CHUNK
    if direction == 0:
        def cidx(s):
            return s
    else:
        def cidx(s):
            return jnp.where(s < ncc, ncc - 1 - s, nc - 1 - s + ncc)
    shared = pl.BlockSpec((nq, CHUNK, d), lambda b, s: (b, cidx(s), 0))
    per_dir = pl.BlockSpec((1, nq, CHUNK, d), lambda b, s: (direction, b, cidx(s), 0))
    return pl.pallas_call(
        functools.partial(_scan_body, reverse=direction == 1),
        grid=(st.b // nq, nc),
        in_specs=[shared, per_dir, per_dir, shared, shared, per_dir],
        out_specs=shared,
        out_shape=st.out_shape(),
        scratch_shapes=[pltpu.VMEM((nq, d // LANES, LANES, LANES), F32)],
        compiler_params=_cparams("parallel", "arbitrary"),
        name="rwkv_scan_bwd" if direction else "rwkv_scan_fwd",
    )(r, lw, kd, v, kk, ab)


def _rwkv_out_body(x_ref, mod_ref, g_ref, yf_ref, yb_ref, r_ref, kd_ref, v_ref, gate_ref,
                   lnw_ref, lnb_ref, rk_ref, wo_ref, ones_ref, o_ref):
    r, v = r_ref[0], v_ref[0]
    inv_n = 1.0 / RWKV_HEAD
    o = None
    for dr, y_ref in enumerate((yf_ref, yb_ref)):
        y = y_ref[0]
        cen = y - _seg_sum(y, ones_ref, 3) * inv_n
        var = _seg_sum(cen * cen, ones_ref, 1) * inv_n
        yn = cen * lax.rsqrt(var + GN_EPS) * lnw_ref[dr] + lnb_ref[dr]
        bonus = _seg_sum(r * kd_ref[dr, 0] * rk_ref[...], ones_ref, 1) * v
        od = (yn + bonus) * gate_ref[dr, 0]
        o = od if o is None else o + od
    y_mix = _dot(o, wo_ref[...])
    m = mod_ref[0]
    o_ref[0] = x_ref[0] + m[5:6] * _rms(y_mix, g_ref[...][3:4])


def _rwkv_out(st, x, mod, g, y_f, y_b, r, kd, v, gate, p, seg):
    d = st.d
    dir_spec = pl.BlockSpec((2, 1, TILE, d), lambda b, t: (0, b, t, 0))
    consts = [p["ln_w"], p["ln_b"], p["r_k"], p["wo"]] + [seg]
    return pl.pallas_call(
        _rwkv_out_body,
        grid=(st.b, st.n_tiles),
        in_specs=[st.tile_spec(), st.mod_spec(), _const_spec(g.shape), st.tile_spec(), st.tile_spec(),
                  st.tile_spec(), dir_spec, st.tile_spec(), dir_spec] + [_const_spec(c.shape) for c in consts],
        out_specs=st.tile_spec(),
        out_shape=st.out_shape(),
        compiler_params=_cparams("parallel", "parallel"),
        name="rwkv_readout",
    )(x, mod, g, y_f, y_b, r, kd, v, gate, *consts)


def _qkv_body(x_ref, mod_ref, g_ref, wq_ref, wqs_ref, wk_ref, wks_ref, wv_ref, gq_ref, gk_ref,
              cos_ref, sin_ref, q_out, k_out, v_out, *, qk_norm, q_scale):
    m = mod_ref[0]
    n = _norm_mod(x_ref[0], g_ref[...][2:3], m[4:5], m[3:4]).astype(BF16)
    cos, sin = cos_ref[...], sin_ref[...]

    def rotated(w_ref, ws_ref, gain_ref, out_ref, scale):
        t_all = jnp.dot(n, w_ref[...], preferred_element_type=F32)
        ts_all = jnp.dot(n, ws_ref[...], preferred_element_type=F32)
        for h in range(out_ref.shape[1]):
            t = t_all[:, h * LANES:h * LANES + HEAD_DIM]
            ts = ts_all[:, h * LANES:h * LANES + HEAD_DIM]
            if qk_norm:
                inv = lax.rsqrt(jnp.mean(t * t, axis=-1, keepdims=True) + NORM_EPS)
                gain = gain_ref[...]
                t, ts = t * inv * gain[0:1], ts * inv * gain[1:2]
            out = t * cos + ts * sin
            out_ref[0, h] = (out * scale if scale != 1.0 else out).astype(BF16)

    rotated(wq_ref, wqs_ref, gq_ref, q_out, q_scale)
    rotated(wk_ref, wks_ref, gk_ref, k_out, 1.0)
    v_all = jnp.dot(n, wv_ref[...], preferred_element_type=F32)
    ones_tail = (lax.broadcasted_iota(jnp.int32, (1, LANES), 1) >= HEAD_DIM).astype(F32)
    for h in range(v_out.shape[1]):
        v_out[0, h] = (v_all[:, h * LANES:(h + 1) * LANES] + ones_tail).astype(BF16)


def _qkv_project(st, x, mod, g, a, rope, qk_norm):
    nh, nkv = a["wq"].shape[1] // LANES, a["wk"].shape[1] // LANES
    consts = [a["wq"], a["wq_sw"], a["wk"], a["wk_sw"], a["wv"], a["gq"], a["gk"]]
    rope_spec = pl.BlockSpec((TILE, HEAD_DIM), lambda b, t: (t, 0))

    def head_spec(n, width):
        return pl.BlockSpec((1, n, TILE, width), lambda b, t: (b, 0, t, 0))

    def head_shape(n, width):
        return jax.ShapeDtypeStruct((st.b, n, st.t, width), BF16)

    return pl.pallas_call(
        functools.partial(_qkv_body, qk_norm=qk_norm, q_scale=HEAD_DIM ** -0.5 * LOG2_E),
        grid=(st.b, st.n_tiles),
        in_specs=[st.tile_spec(), st.mod_spec(), _const_spec(g.shape)]
                 + [_const_spec(c.shape) for c in consts] + [rope_spec, rope_spec],
        out_specs=[head_spec(nh, HEAD_DIM), head_spec(nkv, HEAD_DIM), head_spec(nkv, LANES)],
        out_shape=[head_shape(nh, HEAD_DIM), head_shape(nkv, HEAD_DIM), head_shape(nkv, LANES)],
        compiler_params=_cparams("parallel", "parallel"),
        name="qkv_rope",
    )(x, mod, g, *consts, *rope)


def _key_chunks(k0, k1):
    n = -(-(k1 - k0) // MAX_KEYS)
    size = -(-(k1 - k0) // (n * LANES)) * LANES
    return [(a, min(size, k1 - a)) for a in range(k0, k1, size)]


def _attn_body(*refs, ctx_len, n_ctx_tiles, total, windowed):
    if windowed:
        sink_ref, q_ref, k_ref, v_ref, o_ref = refs
    else:
        q_ref, k_ref, v_ref, o_ref = refs
    group = q_ref.shape[1]
    t = pl.program_id(2)
    tail = lax.broadcasted_iota(jnp.int32, (1, LANES), 1) >= HEAD_DIM
    chains = range(group)
    qs = [q_ref[0, c] for c in chains]

    def scores(k0, size):
        return [_dot_nt(q, k_ref[0, 0, pl.ds(k0, size), :]) for q in qs]

    def weigh(ps, k0, size):
        return [jnp.dot(p.astype(BF16), v_ref[0, 0, pl.ds(k0, size), :], preferred_element_type=F32) for p in ps]

    def finish(accs):
        invs = [1.0 / jnp.where(tail, acc, 1.0) for acc in accs]
        cols = []
        for c in range(0, group, 2):
            first = accs[c] * pltpu.roll(invs[c], HEAD_DIM, 1)
            second = pltpu.roll(accs[c + 1], HEAD_DIM, 1) * invs[c + 1]
            cols.append(jnp.where(tail, second, first))
        return jnp.concatenate(cols, axis=1)

    def row_max(xs):
        return [jnp.max(x, axis=-1, keepdims=True) for x in xs]

    if windowed:
        kv = pl.program_id(1)
        sinks = [jnp.full((TILE, 1), sink_ref[kv * group + c], F32) * LOG2_E for c in chains]

        def with_sink(s_parts, spans):
            ms = sinks
            for part in s_parts:
                ms = [jnp.maximum(m, pm) for m, pm in zip(ms, row_max(part))]
            accs = [jnp.where(tail, jnp.exp2(sk - m), 0.0) for sk, m in zip(sinks, ms)]
            for part, (k0, size) in zip(s_parts, spans):
                pvs = weigh([jnp.exp2(s - m) for s, m in zip(part, ms)], k0, size)
                accs = [acc + pv for acc, pv in zip(accs, pvs)]
            return finish(accs)

        def context_queries():
            return with_sink([scores(0, ctx_len)], [(0, ctx_len)])

        def latent_queries():
            k0 = pl.multiple_of(jnp.clip((t - 1) * TILE, 0, total - LOCAL_KEYS), TILE)
            k_pos = k0 + lax.broadcasted_iota(jnp.int32, (TILE, LOCAL_KEYS), 1)
            q_pos = t * TILE + lax.broadcasted_iota(jnp.int32, (TILE, LOCAL_KEYS), 0)
            mask = jnp.logical_and(jnp.abs(k_pos - q_pos) <= WINDOW, k_pos >= ctx_len)
            s_loc = [jnp.where(mask, s, NEG_INF) for s in scores(k0, LOCAL_KEYS)]
            return with_sink([scores(0, ctx_len), s_loc], [(0, ctx_len), (k0, LOCAL_KEYS)])
    else:
        def online(spans):
            ms = accs = None
            for k0, size in spans:
                ss = scores(k0, size)
                m_chunk = row_max(ss)
                if ms is None:
                    ms, accs = m_chunk, weigh([jnp.exp2(s - m) for s, m in zip(ss, m_chunk)], k0, size)
                else:
                    m_new = [jnp.maximum(m, mc) for m, mc in zip(ms, m_chunk)]
                    pvs = weigh([jnp.exp2(s - m) for s, m in zip(ss, m_new)], k0, size)
                    accs = [jnp.exp2(m - mn) * acc + pv for m, mn, acc, pv in zip(ms, m_new, accs, pvs)]
                    ms = m_new
            return finish(accs)

        def context_queries():
            return online([(0, ctx_len)])

        def latent_queries():
            return online(_key_chunks(0, total))

    o_ref[0] = lax.cond(t < n_ctx_tiles, context_queries, latent_queries).astype(BF16)


def _attention(st, q, k, v, sink):
    nh, nkv = q.shape[1], k.shape[1]
    group = nh // nkv
    windowed = sink is not None
    assert st.t >= LOCAL_KEYS
    q_spec = pl.BlockSpec((1, group, TILE, HEAD_DIM), lambda b, h, t: (b, h, t, 0))
    k_spec = pl.BlockSpec((1, 1, st.t, HEAD_DIM), lambda b, h, t: (b, h, 0, 0))
    v_spec = pl.BlockSpec((1, 1, st.t, LANES), lambda b, h, t: (b, h, 0, 0))
    specs, args = [q_spec, k_spec, v_spec], [q, k, v]
    if windowed:
        specs = [pl.BlockSpec(memory_space=pltpu.SMEM)] + specs
        args = [sink] + args
    return pl.pallas_call(
        functools.partial(_attn_body, ctx_len=st.ctx, n_ctx_tiles=st.n_ctx_tiles, total=st.t,
                          windowed=windowed),
        grid=(st.b, nkv, st.n_tiles),
        in_specs=specs,
        out_specs=pl.BlockSpec((1, TILE, group * HEAD_DIM), lambda b, h, t: (b, t, h)),
        out_shape=jax.ShapeDtypeStruct((st.b, st.t, nh * HEAD_DIM), BF16),
        compiler_params=_cparams("parallel", "parallel", "parallel"),
        name="window_attention" if windowed else "global_attention",
    )(*args)


def _attn_out_body(x_ref, mod_ref, g_ref, o_ref, wo_ref, out_ref):
    y = jnp.dot(o_ref[0], wo_ref[...], preferred_element_type=F32)
    m = mod_ref[0]
    out_ref[0] = x_ref[0] + m[5:6] * _rms(y, g_ref[...][3:4])


def _attn_out(st, x, mod, g, o, wo):
    return pl.pallas_call(
        _attn_out_body,
        grid=(st.b, st.n_tiles),
        in_specs=[st.tile_spec(), st.mod_spec(), _const_spec(g.shape),
                  st.tile_spec(o.shape[-1]), _const_spec(wo.shape)],
        out_specs=st.tile_spec(),
        out_shape=st.out_shape(),
        compiler_params=_cparams("parallel", "parallel"),
        name="attention_out",
    )(x, mod, g, o, wo)


def _pad_to(w, axis, mult):
    pad = (-w.shape[axis]) % mult
    if pad == 0:
        return w
    widths = [(0, 0)] * w.ndim
    widths[axis] = (0, pad)
    return jnp.pad(w, widths)


def _rope_tables(st):
    rows = (st.t - st.ctx) // GRID_W
    row = jnp.broadcast_to(jnp.arange(rows)[:, None], (rows, GRID_W)).reshape(-1)
    col = jnp.broadcast_to(jnp.arange(GRID_W)[None, :], (rows, GRID_W)).reshape(-1)
    inv_freq = ROPE_THETA ** (-jnp.arange(N_FREQ, dtype=F32) / N_FREQ)
    ang = jnp.stack([row, col], axis=-1).astype(F32)[:, :, None] * inv_freq
    cos, sin = jnp.cos(ang), jnp.sin(ang)
    cos_t = jnp.stack([cos, cos], axis=2).reshape(-1, HEAD_DIM)
    sin_t = jnp.stack([-sin, sin], axis=2).reshape(-1, HEAD_DIM)
    cos_t = jnp.concatenate([jnp.ones((st.ctx, HEAD_DIM), F32), cos_t], axis=0)
    sin_t = jnp.concatenate([jnp.zeros((st.ctx, HEAD_DIM), F32), sin_t], axis=0)
    return cos_t, sin_t


def _partner_perm():
    idx = np.arange(HEAD_DIM).reshape(2, 2, N_FREQ)
    return idx[:, ::-1, :].reshape(-1)


def _lane_groups(w):
    d = w.shape[0]
    w = w.reshape(d, -1, HEAD_DIM)
    return jnp.pad(w, ((0, 0), (0, 0), (0, LANES - HEAD_DIM))).reshape(d, -1).astype(BF16)


def _attn_params(wq, wk, wv, wo, gq, gk):
    perm = _partner_perm()

    def swapped(w):
        return w.reshape(w.shape[0], -1, HEAD_DIM)[:, :, perm].reshape(w.shape)

    return {"wq": _lane_groups(wq), "wq_sw": _lane_groups(swapped(wq)),
            "wk": _lane_groups(wk), "wk_sw": _lane_groups(swapped(wk)), "wv": _lane_groups(wv),
            "wo": wo.astype(BF16),
            "gq": jnp.stack([gq, gq[perm]]), "gk": jnp.stack([gk, gk[perm]])}


def _segment_ones():
    head = np.arange(SEG_TILE) // RWKV_HEAD
    return jnp.asarray(head[:, None] == head[None, :], dtype=BF16)


def kernel(x, c, ctx, c_ctx, mod_w, mod_b, norm_g, ffn_w1, ffn_w3, ffn_w2, rwkv_mu, rwkv_wr, rwkv_wk, rwkv_wv, rwkv_wo, rwkv_k_k, rwkv_k_a, rwkv_r_k, rwkv_w0, rwkv_w1, rwkv_w2, rwkv_a0, rwkv_a1, rwkv_a2, rwkv_g1, rwkv_g2, rwkv_ln_w, rwkv_ln_b, rwkv_v0, rwkv_v1, rwkv_v2, gattn_wq, gattn_wk, gattn_wv, gattn_wo, gattn_q_norm, gattn_k_norm, wattn_wq, wattn_wk, wattn_wv, wattn_wo, wattn_sink):
    batch, seq, d = x.shape
    depth = mod_w.shape[0]
    st = _Stream(batch, ctx.shape[1], seq, d)

    rows = -(-(batch + FFN_SEQS) // 8) * 8
    c_rows = jnp.zeros((rows, d), F32).at[:batch].set(c).at[batch:batch + FFN_SEQS].set(c_ctx)
    mods = _modulation(c_rows, mod_w, mod_b).reshape(depth, rows, N_MOD, d)

    w1, w3, w2 = ffn_w1.astype(BF16), ffn_w3.astype(BF16), ffn_w2.astype(BF16)
    rope = _rope_tables(st)
    seg = _segment_ones()
    bf = lambda w: w.astype(BF16)

    h = jnp.concatenate([ctx, x], axis=1)
    v_first = None
    for i in range(depth):
        kind, j = i % 3, i // 3
        g, mod = norm_g[i], mods[i]
        h = _ffn_half(st, h, mod, g, w1[i, 0], w3[i, 0], w2[i, 0], 0, 0)
        if kind == 0:
            p = {"mu": rwkv_mu[j], "wr": bf(rwkv_wr[j]), "wk": bf(rwkv_wk[j]), "wv": bf(rwkv_wv[j]),
                 "wo": bf(rwkv_wo[j]), "k_k": rwkv_k_k[j][None], "k_a": rwkv_k_a[j][None],
                 "r_k": rwkv_r_k[j].reshape(1, d),
                 "w0": rwkv_w0[j][:, None], "w1": bf(_pad_to(rwkv_w1[j], 2, LANES)), "w2": bf(_pad_to(rwkv_w2[j], 1, LANES)),
                 "a0": rwkv_a0[j][:, None], "a1": bf(_pad_to(rwkv_a1[j], 2, LANES)), "a2": bf(_pad_to(rwkv_a2[j], 1, LANES)),
                 "g1": bf(_pad_to(rwkv_g1[j], 2, LANES)), "g2": bf(_pad_to(rwkv_g2[j], 1, LANES)),
                 "ln_w": rwkv_ln_w[j][:, None], "ln_b": rwkv_ln_b[j][:, None]}
            vres = None
            if j > 0:
                vres = (v_first, rwkv_v0[j - 1][None], bf(_pad_to(rwkv_v1[j - 1], 1, LANES)),
                        bf(_pad_to(rwkv_v2[j - 1], 0, LANES)))
            r, v, kk, lw, kd, ab, gate = _rwkv_features(st, h, mod, g, p, seg, vres)
            if j == 0:
                v_first = v
            y_f = _rwkv_scan(st, r, lw, kd, v, kk, ab, 0)
            y_b = _rwkv_scan(st, r, lw, kd, v, kk, ab, 1)
            h = _rwkv_out(st, h, mod, g, y_f, y_b, r, kd, v, gate, p, seg)
        else:
            if kind == 1:
                a = _attn_params(gattn_wq[j], gattn_wk[j], gattn_wv[j], gattn_wo[j],
                                 gattn_q_norm[j], gattn_k_norm[j])
                sink = None
            else:
                ones = jnp.ones((HEAD_DIM,), F32)
                a = _attn_params(wattn_wq[j], wattn_wk[j], wattn_wv[j], wattn_wo[j], ones, ones)
                sink = wattn_sink[j]
            q, k, v = _qkv_project(st, h, mod, g, a, rope, qk_norm=kind == 1)
            o = _attention(st, q, k, v, sink)
            h = _attn_out(st, h, mod, g, o, a["wo"])
        h = _ffn_half(st, h, mod, g, w1[i, 1], w3[i, 1], w2[i, 1], 6, 4, latent_only=i == depth - 1)
    return h
```

```python
import functools

import numpy as np
import jax
import jax.numpy as jnp
from jax import lax
from jax.experimental import pallas as pl
from jax.experimental.pallas import tpu as pltpu

F32 = jnp.float32
BF16 = jnp.bfloat16
HIGHEST = lax.Precision.HIGHEST

GRID_W = 64
N_MOD = 9
HEAD_DIM = 64
N_KV_HEADS = 4
WINDOW = 128
ROPE_THETA = 10000.0
N_FREQ = HEAD_DIM // 4
RWKV_HEAD = 64
NORM_EPS = 1e-6
GN_EPS = 64e-5
L2_EPS = 1e-12
NEG_INF = -1e30
LOG2_E = 1.4426950408889634
EXP_NEG_HALF = 0.6065306597126334

LANES = 128
SUBLANES = 8
TILE = 256
FFN_SEQS = 2
SCAN_SEQS = 4
CHUNK = 64
MAX_KEYS = 1152
SEG_TILE = 256
LOCAL_KEYS = 3 * TILE
VMEM_LIMIT = 56 * 1024 * 1024


def _cparams(*sem):
    return pltpu.CompilerParams(dimension_semantics=sem, vmem_limit_bytes=VMEM_LIMIT)


def _const_spec(shape):
    nd = len(shape)
    return pl.BlockSpec(shape, lambda *_: (0,) * nd, pipeline_mode=pl.Buffered(1))


def _dot(a, b):
    return jnp.dot(a.astype(BF16), b.astype(BF16), preferred_element_type=F32)


def _dot_nt(a, b):
    return lax.dot_general(a.astype(BF16), b.astype(BF16), (((1,), (1,)), ((), ())),
                           preferred_element_type=F32)


def _dot_tn(a, b):
    return lax.dot_general(a.astype(BF16), b.astype(BF16), (((0,), (0,)), ((), ())),
                           preferred_element_type=F32)


def _dot_f32(a, b):
    return jnp.dot(a, b, precision=HIGHEST, preferred_element_type=F32)


def _sigmoid(x):
    return 1.0 / (1.0 + jnp.exp(-x))


def _rms(x, g):
    return x * lax.rsqrt(jnp.mean(x * x, axis=-1, keepdims=True) + NORM_EPS) * g


def _norm_mod(x, g, scale, shift):
    return _rms(x, g) * (1.0 + scale) + shift


def _seg_sum(x, ones_ref, pieces):
    ones = ones_ref[...]
    parts, rest = [], x
    for i in range(pieces):
        part = rest.astype(BF16)
        parts.append(part)
        if i + 1 < pieces:
            rest = rest - part.astype(F32)
    cols = []
    for j in range(x.shape[1] // SEG_TILE):
        sl = slice(j * SEG_TILE, (j + 1) * SEG_TILE)
        cols.append(sum(jnp.dot(part[:, sl], ones, preferred_element_type=F32) for part in parts))
    return jnp.concatenate(cols, axis=1)


def _mod_body(c_ref, w_ref, b_ref, o_ref):
    c = c_ref[...]
    o_ref[0] = _dot_f32(c * _sigmoid(c), w_ref[0]) + b_ref[0]


def _modulation(c_rows, mod_w, mod_b):
    depth, d, nd = mod_w.shape
    rows = c_rows.shape[0]
    tn = nd // 6
    return pl.pallas_call(
        _mod_body,
        grid=(depth, nd // tn),
        in_specs=[_const_spec((rows, d)),
                  pl.BlockSpec((1, d, tn), lambda i, j: (i, 0, j)),
                  pl.BlockSpec((1, 1, tn), lambda i, j: (i, 0, j))],
        out_specs=pl.BlockSpec((1, rows, tn), lambda i, j: (i, 0, j)),
        out_shape=jax.ShapeDtypeStruct((depth, rows, nd), F32),
        compiler_params=_cparams("arbitrary", "arbitrary"),
        name="adaln_rows",
    )(c_rows, mod_w, mod_b.reshape(depth, 1, nd))


class _Stream:
    def __init__(self, batch, ctx_len, seq, d):
        assert ctx_len % TILE == 0 and seq % TILE == 0 and d % LANES == 0
        self.b, self.t, self.d = batch, ctx_len + seq, d
        self.ctx = ctx_len
        self.n_tiles = self.t // TILE
        self.n_ctx_tiles = ctx_len // TILE

    def tile_spec(self, width=None):
        return pl.BlockSpec((1, TILE, width or self.d), lambda b, t: (b, t, 0))

    def mod_spec(self):
        nct, ctx_row = self.n_ctx_tiles, self.b
        return pl.BlockSpec((1, N_MOD, self.d), lambda b, t: (jnp.where(t < nct, ctx_row, b), 0, 0))

    def out_shape(self, width=None, dtype=F32):
        return jax.ShapeDtypeStruct((self.b, self.t, width or self.d), dtype)


def _ffn_body(*refs, mod_row, g_row, with_mixer, n_ctx_tiles, split_input):
    refs = list(refs)
    if split_input:
        ctx_ref, lat_ref = refs[:2]
        x = jnp.where(pl.program_id(1) < n_ctx_tiles, ctx_ref[...], lat_ref[...])
        refs = refs[2:]
    else:
        x = refs.pop(0)[...]
    if with_mixer:
        mod_ref, g_ref, w1_ref, w3_ref, w2_ref, o_ref, wo_ref, out_ref = refs
    else:
        mod_ref, g_ref, w1_ref, w3_ref, w2_ref, out_ref = refs
    m = mod_ref[...]
    g = g_ref[...]
    if with_mixer:
        o = o_ref[...]
        mixed = jnp.dot(o.reshape(-1, o.shape[-1]), wo_ref[...], preferred_element_type=F32).reshape(x.shape)
        x = x + m[:, 5:6] * _rms(mixed, g[3:4])
    shift, scale, gate = m[:, mod_row:mod_row + 1], m[:, mod_row + 1:mod_row + 2], m[:, mod_row + 2:mod_row + 3]
    n = _norm_mod(x, g[g_row:g_row + 1], scale, shift).astype(BF16).reshape(-1, x.shape[-1])
    a = jnp.dot(n, w1_ref[0, 0], preferred_element_type=F32)
    b = jnp.dot(n, w3_ref[0, 0], preferred_element_type=F32)
    h = (a * _sigmoid(a) * b).astype(BF16)
    y = jnp.dot(h, w2_ref[0, 0], preferred_element_type=F32).reshape(x.shape)
    out_ref[...] = x + 0.5 * gate * _rms(y, g[g_row + 1:g_row + 2])


def _ffn_half(st, x, mod, g, weights, layer, half, latent_only=False, mixer=None):
    w1, w3, w2 = weights
    d, f = w1.shape[-2:]
    nb = FFN_SEQS if st.b % FFN_SEQS == 0 else 1
    nct, ctx_blk = st.n_ctx_tiles, st.b // nb
    skip = nct if latent_only else 0
    split_input = isinstance(x, tuple)

    def rows_spec(width):
        return pl.BlockSpec((nb, TILE, width), lambda b, t: (b, t + skip, 0))

    def weight_spec(w):
        return pl.BlockSpec((1, 1) + w.shape[-2:], lambda b, t: (layer, half, 0, 0), pipeline_mode=pl.Buffered(1))

    if split_input:
        specs = [pl.BlockSpec((nb, TILE, d), lambda b, t: (b, jnp.minimum(t, nct - 1), 0)),
                 pl.BlockSpec((nb, TILE, d), lambda b, t: (b, jnp.maximum(t - nct, 0), 0))]
        args = list(x)
    else:
        specs, args = [rows_spec(d)], [x]
    mod_spec = pl.BlockSpec((nb, N_MOD, d), lambda b, t: (jnp.where(t + skip < nct, ctx_blk, b), 0, 0))
    specs += [mod_spec, _const_spec(g.shape), weight_spec(w1), weight_spec(w3), weight_spec(w2)]
    args += [mod, g, w1, w3, w2]
    if mixer is not None:
        o, wo = mixer
        specs += [rows_spec(o.shape[-1]), _const_spec(wo.shape)]
        args += [o, wo]
    return pl.pallas_call(
        functools.partial(_ffn_body, mod_row=6 * half, g_row=4 * half, with_mixer=mixer is not None,
                          n_ctx_tiles=nct, split_input=split_input),
        grid=(st.b // nb, st.n_tiles - skip),
        in_specs=specs,
        out_specs=pl.BlockSpec((nb, TILE, d), lambda b, t: (b, t, 0)),
        out_shape=jax.ShapeDtypeStruct((st.b, st.t - skip * TILE, d), F32),
        compiler_params=_cparams("parallel", "parallel"),
        name="swiglu_half",
    )(*args)


def _rwkv_feat_body(*refs, n_ctx_tiles, n_tiles, with_vres):
    (x_ref, xp_ref, xn_ref, mod_ref, g_ref, mu_ref, wr_ref, wk_ref, wv_ref, kk_ref, ka_ref,
     w0_ref, w1_ref, w2_ref, a0_ref, a1_ref, a2_ref, g1_ref, g2_ref, ones_ref) = refs[:20]
    rest = refs[20:]
    if with_vres:
        vf_ref, v0_ref, v1_ref, v2_ref = rest[:4]
        rest = rest[4:]
    r_out, v_out, kk_out, lw_out, kd_out, ab_out, g_out = rest

    t = pl.program_id(1)
    m = mod_ref[0]
    gn = g_ref[...][2:3]
    shift, scale = m[3:4], m[4:5]
    n = _norm_mod(x_ref[0], gn, scale, shift)
    first = jnp.logical_or(t == 0, t == n_ctx_tiles)
    last = jnp.logical_or(t == n_ctx_tiles - 1, t == n_tiles - 1)
    n_before = jnp.where(first, 0.0, _norm_mod(xp_ref[0][7:8], gn, scale, shift))
    n_after = jnp.where(last, 0.0, _norm_mod(xn_ref[0][0:1], gn, scale, shift))
    row = lax.broadcasted_iota(jnp.int32, (TILE, 1), 0)
    n_prev = jnp.where(row == 0, n_before, pltpu.roll(n, 1, 0))
    n_next = jnp.where(row == TILE - 1, n_after, pltpu.roll(n, TILE - 1, 0))
    xx = 0.5 * (n_prev + n_next) - n
    mu = mu_ref[...]
    xr, xw, xk, xv, xa, xg = (n + xx * mu[i:i + 1] for i in range(6))

    r = _dot(xr, wr_ref[...])
    k = _dot(xk, wk_ref[...])
    v = _dot(xv, wv_ref[...])
    if with_vres:
        lora = _dot(_dot(xv, v1_ref[...]), v2_ref[...])
        v = v + (vf_ref[0] - v) * _sigmoid(v0_ref[...] + lora)
    kk = k * kk_ref[...]
    kk = kk / jnp.maximum(jnp.sqrt(_seg_sum(kk * kk, ones_ref, 2)), L2_EPS)
    r_out[0] = r
    v_out[0] = v
    kk_out[0] = kk
    k_a = ka_ref[...]
    for dr in range(2):
        wl = w0_ref[dr] + _dot(jnp.tanh(_dot(xw, w1_ref[dr])), w2_ref[dr])
        a = _sigmoid(a0_ref[dr] + _dot(_dot(xa, a1_ref[dr]), a2_ref[dr]))
        lw_out[dr, 0] = -EXP_NEG_HALF * _sigmoid(wl)
        kd_out[dr, 0] = k * (1.0 + (a - 1.0) * k_a)
        ab_out[dr, 0] = kk * a
        g_out[dr, 0] = _dot(_sigmoid(_dot(xg, g1_ref[dr])), g2_ref[dr])


def _rwkv_features(st, x, mod, g, p, seg, vres):
    d = st.d
    n8 = st.t // 8
    per8 = TILE // 8
    prev_spec = pl.BlockSpec((1, 8, d), lambda b, t: (b, jnp.maximum(t * per8 - 1, 0), 0))
    next_spec = pl.BlockSpec((1, 8, d), lambda b, t: (b, jnp.minimum((t + 1) * per8, n8 - 1), 0))
    dir_spec = pl.BlockSpec((2, 1, TILE, d), lambda b, t: (0, b, t, 0))
    dir_shape = jax.ShapeDtypeStruct((2, st.b, st.t, d), F32)
    names = ("mu", "wr", "wk", "wv", "k_k", "k_a", "w0", "w1", "w2", "a0", "a1", "a2", "g1", "g2")
    consts = [p[nm] for nm in names] + [seg]
    args = [x, x, x, mod, g] + consts
    specs = [st.tile_spec(), prev_spec, next_spec, st.mod_spec(), _const_spec(g.shape)]
    specs += [_const_spec(c.shape) for c in consts]
    if vres is not None:
        v_first, v0, v1, v2 = vres
        args += [v_first, v0, v1, v2]
        specs += [st.tile_spec(), _const_spec(v0.shape), _const_spec(v1.shape), _const_spec(v2.shape)]
    return pl.pallas_call(
        functools.partial(_rwkv_feat_body, n_ctx_tiles=st.n_ctx_tiles, n_tiles=st.n_tiles,
                          with_vres=vres is not None),
        grid=(st.b, st.n_tiles),
        in_specs=specs,
        out_specs=[st.tile_spec()] * 3 + [dir_spec] * 4,
        out_shape=[st.out_shape()] * 3 + [dir_shape] * 4,
        compiler_params=_cparams("parallel", "parallel"),
        name="rwkv_features",
    )(*args)


def _unit_triangular_inverses(mats, ri, ci, size, upper):
    n = mats[0].shape[0]
    eye = (ri == ci).astype(F32)
    ts = [jnp.where((ri >> 1) == (ci >> 1), a, 0.0) + eye for a in mats]
    level = 1
    while (2 << level) <= size:
        m = 1 << level
        off = jnp.logical_and((ri >> (level + 1)) == (ci >> (level + 1)), (ri >> level) != (ci >> level))
        a_off = [jnp.where(off, a, 0.0) for a in mats]
        if m % SUBLANES == 0:
            moving = range(0 if upper else 1, n // m, 2)

            def rows(x):
                return jnp.concatenate([x[b * m:(b + 1) * m] for b in moving], axis=0)

            zs = [_dot(_dot(rows(t), a), t) for t, a in zip(ts, a_off)]
            ts = [jnp.concatenate([t[b * m:(b + 1) * m] + z[(b // 2) * m:(b // 2 + 1) * m] if b in moving
                                   else t[b * m:(b + 1) * m] for b in range(n // m)], axis=0)
                  for t, z in zip(ts, zs)]
        else:
            ts = [t + _dot(_dot(t, a), t) for t, a in zip(ts, a_off)]
        level += 1
    return ts


def _scan_body(r_ref, lw_ref, k_ref, v_ref, kk_ref, ab_ref, y_ref, s_ref, *, reverse):
    c = CHUNK
    n2 = 2 * c

    @pl.when(pl.program_id(1) == 0)
    def _():
        s_ref[...] = jnp.zeros_like(s_ref)

    ti = lax.broadcasted_iota(jnp.int32, (c, c), 0)
    si = lax.broadcasted_iota(jnp.int32, (c, c), 1)
    upto = ((si >= ti) if reverse else (si <= ti)).astype(F32)
    ri = lax.broadcasted_iota(jnp.int32, (n2, n2), 0)
    ci = lax.broadcasted_iota(jnp.int32, (n2, n2), 1)
    same = (ri >> 6) == (ci >> 6)
    tt, ss = ri & (c - 1), ci & (c - 1)
    before = (ss > tt) if reverse else (ss < tt)
    strict = jnp.logical_and(same, before)
    incl = jnp.logical_and(same, jnp.logical_or(before, ss == tt))
    incl2 = jnp.concatenate([incl, incl], axis=1)
    even = lax.broadcasted_iota(jnp.int32, (1, LANES), 1) < RWKV_HEAD

    def stack(x):
        return jnp.concatenate([jnp.where(even, x, 0.0), jnp.where(even, 0.0, x)], axis=0)

    n_pairs = r_ref.shape[-1] // LANES
    chains = [(q, p) for q in range(r_ref.shape[0]) for p in range(n_pairs)]
    cols = {}
    for q in range(r_ref.shape[0]):
        r, lw, k, v, kk, ab = r_ref[q], lw_ref[0, q], k_ref[0, q], v_ref[q], kk_ref[q], ab_ref[0, q]
        cum = _dot_f32(upto, lw)
        tot = cum[0:1] if reverse else cum[c - 1:c]
        inv = jnp.exp(-cum)
        to_end = jnp.exp(tot - cum)
        full = {"a": -kk * jnp.exp(cum - lw), "r": r * jnp.exp(cum), "b": ab * inv, "k": k * inv,
                "be": ab * to_end, "ke": k * to_end, "w": jnp.exp(tot), "v": v}
        for p in range(n_pairs):
            cols[q, p] = {nm: x[:, p * LANES:(p + 1) * LANES] for nm, x in full.items()}

    prods = [_dot_nt(jnp.concatenate([stack(cols[ch]["a"]), stack(cols[ch]["r"])], axis=0),
                     jnp.concatenate([stack(cols[ch]["b"]), stack(cols[ch]["k"])], axis=0)) for ch in chains]
    a_b = [jnp.where(strict, pr[0:n2, 0:n2], 0.0) for pr in prods]
    t_inv = _unit_triangular_inverses(a_b, ri, ci, c, reverse)
    v_st = [stack(cols[ch]["v"]) for ch in chains]
    u_own = [_dot(jnp.where(strict, pr[0:n2, n2:2 * n2], 0.0), vs) for pr, vs in zip(prods, v_st)]
    from_state = [_dot_nt(jnp.concatenate([cols[ch]["a"], cols[ch]["r"]], axis=0), s_ref[ch])
                  for ch in chains]
    u_st = [_dot(t, stack(fs[0:c]) + uo) for t, fs, uo in zip(t_inv, from_state, u_own)]
    y_st = [_dot(jnp.where(incl2, pr[n2:2 * n2], 0.0), jnp.concatenate([us, vs], axis=0))
            for pr, us, vs in zip(prods, u_st, v_st)]
    for i, (q, p) in enumerate(chains):
        y_ref[q, :, p * LANES:(p + 1) * LANES] = from_state[i][c:2 * c] + y_st[i][0:c] + y_st[i][c:n2]
        u = u_st[i][0:c] + u_st[i][c:n2]
        upd = _dot_tn(jnp.concatenate([u, cols[q, p]["v"]], axis=0),
                      jnp.concatenate([cols[q, p]["be"], cols[q, p]["ke"]], axis=0))
        s_ref[q, p] = s_ref[q, p] * cols[q, p]["w"] + jnp.where(same, upd, 0.0)


def _rwkv_scan(st, r, lw, kd, v, kk, ab, direction):
    d = st.d
    nq = SCAN_SEQS if st.b % SCAN_SEQS == 0 else 1
    nc, ncc = st.t // CHUNK, st.ctx // CHUNK
    if direction == 0:
        def cidx(s):
            return s
    else:
        def cidx(s):
            return jnp.where(s < ncc, ncc - 1 - s, nc - 1 - s + ncc)
    shared = pl.BlockSpec((nq, CHUNK, d), lambda b, s: (b, cidx(s), 0))
    per_dir = pl.BlockSpec((1, nq, CHUNK, d), lambda b, s: (direction, b, cidx(s), 0))
    return pl.pallas_call(
        functools.partial(_scan_body, reverse=direction == 1),
        grid=(st.b // nq, nc),
        in_specs=[shared, per_dir, per_dir, shared, shared, per_dir],
        out_specs=shared,
        out_shape=st.out_shape(),
        scratch_shapes=[pltpu.VMEM((nq, d // LANES, LANES, LANES), F32)],
        compiler_params=_cparams("parallel", "arbitrary"),
        name="rwkv_scan_bwd" if direction else "rwkv_scan_fwd",
    )(r, lw, kd, v, kk, ab)


def _rwkv_out_body(x_ref, mod_ref, g_ref, yf_ref, yb_ref, r_ref, kd_ref, v_ref, gate_ref,
                   lnw_ref, lnb_ref, rk_ref, wo_ref, ones_ref, o_ref):
    r, v = r_ref[0], v_ref[0]
    inv_n = 1.0 / RWKV_HEAD
    o = None
    for dr, y_ref in enumerate((yf_ref, yb_ref)):
        y = y_ref[0]
        cen = y - _seg_sum(y, ones_ref, 3) * inv_n
        var = _seg_sum(cen * cen, ones_ref, 1) * inv_n
        yn = cen * lax.rsqrt(var + GN_EPS) * lnw_ref[dr] + lnb_ref[dr]
        bonus = _seg_sum(r * kd_ref[dr, 0] * rk_ref[...], ones_ref, 1) * v
        od = (yn + bonus) * gate_ref[dr, 0]
        o = od if o is None else o + od
    y_mix = _dot(o, wo_ref[...])
    m = mod_ref[0]
    o_ref[0] = x_ref[0] + m[5:6] * _rms(y_mix, g_ref[...][3:4])


def _rwkv_out(st, x, mod, g, y_f, y_b, r, kd, v, gate, p, seg):
    d = st.d
    dir_spec = pl.BlockSpec((2, 1, TILE, d), lambda b, t: (0, b, t, 0))
    consts = [p["ln_w"], p["ln_b"], p["r_k"], p["wo"]] + [seg]
    return pl.pallas_call(
        _rwkv_out_body,
        grid=(st.b, st.n_tiles),
        in_specs=[st.tile_spec(), st.mod_spec(), _const_spec(g.shape), st.tile_spec(), st.tile_spec(),
                  st.tile_spec(), dir_spec, st.tile_spec(), dir_spec] + [_const_spec(c.shape) for c in consts],
        out_specs=st.tile_spec(),
        out_shape=st.out_shape(),
        compiler_params=_cparams("parallel", "parallel"),
        name="rwkv_readout",
    )(x, mod, g, y_f, y_b, r, kd, v, gate, *consts)


def _qkv_body(x_ref, mod_ref, g_ref, wq_ref, wk_ref, wv_ref, gq_ref, gk_ref, ones_ref,
              cos_ref, sin_ref, q_out, k_out, v_out, *, qk_norm, q_scale):
    m = mod_ref[0]
    n = _norm_mod(x_ref[0], g_ref[...][2:3], m[4:5], m[3:4]).astype(BF16)
    cos, sin = cos_ref[...], sin_ref[...]
    lane = lax.broadcasted_iota(jnp.int32, (1, LANES), 1)
    second_half = (lane & N_FREQ) != 0
    tail = lane >= HEAD_DIM

    def rotated(w_ref, gain_ref, out_ref, scale):
        t_all = jnp.dot(n, w_ref[...], preferred_element_type=F32)
        if qk_norm:
            mean_sq = _seg_sum(t_all * t_all, ones_ref, 2) * (1.0 / HEAD_DIM)
            t_all = t_all * lax.rsqrt(mean_sq + NORM_EPS) * gain_ref[...]
        for j in range(t_all.shape[1] // LANES):
            t = t_all[:, j * LANES:(j + 1) * LANES]
            partner = jnp.where(second_half, pltpu.roll(t, N_FREQ, 1), pltpu.roll(t, LANES - N_FREQ, 1))
            out = t * cos + partner * sin
            out = (out * scale if scale != 1.0 else out).astype(BF16)
            out_ref[0, 2 * j] = out[:, 0:HEAD_DIM]
            out_ref[0, 2 * j + 1] = out[:, HEAD_DIM:LANES]

    rotated(wq_ref, gq_ref, q_out, q_scale)
    rotated(wk_ref, gk_ref, k_out, 1.0)
    v_all = jnp.dot(n, wv_ref[...], preferred_element_type=F32)
    for j in range(v_all.shape[1] // LANES):
        v2 = v_all[:, j * LANES:(j + 1) * LANES]
        v_out[0, 2 * j] = jnp.where(tail, 1.0, v2).astype(BF16)
        v_out[0, 2 * j + 1] = jnp.where(tail, 1.0, pltpu.roll(v2, HEAD_DIM, 1)).astype(BF16)


def _qkv_project(st, x, mod, g, a, rope, seg, qk_norm):
    nh, nkv = a["wq"].shape[1] // HEAD_DIM, a["wk"].shape[1] // HEAD_DIM
    consts = [a["wq"], a["wk"], a["wv"], a["gq"], a["gk"], seg]
    rope_spec = pl.BlockSpec((TILE, LANES), lambda b, t: (t, 0))

    def head_spec(n, width):
        return pl.BlockSpec((1, n, TILE, width), lambda b, t: (b, 0, t, 0))

    def head_shape(n, width):
        return jax.ShapeDtypeStruct((st.b, n, st.t, width), BF16)

    return pl.pallas_call(
        functools.partial(_qkv_body, qk_norm=qk_norm, q_scale=HEAD_DIM ** -0.5 * LOG2_E),
        grid=(st.b, st.n_tiles),
        in_specs=[st.tile_spec(), st.mod_spec(), _const_spec(g.shape)]
                 + [_const_spec(c.shape) for c in consts] + [rope_spec, rope_spec],
        out_specs=[head_spec(nh, HEAD_DIM), head_spec(nkv, HEAD_DIM), head_spec(nkv, LANES)],
        out_shape=[head_shape(nh, HEAD_DIM), head_shape(nkv, HEAD_DIM), head_shape(nkv, LANES)],
        compiler_params=_cparams("parallel", "parallel"),
        name="qkv_rope",
    )(x, mod, g, *consts, *rope)


def _key_chunks(k0, k1):
    n = -(-(k1 - k0) // MAX_KEYS)
    size = -(-(k1 - k0) // (n * LANES)) * LANES
    return [(a, min(size, k1 - a)) for a in range(k0, k1, size)]


def _attn_body(*refs, ctx_len, n_ctx_tiles, total, windowed):
    if windowed:
        sink_ref, q_ref, k_ref, v_ref, o_ref = refs
    else:
        q_ref, k_ref, v_ref, o_ref = refs
    group = q_ref.shape[1]
    t = pl.program_id(2)
    tail = lax.broadcasted_iota(jnp.int32, (1, LANES), 1) >= HEAD_DIM
    chains = range(group)
    qs = [q_ref[0, c] for c in chains]

    def scores(k0, size):
        return [_dot_nt(q, k_ref[0, 0, pl.ds(k0, size), :]) for q in qs]

    def weigh(ps, k0, size):
        return [jnp.dot(p.astype(BF16), v_ref[0, 0, pl.ds(k0, size), :], preferred_element_type=F32) for p in ps]

    def finish(accs):
        invs = [1.0 / jnp.where(tail, acc, 1.0) for acc in accs]
        cols = []
        for c in range(0, group, 2):
            first = accs[c] * pltpu.roll(invs[c], HEAD_DIM, 1)
            second = pltpu.roll(accs[c + 1], HEAD_DIM, 1) * invs[c + 1]
            cols.append(jnp.where(tail, second, first))
        return jnp.concatenate(cols, axis=1)

    def row_max(xs):
        return [jnp.max(x, axis=-1, keepdims=True) for x in xs]

    if windowed:
        kv = pl.program_id(1)
        sinks = [jnp.full((TILE, 1), sink_ref[kv * group + c], F32) * LOG2_E for c in chains]

        def with_sink(s_parts, spans):
            ms = sinks
            for part in s_parts:
                ms = [jnp.maximum(m, pm) for m, pm in zip(ms, row_max(part))]
            accs = [jnp.where(tail, jnp.exp2(sk - m), 0.0) for sk, m in zip(sinks, ms)]
            for part, (k0, size) in zip(s_parts, spans):
                pvs = weigh([jnp.exp2(s - m) for s, m in zip(part, ms)], k0, size)
                accs = [acc + pv for acc, pv in zip(accs, pvs)]
            return finish(accs)

        def context_queries():
            return with_sink([scores(0, ctx_len)], [(0, ctx_len)])

        def latent_queries():
            k0 = pl.multiple_of(jnp.clip((t - 1) * TILE, 0, total - LOCAL_KEYS), TILE)
            k_pos = k0 + lax.broadcasted_iota(jnp.int32, (TILE, LOCAL_KEYS), 1)
            q_pos = t * TILE + lax.broadcasted_iota(jnp.int32, (TILE, LOCAL_KEYS), 0)
            mask = jnp.logical_and(jnp.abs(k_pos - q_pos) <= WINDOW, k_pos >= ctx_len)
            s_loc = [jnp.where(mask, s, NEG_INF) for s in scores(k0, LOCAL_KEYS)]
            return with_sink([scores(0, ctx_len), s_loc], [(0, ctx_len), (k0, LOCAL_KEYS)])
    else:
        def online(spans):
            ms = accs = None
            for k0, size in spans:
                ss = scores(k0, size)
                m_chunk = row_max(ss)
                if ms is None:
                    ms, accs = m_chunk, weigh([jnp.exp2(s - m) for s, m in zip(ss, m_chunk)], k0, size)
                else:
                    m_new = [jnp.maximum(m, mc) for m, mc in zip(ms, m_chunk)]
                    pvs = weigh([jnp.exp2(s - m) for s, m in zip(ss, m_new)], k0, size)
                    accs = [jnp.exp2(m - mn) * acc + pv for m, mn, acc, pv in zip(ms, m_new, accs, pvs)]
                    ms = m_new
            return finish(accs)

        def context_queries():
            return online([(0, ctx_len)])

        def latent_queries():
            return online(_key_chunks(0, total))

    o_ref[0] = lax.cond(t < n_ctx_tiles, context_queries, latent_queries).astype(BF16)


def _attention(st, q, k, v, sink):
    nh, nkv = q.shape[1], k.shape[1]
    group = nh // nkv
    windowed = sink is not None
    assert st.t >= LOCAL_KEYS
    q_spec = pl.BlockSpec((1, group, TILE, HEAD_DIM), lambda b, h, t: (b, h, t, 0))
    k_spec = pl.BlockSpec((1, 1, st.t, HEAD_DIM), lambda b, h, t: (b, h, 0, 0))
    v_spec = pl.BlockSpec((1, 1, st.t, LANES), lambda b, h, t: (b, h, 0, 0))
    specs, args = [q_spec, k_spec, v_spec], [q, k, v]
    if windowed:
        specs = [pl.BlockSpec(memory_space=pltpu.SMEM)] + specs
        args = [sink] + args
    return pl.pallas_call(
        functools.partial(_attn_body, ctx_len=st.ctx, n_ctx_tiles=st.n_ctx_tiles, total=st.t,
                          windowed=windowed),
        grid=(st.b, nkv, st.n_tiles),
        in_specs=specs,
        out_specs=pl.BlockSpec((1, TILE, group * HEAD_DIM), lambda b, h, t: (b, t, h)),
        out_shape=jax.ShapeDtypeStruct((st.b, st.t, nh * HEAD_DIM), BF16),
        compiler_params=_cparams("parallel", "parallel", "parallel"),
        name="window_attention" if windowed else "global_attention",
    )(*args)


def _pad_to(w, axis, mult):
    pad = (-w.shape[axis]) % mult
    if pad == 0:
        return w
    widths = [(0, 0)] * w.ndim
    widths[axis] = (0, pad)
    return jnp.pad(w, widths)


def _rope_tables(st):
    rows = (st.t - st.ctx) // GRID_W
    row = jnp.broadcast_to(jnp.arange(rows)[:, None], (rows, GRID_W)).reshape(-1)
    col = jnp.broadcast_to(jnp.arange(GRID_W)[None, :], (rows, GRID_W)).reshape(-1)
    inv_freq = ROPE_THETA ** (-jnp.arange(N_FREQ, dtype=F32) / N_FREQ)
    ang = jnp.stack([row, col], axis=-1).astype(F32)[:, :, None] * inv_freq
    cos, sin = jnp.cos(ang), jnp.sin(ang)
    cos_t = jnp.stack([cos, cos], axis=2).reshape(-1, HEAD_DIM)
    sin_t = jnp.stack([-sin, sin], axis=2).reshape(-1, HEAD_DIM)
    cos_t = jnp.concatenate([jnp.ones((st.ctx, HEAD_DIM), F32), cos_t], axis=0)
    sin_t = jnp.concatenate([jnp.zeros((st.ctx, HEAD_DIM), F32), sin_t], axis=0)
    return jnp.tile(cos_t, (1, LANES // HEAD_DIM)), jnp.tile(sin_t, (1, LANES // HEAD_DIM))


def _attn_params(wq, wk, wv, wo, gq, gk):
    nh, nkv = wq.shape[1] // HEAD_DIM, wk.shape[1] // HEAD_DIM
    return {"wq": wq.astype(BF16), "wk": wk.astype(BF16), "wv": wv.astype(BF16), "wo": wo.astype(BF16),
            "gq": jnp.tile(gq, nh)[None], "gk": jnp.tile(gk, nkv)[None]}


def _segment_ones():
    head = np.arange(SEG_TILE) // RWKV_HEAD
    return jnp.asarray(head[:, None] == head[None, :], dtype=BF16)


def kernel(x, c, ctx, c_ctx, mod_w, mod_b, norm_g, ffn_w1, ffn_w3, ffn_w2, rwkv_mu, rwkv_wr, rwkv_wk, rwkv_wv, rwkv_wo, rwkv_k_k, rwkv_k_a, rwkv_r_k, rwkv_w0, rwkv_w1, rwkv_w2, rwkv_a0, rwkv_a1, rwkv_a2, rwkv_g1, rwkv_g2, rwkv_ln_w, rwkv_ln_b, rwkv_v0, rwkv_v1, rwkv_v2, gattn_wq, gattn_wk, gattn_wv, gattn_wo, gattn_q_norm, gattn_k_norm, wattn_wq, wattn_wk, wattn_wv, wattn_wo, wattn_sink):
    batch, seq, d = x.shape
    depth = mod_w.shape[0]
    st = _Stream(batch, ctx.shape[1], seq, d)

    rows = -(-(batch + FFN_SEQS) // 8) * 8
    c_rows = jnp.zeros((rows, d), F32).at[:batch].set(c).at[batch:batch + FFN_SEQS].set(c_ctx)
    mods = _modulation(c_rows, mod_w, mod_b).reshape(depth, rows, N_MOD, d)

    ffn_w = (ffn_w1.astype(BF16), ffn_w3.astype(BF16), ffn_w2.astype(BF16))
    rope = _rope_tables(st)
    seg = _segment_ones()
    bf = lambda w: w.astype(BF16)

    h = (ctx, x)
    v_first = None
    for i in range(depth):
        kind, j = i % 3, i // 3
        g, mod = norm_g[i], mods[i]
        h = _ffn_half(st, h, mod, g, ffn_w, i, 0)
        mixer = None
        if kind == 0:
            p = {"mu": rwkv_mu[j], "wr": bf(rwkv_wr[j]), "wk": bf(rwkv_wk[j]), "wv": bf(rwkv_wv[j]),
                 "wo": bf(rwkv_wo[j]), "k_k": rwkv_k_k[j][None], "k_a": rwkv_k_a[j][None],
                 "r_k": rwkv_r_k[j].reshape(1, d),
                 "w0": rwkv_w0[j][:, None], "w1": bf(_pad_to(rwkv_w1[j], 2, LANES)), "w2": bf(_pad_to(rwkv_w2[j], 1, LANES)),
                 "a0": rwkv_a0[j][:, None], "a1": bf(_pad_to(rwkv_a1[j], 2, LANES)), "a2": bf(_pad_to(rwkv_a2[j], 1, LANES)),
                 "g1": bf(_pad_to(rwkv_g1[j], 2, LANES)), "g2": bf(_pad_to(rwkv_g2[j], 1, LANES)),
                 "ln_w": rwkv_ln_w[j][:, None], "ln_b": rwkv_ln_b[j][:, None]}
            vres = None
            if j > 0:
                vres = (v_first, rwkv_v0[j - 1][None], bf(_pad_to(rwkv_v1[j - 1], 1, LANES)),
                        bf(_pad_to(rwkv_v2[j - 1], 0, LANES)))
            r, v, kk, lw, kd, ab, gate = _rwkv_features(st, h, mod, g, p, seg, vres)
            if j == 0:
                v_first = v
            y_f = _rwkv_scan(st, r, lw, kd, v, kk, ab, 0)
            y_b = _rwkv_scan(st, r, lw, kd, v, kk, ab, 1)
            h = _rwkv_out(st, h, mod, g, y_f, y_b, r, kd, v, gate, p, seg)
        else:
            if kind == 1:
                a = _attn_params(gattn_wq[j], gattn_wk[j], gattn_wv[j], gattn_wo[j],
                                 gattn_q_norm[j], gattn_k_norm[j])
                sink = None
            else:
                ones = jnp.ones((HEAD_DIM,), F32)
                a = _attn_params(wattn_wq[j], wattn_wk[j], wattn_wv[j], wattn_wo[j], ones, ones)
                sink = wattn_sink[j]
            q, k, v = _qkv_project(st, h, mod, g, a, rope, seg, qk_norm=kind == 1)
            mixer = (_attention(st, q, k, v, sink), a["wo"])
        h = _ffn_half(st, h, mod, g, ffn_w, i, 1, latent_only=i == depth - 1, mixer=mixer)
    return h
```

```python
import functools

import numpy as np
import jax
import jax.numpy as jnp
from jax import lax
from jax.experimental import pallas as pl
from jax.experimental.pallas import tpu as pltpu

F32 = jnp.float32
BF16 = jnp.bfloat16
HIGHEST = lax.Precision.HIGHEST

GRID_W = 64
N_MOD = 9
HEAD_DIM = 64
N_KV_HEADS = 4
WINDOW = 128
ROPE_THETA = 10000.0
N_FREQ = HEAD_DIM // 4
RWKV_HEAD = 64
NORM_EPS = 1e-6
GN_EPS = 64e-5
L2_EPS = 1e-12
NEG_INF = -1e30
LOG2_E = 1.4426950408889634
EXP_NEG_HALF = 0.6065306597126334

LANES = 128
SUBLANES = 8
TILE = 256
FFN_SEQS = 2
SCAN_SEQS = 4
WINDOW_KV_HEADS = 2
CHUNK = 64
MAX_KEYS = 1152
SEG_TILE = 256
LOCAL_KEYS = 3 * TILE
VMEM_LIMIT = 56 * 1024 * 1024


def _cparams(*sem):
    return pltpu.CompilerParams(dimension_semantics=sem, vmem_limit_bytes=VMEM_LIMIT)


def _const_spec(shape):
    nd = len(shape)
    return pl.BlockSpec(shape, lambda *_: (0,) * nd, pipeline_mode=pl.Buffered(1))


def _dot(a, b):
    return jnp.dot(a.astype(BF16), b.astype(BF16), preferred_element_type=F32)


def _dot_nt(a, b):
    return lax.dot_general(a.astype(BF16), b.astype(BF16), (((1,), (1,)), ((), ())),
                           preferred_element_type=F32)


def _dot_tn(a, b):
    return lax.dot_general(a.astype(BF16), b.astype(BF16), (((0,), (0,)), ((), ())),
                           preferred_element_type=F32)


def _dot_f32(a, b):
    return jnp.dot(a, b, precision=HIGHEST, preferred_element_type=F32)


def _sigmoid(x):
    return 1.0 / (1.0 + jnp.exp(-x))


def _rms(x, g):
    return x * lax.rsqrt(jnp.mean(x * x, axis=-1, keepdims=True) + NORM_EPS) * g


def _norm_mod(x, g, scale, shift):
    return _rms(x, g) * (1.0 + scale) + shift


def _seg_sum(x, ones_ref, pieces):
    ones = ones_ref[...]
    parts, rest = [], x
    for i in range(pieces):
        part = rest.astype(BF16)
        parts.append(part)
        if i + 1 < pieces:
            rest = rest - part.astype(F32)
    cols = []
    for j in range(x.shape[1] // SEG_TILE):
        sl = slice(j * SEG_TILE, (j + 1) * SEG_TILE)
        cols.append(sum(jnp.dot(part[:, sl], ones, preferred_element_type=F32) for part in parts))
    return jnp.concatenate(cols, axis=1)


def _mod_body(c_ref, w_ref, b_ref, o_ref):
    c = c_ref[...]
    o_ref[0] = _dot_f32(c * _sigmoid(c), w_ref[0]) + b_ref[0]


def _modulation(c_rows, mod_w, mod_b):
    depth, d, nd = mod_w.shape
    rows = c_rows.shape[0]
    tn = nd // 6
    return pl.pallas_call(
        _mod_body,
        grid=(depth, nd // tn),
        in_specs=[_const_spec((rows, d)),
                  pl.BlockSpec((1, d, tn), lambda i, j: (i, 0, j)),
                  pl.BlockSpec((1, 1, tn), lambda i, j: (i, 0, j))],
        out_specs=pl.BlockSpec((1, rows, tn), lambda i, j: (i, 0, j)),
        out_shape=jax.ShapeDtypeStruct((depth, rows, nd), F32),
        compiler_params=_cparams("arbitrary", "arbitrary"),
        name="adaln_rows",
    )(c_rows, mod_w, mod_b.reshape(depth, 1, nd))


class _Stream:
    def __init__(self, batch, ctx_len, seq, d):
        assert ctx_len % TILE == 0 and seq % TILE == 0 and d % LANES == 0
        self.b, self.t, self.d = batch, ctx_len + seq, d
        self.ctx = ctx_len
        self.n_tiles = self.t // TILE
        self.n_ctx_tiles = ctx_len // TILE

    def tile_spec(self, width=None):
        return pl.BlockSpec((1, TILE, width or self.d), lambda b, t: (b, t, 0))

    def mod_spec(self):
        nct, ctx_row = self.n_ctx_tiles, self.b
        return pl.BlockSpec((1, N_MOD, self.d), lambda b, t: (jnp.where(t < nct, ctx_row, b), 0, 0))

    def out_shape(self, width=None, dtype=F32):
        return jax.ShapeDtypeStruct((self.b, self.t, width or self.d), dtype)


def _ffn_body(*refs, mod_row, g_row, with_mixer, n_ctx_tiles, split_input):
    refs = list(refs)
    if split_input:
        ctx_ref, lat_ref = refs[:2]
        x = jnp.where(pl.program_id(1) < n_ctx_tiles, ctx_ref[...], lat_ref[...])
        refs = refs[2:]
    else:
        x = refs.pop(0)[...]
    if with_mixer:
        mod_ref, g_ref, w1_ref, w3_ref, w2_ref, o_ref, wo_ref, out_ref = refs
    else:
        mod_ref, g_ref, w1_ref, w3_ref, w2_ref, out_ref = refs
    m = mod_ref[...]
    g = g_ref[...]
    if with_mixer:
        o = o_ref[...]
        mixed = jnp.dot(o.reshape(-1, o.shape[-1]), wo_ref[...], preferred_element_type=F32).reshape(x.shape)
        x = x + m[:, 5:6] * _rms(mixed, g[3:4])
    shift, scale, gate = m[:, mod_row:mod_row + 1], m[:, mod_row + 1:mod_row + 2], m[:, mod_row + 2:mod_row + 3]
    n = _norm_mod(x, g[g_row:g_row + 1], scale, shift).astype(BF16).reshape(-1, x.shape[-1])
    a = jnp.dot(n, w1_ref[0, 0], preferred_element_type=F32)
    b = jnp.dot(n, w3_ref[0, 0], preferred_element_type=F32)
    h = (a * _sigmoid(a) * b).astype(BF16)
    y = jnp.dot(h, w2_ref[0, 0], preferred_element_type=F32).reshape(x.shape)
    out_ref[...] = x + 0.5 * gate * _rms(y, g[g_row + 1:g_row + 2])


def _ffn_half(st, x, mod, g, weights, layer, half, latent_only=False, mixer=None):
    w1, w3, w2 = weights
    d, f = w1.shape[-2:]
    nb = FFN_SEQS if st.b % FFN_SEQS == 0 else 1
    nct, ctx_blk = st.n_ctx_tiles, st.b // nb
    skip = nct if latent_only else 0
    split_input = isinstance(x, tuple)

    def rows_spec(width):
        return pl.BlockSpec((nb, TILE, width), lambda b, t: (b, t + skip, 0))

    def weight_spec(w):
        return pl.BlockSpec((1, 1) + w.shape[-2:], lambda b, t: (layer, half, 0, 0), pipeline_mode=pl.Buffered(1))

    if split_input:
        specs = [pl.BlockSpec((nb, TILE, d), lambda b, t: (b, jnp.minimum(t, nct - 1), 0)),
                 pl.BlockSpec((nb, TILE, d), lambda b, t: (b, jnp.maximum(t - nct, 0), 0))]
        args = list(x)
    else:
        specs, args = [rows_spec(d)], [x]
    mod_spec = pl.BlockSpec((nb, N_MOD, d), lambda b, t: (jnp.where(t + skip < nct, ctx_blk, b), 0, 0))
    specs += [mod_spec, _const_spec(g.shape), weight_spec(w1), weight_spec(w3), weight_spec(w2)]
    args += [mod, g, w1, w3, w2]
    if mixer is not None:
        o, wo = mixer
        specs += [rows_spec(o.shape[-1]), _const_spec(wo.shape)]
        args += [o, wo]
    return pl.pallas_call(
        functools.partial(_ffn_body, mod_row=6 * half, g_row=4 * half, with_mixer=mixer is not None,
                          n_ctx_tiles=nct, split_input=split_input),
        grid=(st.b // nb, st.n_tiles - skip),
        in_specs=specs,
        out_specs=pl.BlockSpec((nb, TILE, d), lambda b, t: (b, t, 0)),
        out_shape=jax.ShapeDtypeStruct((st.b, st.t - skip * TILE, d), F32),
        compiler_params=_cparams("parallel", "parallel"),
        name="swiglu_half",
    )(*args)


def _rwkv_feat_body(*refs, n_ctx_tiles, n_tiles, with_vres):
    (x_ref, xp_ref, xn_ref, mod_ref, g_ref, mu_ref, wr_ref, wk_ref, wv_ref, kk_ref, ka_ref,
     w0_ref, w1_ref, w2_ref, a0_ref, a1_ref, a2_ref, g1_ref, g2_ref, ones_ref) = refs[:20]
    rest = refs[20:]
    if with_vres:
        vf_ref, v0_ref, v1_ref, v2_ref = rest[:4]
        rest = rest[4:]
    r_out, v_out, kk_out, lw_out, kd_out, ab_out, g_out = rest

    t = pl.program_id(1)
    m = mod_ref[0]
    gn = g_ref[...][2:3]
    shift, scale = m[3:4], m[4:5]
    n = _norm_mod(x_ref[0], gn, scale, shift)
    first = jnp.logical_or(t == 0, t == n_ctx_tiles)
    last = jnp.logical_or(t == n_ctx_tiles - 1, t == n_tiles - 1)
    n_before = jnp.where(first, 0.0, _norm_mod(xp_ref[0][7:8], gn, scale, shift))
    n_after = jnp.where(last, 0.0, _norm_mod(xn_ref[0][0:1], gn, scale, shift))
    row = lax.broadcasted_iota(jnp.int32, (TILE, 1), 0)
    n_prev = jnp.where(row == 0, n_before, pltpu.roll(n, 1, 0))
    n_next = jnp.where(row == TILE - 1, n_after, pltpu.roll(n, TILE - 1, 0))
    xx = 0.5 * (n_prev + n_next) - n
    mu = mu_ref[...]
    xr, xw, xk, xv, xa, xg = (n + xx * mu[i:i + 1] for i in range(6))

    r = _dot(xr, wr_ref[...])
    k = _dot(xk, wk_ref[...])
    v = _dot(xv, wv_ref[...])
    if with_vres:
        lora = _dot(_dot(xv, v1_ref[...]), v2_ref[...])
        v = v + (vf_ref[0] - v) * _sigmoid(v0_ref[...] + lora)
    kk = k * kk_ref[...]
    kk = kk / jnp.maximum(jnp.sqrt(_seg_sum(kk * kk, ones_ref, 2)), L2_EPS)
    r_out[0] = r
    v_out[0] = v
    kk_out[0] = kk
    k_a = ka_ref[...]
    for dr in range(2):
        wl = w0_ref[dr] + _dot(jnp.tanh(_dot(xw, w1_ref[dr])), w2_ref[dr])
        a = _sigmoid(a0_ref[dr] + _dot(_dot(xa, a1_ref[dr]), a2_ref[dr]))
        lw_out[dr, 0] = -EXP_NEG_HALF * _sigmoid(wl)
        kd_out[dr, 0] = k * (1.0 + (a - 1.0) * k_a)
        ab_out[dr, 0] = kk * a
        g_out[dr, 0] = _dot(_sigmoid(_dot(xg, g1_ref[dr])), g2_ref[dr])


def _rwkv_features(st, x, mod, g, p, seg, vres):
    d = st.d
    n8 = st.t // 8
    per8 = TILE // 8
    prev_spec = pl.BlockSpec((1, 8, d), lambda b, t: (b, jnp.maximum(t * per8 - 1, 0), 0))
    next_spec = pl.BlockSpec((1, 8, d), lambda b, t: (b, jnp.minimum((t + 1) * per8, n8 - 1), 0))
    dir_spec = pl.BlockSpec((2, 1, TILE, d), lambda b, t: (0, b, t, 0))
    dir_shape = jax.ShapeDtypeStruct((2, st.b, st.t, d), F32)
    names = ("mu", "wr", "wk", "wv", "k_k", "k_a", "w0", "w1", "w2", "a0", "a1", "a2", "g1", "g2")
    consts = [p[nm] for nm in names] + [seg]
    args = [x, x, x, mod, g] + consts
    specs = [st.tile_spec(), prev_spec, next_spec, st.mod_spec(), _const_spec(g.shape)]
    specs += [_const_spec(c.shape) for c in consts]
    if vres is not None:
        v_first, v0, v1, v2 = vres
        args += [v_first, v0, v1, v2]
        specs += [st.tile_spec(), _const_spec(v0.shape), _const_spec(v1.shape), _const_spec(v2.shape)]
    return pl.pallas_call(
        functools.partial(_rwkv_feat_body, n_ctx_tiles=st.n_ctx_tiles, n_tiles=st.n_tiles,
                          with_vres=vres is not None),
        grid=(st.b, st.n_tiles),
        in_specs=specs,
        out_specs=[st.tile_spec()] * 3 + [dir_spec] * 4,
        out_shape=[st.out_shape()] * 3 + [dir_shape] * 4,
        compiler_params=_cparams("parallel", "parallel"),
        name="rwkv_features",
    )(*args)


def _unit_triangular_inverses(mats, ri, ci, size, upper):
    n = mats[0].shape[0]
    eye = (ri == ci).astype(F32)
    ts = [jnp.where((ri >> 1) == (ci >> 1), a, 0.0) + eye for a in mats]
    level = 1
    while (2 << level) <= size:
        m = 1 << level
        off = jnp.logical_and((ri >> (level + 1)) == (ci >> (level + 1)), (ri >> level) != (ci >> level))
        a_off = [jnp.where(off, a, 0.0) for a in mats]
        if m % SUBLANES == 0:
            moving = range(0 if upper else 1, n // m, 2)

            def rows(x):
                return jnp.concatenate([x[b * m:(b + 1) * m] for b in moving], axis=0)

            zs = [_dot(_dot(rows(t), a), t) for t, a in zip(ts, a_off)]
            ts = [jnp.concatenate([t[b * m:(b + 1) * m] + z[(b // 2) * m:(b // 2 + 1) * m] if b in moving
                                   else t[b * m:(b + 1) * m] for b in range(n // m)], axis=0)
                  for t, z in zip(ts, zs)]
        else:
            ts = [t + _dot(_dot(t, a), t) for t, a in zip(ts, a_off)]
        level += 1
    return ts


def _scan_body(r_ref, lw_ref, k_ref, v_ref, kk_ref, ab_ref, y_ref, s_ref, *, reverse):
    c = CHUNK
    n2 = 2 * c

    @pl.when(pl.program_id(1) == 0)
    def _():
        s_ref[...] = jnp.zeros_like(s_ref)

    row = lax.broadcasted_iota(jnp.int32, (c, 1), 0)

    def running_sum(x):
        step = 1
        while step < c:
            if reverse:
                x = x + jnp.where(row < c - step, pltpu.roll(x, c - step, 0), 0.0)
            else:
                x = x + jnp.where(row >= step, pltpu.roll(x, step, 0), 0.0)
            step *= 2
        return x

    ri = lax.broadcasted_iota(jnp.int32, (n2, n2), 0)
    ci = lax.broadcasted_iota(jnp.int32, (n2, n2), 1)
    same = (ri >> 6) == (ci >> 6)
    tt, ss = ri & (c - 1), ci & (c - 1)
    before = (ss > tt) if reverse else (ss < tt)
    strict = jnp.logical_and(same, before)
    incl = jnp.logical_and(same, jnp.logical_or(before, ss == tt))
    incl2 = jnp.concatenate([incl, incl], axis=1)
    even = lax.broadcasted_iota(jnp.int32, (1, LANES), 1) < RWKV_HEAD

    def stack(x):
        return jnp.concatenate([jnp.where(even, x, 0.0), jnp.where(even, 0.0, x)], axis=0)

    n_pairs = r_ref.shape[-1] // LANES
    chains = [(q, p) for q in range(r_ref.shape[0]) for p in range(n_pairs)]
    cols = {}
    for q in range(r_ref.shape[0]):
        r, lw, k, v, kk, ab = r_ref[q], lw_ref[0, q], k_ref[0, q], v_ref[q], kk_ref[q], ab_ref[0, q]
        cum = running_sum(lw)
        tot = cum[0:1] if reverse else cum[c - 1:c]
        inv = jnp.exp(-cum)
        to_end = jnp.exp(tot - cum)
        full = {"a": -kk * jnp.exp(cum - lw), "r": r * jnp.exp(cum), "b": ab * inv, "k": k * inv,
                "be": ab * to_end, "ke": k * to_end, "w": jnp.exp(tot), "v": v}
        for p in range(n_pairs):
            cols[q, p] = {nm: x[:, p * LANES:(p + 1) * LANES] for nm, x in full.items()}

    prods = [_dot_nt(jnp.concatenate([stack(cols[ch]["a"]), stack(cols[ch]["r"])], axis=0),
                     jnp.concatenate([stack(cols[ch]["b"]), stack(cols[ch]["k"])], axis=0)) for ch in chains]
    a_b = [jnp.where(strict, pr[0:n2, 0:n2], 0.0) for pr in prods]
    t_inv = _unit_triangular_inverses(a_b, ri, ci, c, reverse)
    v_st = [stack(cols[ch]["v"]) for ch in chains]
    u_own = [_dot(jnp.where(strict, pr[0:n2, n2:2 * n2], 0.0), vs) for pr, vs in zip(prods, v_st)]
    from_state = [_dot_nt(jnp.concatenate([cols[ch]["a"], cols[ch]["r"]], axis=0), s_ref[ch])
                  for ch in chains]
    u_st = [_dot(t, stack(fs[0:c]) + uo) for t, fs, uo in zip(t_inv, from_state, u_own)]
    y_st = [_dot(jnp.where(incl2, pr[n2:2 * n2], 0.0), jnp.concatenate([us, vs], axis=0))
            for pr, us, vs in zip(prods, u_st, v_st)]
    for i, (q, p) in enumerate(chains):
        y_ref[q, :, p * LANES:(p + 1) * LANES] = from_state[i][c:2 * c] + y_st[i][0:c] + y_st[i][c:n2]
        u = u_st[i][0:c] + u_st[i][c:n2]
        upd = _dot_tn(jnp.concatenate([u, cols[q, p]["v"]], axis=0),
                      jnp.concatenate([cols[q, p]["be"], cols[q, p]["ke"]], axis=0))
        s_ref[q, p] = s_ref[q, p] * cols[q, p]["w"] + jnp.where(same, upd, 0.0)


def _rwkv_scan(st, r, lw, kd, v, kk, ab, direction):
    d = st.d
    nq = SCAN_SEQS if st.b % SCAN_SEQS == 0 else 1
    nc, ncc = st.t // CHUNK, st.ctx // CHUNK
    if direction == 0:
        def cidx(s):
            return s
    else:
        def cidx(s):
            return jnp.where(s < ncc, ncc - 1 - s, nc - 1 - s + ncc)
    shared = pl.BlockSpec((nq, CHUNK, d), lambda b, s: (b, cidx(s), 0))
    per_dir = pl.BlockSpec((1, nq, CHUNK, d), lambda b, s: (direction, b, cidx(s), 0))
    return pl.pallas_call(
        functools.partial(_scan_body, reverse=direction == 1),
        grid=(st.b // nq, nc),
        in_specs=[shared, per_dir, per_dir, shared, shared, per_dir],
        out_specs=shared,
        out_shape=st.out_shape(),
        scratch_shapes=[pltpu.VMEM((nq, d // LANES, LANES, LANES), F32)],
        compiler_params=_cparams("parallel", "arbitrary"),
        name="rwkv_scan_bwd" if direction else "rwkv_scan_fwd",
    )(r, lw, kd, v, kk, ab)


def _rwkv_out_body(x_ref, mod_ref, g_ref, yf_ref, yb_ref, r_ref, kd_ref, v_ref, gate_ref,
                   lnw_ref, lnb_ref, rk_ref, wo_ref, ones_ref, o_ref):
    r, v = r_ref[0], v_ref[0]
    inv_n = 1.0 / RWKV_HEAD
    o = None
    for dr, y_ref in enumerate((yf_ref, yb_ref)):
        y = y_ref[0]
        cen = y - _seg_sum(y, ones_ref, 3) * inv_n
        var = _seg_sum(cen * cen, ones_ref, 1) * inv_n
        yn = cen * lax.rsqrt(var + GN_EPS) * lnw_ref[dr] + lnb_ref[dr]
        bonus = _seg_sum(r * kd_ref[dr, 0] * rk_ref[...], ones_ref, 1) * v
        od = (yn + bonus) * gate_ref[dr, 0]
        o = od if o is None else o + od
    y_mix = _dot(o, wo_ref[...])
    m = mod_ref[0]
    o_ref[0] = x_ref[0] + m[5:6] * _rms(y_mix, g_ref[...][3:4])


def _rwkv_out(st, x, mod, g, y_f, y_b, r, kd, v, gate, p, seg):
    d = st.d
    dir_spec = pl.BlockSpec((2, 1, TILE, d), lambda b, t: (0, b, t, 0))
    consts = [p["ln_w"], p["ln_b"], p["r_k"], p["wo"]] + [seg]
    return pl.pallas_call(
        _rwkv_out_body,
        grid=(st.b, st.n_tiles),
        in_specs=[st.tile_spec(), st.mod_spec(), _const_spec(g.shape), st.tile_spec(), st.tile_spec(),
                  st.tile_spec(), dir_spec, st.tile_spec(), dir_spec] + [_const_spec(c.shape) for c in consts],
        out_specs=st.tile_spec(),
        out_shape=st.out_shape(),
        compiler_params=_cparams("parallel", "parallel"),
        name="rwkv_readout",
    )(x, mod, g, y_f, y_b, r, kd, v, gate, *consts)


def _qkv_body(x_ref, mod_ref, g_ref, wq_ref, wk_ref, wv_ref, gq_ref, gk_ref, ones_ref,
              cos_ref, sin_ref, q_out, k_out, v_out, *, qk_norm, q_scale):
    m = mod_ref[0]
    n = _norm_mod(x_ref[0], g_ref[...][2:3], m[4:5], m[3:4]).astype(BF16)
    cos, sin = cos_ref[...], sin_ref[...]
    lane = lax.broadcasted_iota(jnp.int32, (1, LANES), 1)
    second_half = (lane & N_FREQ) != 0
    tail = lane >= HEAD_DIM

    def rotated(w_ref, gain_ref, out_ref, scale):
        t_all = jnp.dot(n, w_ref[...], preferred_element_type=F32)
        if qk_norm:
            mean_sq = _seg_sum(t_all * t_all, ones_ref, 2) * (1.0 / HEAD_DIM)
            t_all = t_all * lax.rsqrt(mean_sq + NORM_EPS) * gain_ref[...]
        for j in range(t_all.shape[1] // LANES):
            t = t_all[:, j * LANES:(j + 1) * LANES]
            partner = jnp.where(second_half, pltpu.roll(t, N_FREQ, 1), pltpu.roll(t, LANES - N_FREQ, 1))
            out = t * cos + partner * sin
            out = (out * scale if scale != 1.0 else out).astype(BF16)
            out_ref[0, 2 * j] = out[:, 0:HEAD_DIM]
            out_ref[0, 2 * j + 1] = out[:, HEAD_DIM:LANES]

    rotated(wq_ref, gq_ref, q_out, q_scale)
    rotated(wk_ref, gk_ref, k_out, 1.0)
    v_all = jnp.dot(n, wv_ref[...], preferred_element_type=F32)
    for j in range(v_all.shape[1] // LANES):
        v2 = v_all[:, j * LANES:(j + 1) * LANES]
        v_out[0, 2 * j] = jnp.where(tail, 1.0, v2).astype(BF16)
        v_out[0, 2 * j + 1] = jnp.where(tail, 1.0, pltpu.roll(v2, HEAD_DIM, 1)).astype(BF16)


def _qkv_project(st, x, mod, g, a, rope, seg, qk_norm):
    nh, nkv = a["wq"].shape[1] // HEAD_DIM, a["wk"].shape[1] // HEAD_DIM
    consts = [a["wq"], a["wk"], a["wv"], a["gq"], a["gk"], seg]
    rope_spec = pl.BlockSpec((TILE, LANES), lambda b, t: (t, 0))

    def head_spec(n, width):
        return pl.BlockSpec((1, n, TILE, width), lambda b, t: (b, 0, t, 0))

    def head_shape(n, width):
        return jax.ShapeDtypeStruct((st.b, n, st.t, width), BF16)

    return pl.pallas_call(
        functools.partial(_qkv_body, qk_norm=qk_norm, q_scale=HEAD_DIM ** -0.5 * LOG2_E),
        grid=(st.b, st.n_tiles),
        in_specs=[st.tile_spec(), st.mod_spec(), _const_spec(g.shape)]
                 + [_const_spec(c.shape) for c in consts] + [rope_spec, rope_spec],
        out_specs=[head_spec(nh, HEAD_DIM), head_spec(nkv, HEAD_DIM), head_spec(nkv, LANES)],
        out_shape=[head_shape(nh, HEAD_DIM), head_shape(nkv, HEAD_DIM), head_shape(nkv, LANES)],
        compiler_params=_cparams("parallel", "parallel"),
        name="qkv_rope",
    )(x, mod, g, *consts, *rope)


def _key_chunks(k0, k1):
    n = -(-(k1 - k0) // MAX_KEYS)
    size = -(-(k1 - k0) // (n * LANES)) * LANES
    return [(a, min(size, k1 - a)) for a in range(k0, k1, size)]


def _attn_body(*refs, ctx_len, n_ctx_tiles, total, windowed):
    if windowed:
        sink_ref, q_ref, k_ref, v_ref, o_ref = refs
    else:
        q_ref, k_ref, v_ref, o_ref = refs
    heads = q_ref.shape[1]
    group = heads // k_ref.shape[1]
    t = pl.program_id(2)
    tail = lax.broadcasted_iota(jnp.int32, (1, LANES), 1) >= HEAD_DIM
    chains = range(heads)
    qs = [q_ref[0, c] for c in chains]

    def scores(k0, size):
        return [_dot_nt(q, k_ref[0, c // group, pl.ds(k0, size), :]) for c, q in zip(chains, qs)]

    def weigh(ps, k0, size):
        return [jnp.dot(p.astype(BF16), v_ref[0, c // group, pl.ds(k0, size), :], preferred_element_type=F32)
                for c, p in zip(chains, ps)]

    def finish(accs):
        invs = [1.0 / jnp.where(tail, acc, 1.0) for acc in accs]
        cols = []
        for c in range(0, heads, 2):
            first = accs[c] * pltpu.roll(invs[c], HEAD_DIM, 1)
            second = pltpu.roll(accs[c + 1], HEAD_DIM, 1) * invs[c + 1]
            cols.append(jnp.where(tail, second, first))
        return jnp.concatenate(cols, axis=1)

    def row_max(xs):
        return [jnp.max(x, axis=-1, keepdims=True) for x in xs]

    if windowed:
        sinks = [jnp.full((TILE, 1), sink_ref[pl.program_id(1) * heads + c], F32) * LOG2_E for c in chains]

        def with_sink(s_parts, spans):
            ms = sinks
            for part in s_parts:
                ms = [jnp.maximum(m, pm) for m, pm in zip(ms, row_max(part))]
            accs = [jnp.where(tail, jnp.exp2(sk - m), 0.0) for sk, m in zip(sinks, ms)]
            for part, (k0, size) in zip(s_parts, spans):
                pvs = weigh([jnp.exp2(s - m) for s, m in zip(part, ms)], k0, size)
                accs = [acc + pv for acc, pv in zip(accs, pvs)]
            return finish(accs)

        def context_queries():
            return with_sink([scores(0, ctx_len)], [(0, ctx_len)])

        def latent_queries():
            k0 = pl.multiple_of(jnp.clip((t - 1) * TILE, 0, total - LOCAL_KEYS), TILE)
            k_pos = k0 + lax.broadcasted_iota(jnp.int32, (TILE, LOCAL_KEYS), 1)
            q_pos = t * TILE + lax.broadcasted_iota(jnp.int32, (TILE, LOCAL_KEYS), 0)
            mask = jnp.logical_and(jnp.abs(k_pos - q_pos) <= WINDOW, k_pos >= ctx_len)
            s_loc = [jnp.where(mask, s, NEG_INF) for s in scores(k0, LOCAL_KEYS)]
            return with_sink([scores(0, ctx_len), s_loc], [(0, ctx_len), (k0, LOCAL_KEYS)])
    else:
        def online(spans):
            ms = accs = None
            for k0, size in spans:
                ss = scores(k0, size)
                m_chunk = row_max(ss)
                if ms is None:
                    ms, accs = m_chunk, weigh([jnp.exp2(s - m) for s, m in zip(ss, m_chunk)], k0, size)
                else:
                    m_new = [jnp.maximum(m, mc) for m, mc in zip(ms, m_chunk)]
                    pvs = weigh([jnp.exp2(s - m) for s, m in zip(ss, m_new)], k0, size)
                    accs = [jnp.exp2(m - mn) * acc + pv for m, mn, acc, pv in zip(ms, m_new, accs, pvs)]
                    ms = m_new
            return finish(accs)

        def context_queries():
            return online([(0, ctx_len)])

        def latent_queries():
            return online(_key_chunks(0, total))

    o_ref[0] = lax.cond(t < n_ctx_tiles, context_queries, latent_queries).astype(BF16)


def _attention(st, q, k, v, sink):
    nh, nkv = q.shape[1], k.shape[1]
    group = nh // nkv
    windowed = sink is not None
    assert st.t >= LOCAL_KEYS
    kvs = WINDOW_KV_HEADS if windowed and nkv % WINDOW_KV_HEADS == 0 else 1
    q_spec = pl.BlockSpec((1, kvs * group, TILE, HEAD_DIM), lambda b, h, t: (b, h, t, 0))
    k_spec = pl.BlockSpec((1, kvs, st.t, HEAD_DIM), lambda b, h, t: (b, h, 0, 0))
    v_spec = pl.BlockSpec((1, kvs, st.t, LANES), lambda b, h, t: (b, h, 0, 0))
    specs, args = [q_spec, k_spec, v_spec], [q, k, v]
    if windowed:
        specs = [pl.BlockSpec(memory_space=pltpu.SMEM)] + specs
        args = [sink] + args
    return pl.pallas_call(
        functools.partial(_attn_body, ctx_len=st.ctx, n_ctx_tiles=st.n_ctx_tiles, total=st.t,
                          windowed=windowed),
        grid=(st.b, nkv // kvs, st.n_tiles),
        in_specs=specs,
        out_specs=pl.BlockSpec((1, TILE, kvs * group * HEAD_DIM), lambda b, h, t: (b, t, h)),
        out_shape=jax.ShapeDtypeStruct((st.b, st.t, nh * HEAD_DIM), BF16),
        compiler_params=_cparams("parallel", "parallel", "parallel"),
        name="window_attention" if windowed else "global_attention",
    )(*args)


def _pad_to(w, axis, mult):
    pad = (-w.shape[axis]) % mult
    if pad == 0:
        return w
    widths = [(0, 0)] * w.ndim
    widths[axis] = (0, pad)
    return jnp.pad(w, widths)


def _rope_tables(st):
    rows = (st.t - st.ctx) // GRID_W
    row = jnp.broadcast_to(jnp.arange(rows)[:, None], (rows, GRID_W)).reshape(-1)
    col = jnp.broadcast_to(jnp.arange(GRID_W)[None, :], (rows, GRID_W)).reshape(-1)
    inv_freq = ROPE_THETA ** (-jnp.arange(N_FREQ, dtype=F32) / N_FREQ)
    ang = jnp.stack([row, col], axis=-1).astype(F32)[:, :, None] * inv_freq
    cos, sin = jnp.cos(ang), jnp.sin(ang)
    cos_t = jnp.stack([cos, cos], axis=2).reshape(-1, HEAD_DIM)
    sin_t = jnp.stack([-sin, sin], axis=2).reshape(-1, HEAD_DIM)
    cos_t = jnp.concatenate([jnp.ones((st.ctx, HEAD_DIM), F32), cos_t], axis=0)
    sin_t = jnp.concatenate([jnp.zeros((st.ctx, HEAD_DIM), F32), sin_t], axis=0)
    return jnp.tile(cos_t, (1, LANES // HEAD_DIM)), jnp.tile(sin_t, (1, LANES // HEAD_DIM))


def _attn_params(wq, wk, wv, wo, gq, gk):
    nh, nkv = wq.shape[1] // HEAD_DIM, wk.shape[1] // HEAD_DIM
    return {"wq": wq.astype(BF16), "wk": wk.astype(BF16), "wv": wv.astype(BF16), "wo": wo.astype(BF16),
            "gq": jnp.tile(gq, nh)[None], "gk": jnp.tile(gk, nkv)[None]}


def _segment_ones():
    head = np.arange(SEG_TILE) // RWKV_HEAD
    return jnp.asarray(head[:, None] == head[None, :], dtype=BF16)


def kernel(x, c, ctx, c_ctx, mod_w, mod_b, norm_g, ffn_w1, ffn_w3, ffn_w2, rwkv_mu, rwkv_wr, rwkv_wk, rwkv_wv, rwkv_wo, rwkv_k_k, rwkv_k_a, rwkv_r_k, rwkv_w0, rwkv_w1, rwkv_w2, rwkv_a0, rwkv_a1, rwkv_a2, rwkv_g1, rwkv_g2, rwkv_ln_w, rwkv_ln_b, rwkv_v0, rwkv_v1, rwkv_v2, gattn_wq, gattn_wk, gattn_wv, gattn_wo, gattn_q_norm, gattn_k_norm, wattn_wq, wattn_wk, wattn_wv, wattn_wo, wattn_sink):
    batch, seq, d = x.shape
    depth = mod_w.shape[0]
    st = _Stream(batch, ctx.shape[1], seq, d)

    rows = -(-(batch + FFN_SEQS) // 8) * 8
    c_rows = jnp.zeros((rows, d), F32).at[:batch].set(c).at[batch:batch + FFN_SEQS].set(c_ctx)
    mods = _modulation(c_rows, mod_w, mod_b).reshape(depth, rows, N_MOD, d)

    ffn_w = (ffn_w1.astype(BF16), ffn_w3.astype(BF16), ffn_w2.astype(BF16))
    rope = _rope_tables(st)
    seg = _segment_ones()
    bf = lambda w: w.astype(BF16)

    h = (ctx, x)
    v_first = None
    for i in range(depth):
        kind, j = i % 3, i // 3
        g, mod = norm_g[i], mods[i]
        h = _ffn_half(st, h, mod, g, ffn_w, i, 0)
        mixer = None
        if kind == 0:
            p = {"mu": rwkv_mu[j], "wr": bf(rwkv_wr[j]), "wk": bf(rwkv_wk[j]), "wv": bf(rwkv_wv[j]),
                 "wo": bf(rwkv_wo[j]), "k_k": rwkv_k_k[j][None], "k_a": rwkv_k_a[j][None],
                 "r_k": rwkv_r_k[j].reshape(1, d),
                 "w0": rwkv_w0[j][:, None], "w1": bf(_pad_to(rwkv_w1[j], 2, LANES)), "w2": bf(_pad_to(rwkv_w2[j], 1, LANES)),
                 "a0": rwkv_a0[j][:, None], "a1": bf(_pad_to(rwkv_a1[j], 2, LANES)), "a2": bf(_pad_to(rwkv_a2[j], 1, LANES)),
                 "g1": bf(_pad_to(rwkv_g1[j], 2, LANES)), "g2": bf(_pad_to(rwkv_g2[j], 1, LANES)),
                 "ln_w": rwkv_ln_w[j][:, None], "ln_b": rwkv_ln_b[j][:, None]}
            vres = None
            if j > 0:
                vres = (v_first, rwkv_v0[j - 1][None], bf(_pad_to(rwkv_v1[j - 1], 1, LANES)),
                        bf(_pad_to(rwkv_v2[j - 1], 0, LANES)))
            r, v, kk, lw, kd, ab, gate = _rwkv_features(st, h, mod, g, p, seg, vres)
            if j == 0:
                v_first = v
            y_f = _rwkv_scan(st, r, lw, kd, v, kk, ab, 0)
            y_b = _rwkv_scan(st, r, lw, kd, v, kk, ab, 1)
            h = _rwkv_out(st, h, mod, g, y_f, y_b, r, kd, v, gate, p, seg)
        else:
            if kind == 1:
                a = _attn_params(gattn_wq[j], gattn_wk[j], gattn_wv[j], gattn_wo[j],
                                 gattn_q_norm[j], gattn_k_norm[j])
                sink = None
            else:
                ones = jnp.ones((HEAD_DIM,), F32)
                a = _attn_params(wattn_wq[j], wattn_wk[j], wattn_wv[j], wattn_wo[j], ones, ones)
                sink = wattn_sink[j]
            q, k, v = _qkv_project(st, h, mod, g, a, rope, seg, qk_norm=kind == 1)
            mixer = (_attention(st, q, k, v, sink), a["wo"])
        h = _ffn_half(st, h, mod, g, ffn_w, i, 1, latent_only=i == depth - 1, mixer=mixer)
    return h
```

```python
import functools

import numpy as np
import jax
import jax.numpy as jnp
from jax import lax
from jax.experimental import pallas as pl
from jax.experimental.pallas import tpu as pltpu

F32 = jnp.float32
BF16 = jnp.bfloat16
HIGHEST = lax.Precision.HIGHEST

GRID_W = 64
N_MOD = 9
HEAD_DIM = 64
N_KV_HEADS = 4
WINDOW = 128
ROPE_THETA = 10000.0
N_FREQ = HEAD_DIM // 4
RWKV_HEAD = 64
NORM_EPS = 1e-6
GN_EPS = 64e-5
L2_EPS = 1e-12
NEG_INF = -1e30
LOG2_E = 1.4426950408889634
EXP_NEG_HALF = 0.6065306597126334

LANES = 128
SUBLANES = 8
TILE = 256
FFN_SEQS = 2
SCAN_SEQS = 4
ATTN_KV_HEADS = 2
CHUNK = 64
MAX_KEYS = 1152
SEG_TILE = 256
LOCAL_KEYS = 3 * TILE
VMEM_LIMIT = 56 * 1024 * 1024


def _cparams(*sem):
    return pltpu.CompilerParams(dimension_semantics=sem, vmem_limit_bytes=VMEM_LIMIT)


def _const_spec(shape):
    nd = len(shape)
    return pl.BlockSpec(shape, lambda *_: (0,) * nd, pipeline_mode=pl.Buffered(1))


def _dot(a, b):
    return jnp.dot(a.astype(BF16), b.astype(BF16), preferred_element_type=F32)


def _dot_nt(a, b):
    return lax.dot_general(a.astype(BF16), b.astype(BF16), (((1,), (1,)), ((), ())),
                           preferred_element_type=F32)


def _dot_tn(a, b):
    return lax.dot_general(a.astype(BF16), b.astype(BF16), (((0,), (0,)), ((), ())),
                           preferred_element_type=F32)


def _dot_f32(a, b):
    return jnp.dot(a, b, precision=HIGHEST, preferred_element_type=F32)


def _sigmoid(x):
    return 1.0 / (1.0 + jnp.exp2(x * -LOG2_E))


def _rms(x, g):
    return x * lax.rsqrt(jnp.mean(x * x, axis=-1, keepdims=True) + NORM_EPS) * g


def _norm_mod(x, g, scale, shift):
    return _rms(x, g) * (1.0 + scale) + shift


def _seg_sum(x, ones_ref, pieces):
    ones = ones_ref[...]
    parts, rest = [], x
    for i in range(pieces):
        part = rest.astype(BF16)
        parts.append(part)
        if i + 1 < pieces:
            rest = rest - part.astype(F32)
    cols = []
    for j in range(x.shape[1] // SEG_TILE):
        sl = slice(j * SEG_TILE, (j + 1) * SEG_TILE)
        cols.append(sum(jnp.dot(part[:, sl], ones, preferred_element_type=F32) for part in parts))
    return jnp.concatenate(cols, axis=1)


def _mod_body(c_ref, w_ref, b_ref, o_ref):
    c = c_ref[...]
    o_ref[0] = _dot_f32(c * _sigmoid(c), w_ref[0]) + b_ref[0]


def _modulation(c_rows, mod_w, mod_b):
    depth, d, nd = mod_w.shape
    rows = c_rows.shape[0]
    tn = nd // 6
    return pl.pallas_call(
        _mod_body,
        grid=(depth, nd // tn),
        in_specs=[_const_spec((rows, d)),
                  pl.BlockSpec((1, d, tn), lambda i, j: (i, 0, j)),
                  pl.BlockSpec((1, 1, tn), lambda i, j: (i, 0, j))],
        out_specs=pl.BlockSpec((1, rows, tn), lambda i, j: (i, 0, j)),
        out_shape=jax.ShapeDtypeStruct((depth, rows, nd), F32),
        compiler_params=_cparams("arbitrary", "arbitrary"),
        name="adaln_rows",
    )(c_rows, mod_w, mod_b.reshape(depth, 1, nd))


class _Stream:
    def __init__(self, batch, ctx_len, seq, d):
        assert ctx_len % TILE == 0 and seq % TILE == 0 and d % LANES == 0
        self.b, self.t, self.d = batch, ctx_len + seq, d
        self.ctx = ctx_len
        self.n_tiles = self.t // TILE
        self.n_ctx_tiles = ctx_len // TILE

    def tile_spec(self, width=None):
        return pl.BlockSpec((1, TILE, width or self.d), lambda b, t: (b, t, 0))

    def mod_spec(self):
        nct, ctx_row = self.n_ctx_tiles, self.b
        return pl.BlockSpec((1, N_MOD, self.d), lambda b, t: (jnp.where(t < nct, ctx_row, b), 0, 0))

    def out_shape(self, width=None, dtype=F32):
        return jax.ShapeDtypeStruct((self.b, self.t, width or self.d), dtype)


def _ffn_body(*refs, mod_row, g_row, with_mixer, n_ctx_tiles, split_input):
    refs = list(refs)
    if split_input:
        ctx_ref, lat_ref = refs[:2]
        x = jnp.where(pl.program_id(1) < n_ctx_tiles, ctx_ref[...], lat_ref[...])
        refs = refs[2:]
    else:
        x = refs.pop(0)[...]
    if with_mixer:
        mod_ref, g_ref, w1_ref, w3_ref, w2_ref, o_ref, wo_ref, out_ref = refs
    else:
        mod_ref, g_ref, w1_ref, w3_ref, w2_ref, out_ref = refs
    m = mod_ref[...]
    g = g_ref[...]
    if with_mixer:
        o = o_ref[...]
        mixed = jnp.dot(o.reshape(-1, o.shape[-1]), wo_ref[...], preferred_element_type=F32).reshape(x.shape)
        x = x + m[:, 5:6] * _rms(mixed, g[3:4])
    shift, scale, gate = m[:, mod_row:mod_row + 1], m[:, mod_row + 1:mod_row + 2], m[:, mod_row + 2:mod_row + 3]
    n = _norm_mod(x, g[g_row:g_row + 1], scale, shift).astype(BF16).reshape(-1, x.shape[-1])
    a = jnp.dot(n, w1_ref[0, 0], preferred_element_type=F32)
    b = jnp.dot(n, w3_ref[0, 0], preferred_element_type=F32)
    h = (a * _sigmoid(a) * b).astype(BF16)
    y = jnp.dot(h, w2_ref[0, 0], preferred_element_type=F32).reshape(x.shape)
    out_ref[...] = x + 0.5 * gate * _rms(y, g[g_row + 1:g_row + 2])


def _ffn_half(st, x, mod, g, weights, layer, half, latent_only=False, mixer=None):
    w1, w3, w2 = weights
    d, f = w1.shape[-2:]
    nb = FFN_SEQS if st.b % FFN_SEQS == 0 else 1
    nct, ctx_blk = st.n_ctx_tiles, st.b // nb
    skip = nct if latent_only else 0
    split_input = isinstance(x, tuple)

    def rows_spec(width):
        return pl.BlockSpec((nb, TILE, width), lambda b, t: (b, t + skip, 0))

    def weight_spec(w):
        return pl.BlockSpec((1, 1) + w.shape[-2:], lambda b, t: (layer, half, 0, 0), pipeline_mode=pl.Buffered(1))

    if split_input:
        specs = [pl.BlockSpec((nb, TILE, d), lambda b, t: (b, jnp.minimum(t, nct - 1), 0)),
                 pl.BlockSpec((nb, TILE, d), lambda b, t: (b, jnp.maximum(t - nct, 0), 0))]
        args = list(x)
    else:
        specs, args = [rows_spec(d)], [x]
    mod_spec = pl.BlockSpec((nb, N_MOD, d), lambda b, t: (jnp.where(t + skip < nct, ctx_blk, b), 0, 0))
    specs += [mod_spec, _const_spec(g.shape), weight_spec(w1), weight_spec(w3), weight_spec(w2)]
    args += [mod, g, w1, w3, w2]
    if mixer is not None:
        o, wo = mixer
        specs += [rows_spec(o.shape[-1]), _const_spec(wo.shape)]
        args += [o, wo]
    return pl.pallas_call(
        functools.partial(_ffn_body, mod_row=6 * half, g_row=4 * half, with_mixer=mixer is not None,
                          n_ctx_tiles=nct, split_input=split_input),
        grid=(st.b // nb, st.n_tiles - skip),
        in_specs=specs,
        out_specs=pl.BlockSpec((nb, TILE, d), lambda b, t: (b, t, 0)),
        out_shape=jax.ShapeDtypeStruct((st.b, st.t - skip * TILE, d), F32),
        compiler_params=_cparams("parallel", "parallel"),
        name="swiglu_half",
    )(*args)


def _rwkv_feat_body(*refs, n_ctx_tiles, n_tiles, with_vres):
    (x_ref, xp_ref, xn_ref, mod_ref, g_ref, mu_ref, wr_ref, wk_ref, wv_ref, kk_ref, ka_ref,
     w0_ref, w1_ref, w2_ref, a0_ref, a1_ref, a2_ref, g1_ref, g2_ref, ones_ref) = refs[:20]
    rest = refs[20:]
    if with_vres:
        vf_ref, v0_ref, v1_ref, v2_ref = rest[:4]
        rest = rest[4:]
    r_out, v_out, kk_out, lw_out, kd_out, ab_out, g_out = rest

    t = pl.program_id(1)
    m = mod_ref[0]
    gn = g_ref[...][2:3]
    shift, scale = m[3:4], m[4:5]
    n = _norm_mod(x_ref[0], gn, scale, shift)
    first = jnp.logical_or(t == 0, t == n_ctx_tiles)
    last = jnp.logical_or(t == n_ctx_tiles - 1, t == n_tiles - 1)
    n_before = jnp.where(first, 0.0, _norm_mod(xp_ref[0][7:8], gn, scale, shift))
    n_after = jnp.where(last, 0.0, _norm_mod(xn_ref[0][0:1], gn, scale, shift))
    row = lax.broadcasted_iota(jnp.int32, (TILE, 1), 0)
    n_prev = jnp.where(row == 0, n_before, pltpu.roll(n, 1, 0))
    n_next = jnp.where(row == TILE - 1, n_after, pltpu.roll(n, TILE - 1, 0))
    xx = 0.5 * (n_prev + n_next) - n
    mu = mu_ref[...]
    xr, xw, xk, xv, xa, xg = (n + xx * mu[i:i + 1] for i in range(6))

    r = _dot(xr, wr_ref[...])
    k = _dot(xk, wk_ref[...])
    v = _dot(xv, wv_ref[...])
    if with_vres:
        lora = _dot(_dot(xv, v1_ref[...]), v2_ref[...])
        v = v + (vf_ref[0] - v) * _sigmoid(v0_ref[...] + lora)
    kk = k * kk_ref[...]
    kk = kk / jnp.maximum(jnp.sqrt(_seg_sum(kk * kk, ones_ref, 2)), L2_EPS)
    r_out[0] = r.astype(BF16)
    v_out[0] = v
    kk_out[0] = kk
    k_ka = k * ka_ref[...]
    k_rest = k - k_ka
    for dr in range(2):
        wl = w0_ref[dr] + _dot(jnp.tanh(_dot(xw, w1_ref[dr])), w2_ref[dr])
        a = _sigmoid(a0_ref[dr] + _dot(_dot(xa, a1_ref[dr]), a2_ref[dr]))
        lw_out[dr, 0] = -EXP_NEG_HALF * _sigmoid(wl)
        kd_out[dr, 0] = (k_rest + k_ka * a).astype(BF16)
        ab_out[dr, 0] = kk * a
        g_out[dr, 0] = _dot(_sigmoid(_dot(xg, g1_ref[dr])), g2_ref[dr]).astype(BF16)


def _rwkv_features(st, x, mod, g, p, seg, vres):
    d = st.d
    n8 = st.t // 8
    per8 = TILE // 8
    prev_spec = pl.BlockSpec((1, 8, d), lambda b, t: (b, jnp.maximum(t * per8 - 1, 0), 0))
    next_spec = pl.BlockSpec((1, 8, d), lambda b, t: (b, jnp.minimum((t + 1) * per8, n8 - 1), 0))
    dir_spec = pl.BlockSpec((2, 1, TILE, d), lambda b, t: (0, b, t, 0))
    dir_f32 = jax.ShapeDtypeStruct((2, st.b, st.t, d), F32)
    dir_bf16 = jax.ShapeDtypeStruct((2, st.b, st.t, d), BF16)
    names = ("mu", "wr", "wk", "wv", "k_k", "k_a", "w0", "w1", "w2", "a0", "a1", "a2", "g1", "g2")
    consts = [p[nm] for nm in names] + [seg]
    args = [x, x, x, mod, g] + consts
    specs = [st.tile_spec(), prev_spec, next_spec, st.mod_spec(), _const_spec(g.shape)]
    specs += [_const_spec(c.shape) for c in consts]
    if vres is not None:
        v_first, v0, v1, v2 = vres
        args += [v_first, v0, v1, v2]
        specs += [st.tile_spec(), _const_spec(v0.shape), _const_spec(v1.shape), _const_spec(v2.shape)]
    return pl.pallas_call(
        functools.partial(_rwkv_feat_body, n_ctx_tiles=st.n_ctx_tiles, n_tiles=st.n_tiles,
                          with_vres=vres is not None),
        grid=(st.b, st.n_tiles),
        in_specs=specs,
        out_specs=[st.tile_spec()] * 3 + [dir_spec] * 4,
        out_shape=[st.out_shape(dtype=BF16), st.out_shape(), st.out_shape(), dir_f32, dir_bf16, dir_f32, dir_bf16],
        compiler_params=_cparams("parallel", "parallel"),
        name="rwkv_features",
    )(*args)


def _unit_triangular_inverses(mats, ri, ci, size, upper):
    n = mats[0].shape[0]
    eye = (ri == ci).astype(F32)
    ts = [jnp.where((ri >> 1) == (ci >> 1), a, 0.0) + eye for a in mats]
    level = 1
    while (2 << level) <= size:
        m = 1 << level
        off = jnp.logical_and((ri >> (level + 1)) == (ci >> (level + 1)), (ri >> level) != (ci >> level))
        a_off = [jnp.where(off, a, 0.0) for a in mats]
        if m % SUBLANES == 0:
            moving = range(0 if upper else 1, n // m, 2)

            def rows(x):
                return jnp.concatenate([x[b * m:(b + 1) * m] for b in moving], axis=0)

            zs = [_dot(_dot(rows(t), a), t) for t, a in zip(ts, a_off)]
            ts = [jnp.concatenate([t[b * m:(b + 1) * m] + z[(b // 2) * m:(b // 2 + 1) * m] if b in moving
                                   else t[b * m:(b + 1) * m] for b in range(n // m)], axis=0)
                  for t, z in zip(ts, zs)]
        else:
            ts = [t + _dot(_dot(t, a), t) for t, a in zip(ts, a_off)]
        level += 1
    return ts


def _scan_body(r_ref, lw_ref, k_ref, v_ref, kk_ref, ab_ref, y_ref, s_ref, *, reverse):
    c = CHUNK
    n2 = 2 * c

    @pl.when(pl.program_id(1) == 0)
    def _():
        s_ref[...] = jnp.zeros_like(s_ref)

    row = lax.broadcasted_iota(jnp.int32, (c, 1), 0)

    def running_sum(x):
        step = 1
        while step < c:
            if reverse:
                x = x + jnp.where(row < c - step, pltpu.roll(x, c - step, 0), 0.0)
            else:
                x = x + jnp.where(row >= step, pltpu.roll(x, step, 0), 0.0)
            step *= 2
        return x

    ri = lax.broadcasted_iota(jnp.int32, (n2, n2), 0)
    ci = lax.broadcasted_iota(jnp.int32, (n2, n2), 1)
    same = (ri >> 6) == (ci >> 6)
    tt, ss = ri & (c - 1), ci & (c - 1)
    before = (ss > tt) if reverse else (ss < tt)
    strict = jnp.logical_and(same, before)
    incl = jnp.logical_and(same, jnp.logical_or(before, ss == tt))
    incl2 = jnp.concatenate([incl, incl], axis=1)
    even = lax.broadcasted_iota(jnp.int32, (1, LANES), 1) < RWKV_HEAD

    def stack(x):
        return jnp.concatenate([jnp.where(even, x, 0.0), jnp.where(even, 0.0, x)], axis=0)

    n_pairs = r_ref.shape[-1] // LANES
    chains = [(q, p) for q in range(r_ref.shape[0]) for p in range(n_pairs)]
    cols = {}
    for q in range(r_ref.shape[0]):
        r, k = r_ref[q].astype(F32), k_ref[0, q].astype(F32)
        lw, v, kk, ab = lw_ref[0, q], v_ref[q], kk_ref[q], ab_ref[0, q]
        cum = running_sum(lw)
        tot = cum[0:1] if reverse else cum[c - 1:c]
        inv = jnp.exp(-cum)
        to_end = jnp.exp(tot - cum)
        full = {"a": -kk * jnp.exp(cum - lw), "r": r * jnp.exp(cum), "b": ab * inv, "k": k * inv,
                "be": ab * to_end, "ke": k * to_end, "w": jnp.exp(tot), "v": v}
        for p in range(n_pairs):
            cols[q, p] = {nm: x[:, p * LANES:(p + 1) * LANES] for nm, x in full.items()}

    prods = [_dot_nt(jnp.concatenate([stack(cols[ch]["a"]), stack(cols[ch]["r"])], axis=0),
                     jnp.concatenate([stack(cols[ch]["b"]), stack(cols[ch]["k"])], axis=0)) for ch in chains]
    a_b = [jnp.where(strict, pr[0:n2, 0:n2], 0.0) for pr in prods]
    t_inv = _unit_triangular_inverses(a_b, ri, ci, c, reverse)
    v_st = [stack(cols[ch]["v"]) for ch in chains]
    u_own = [_dot(jnp.where(strict, pr[0:n2, n2:2 * n2], 0.0), vs) for pr, vs in zip(prods, v_st)]
    from_state = [_dot_nt(jnp.concatenate([cols[ch]["a"], cols[ch]["r"]], axis=0), s_ref[ch])
                  for ch in chains]
    u_st = [_dot(t, stack(fs[0:c]) + uo) for t, fs, uo in zip(t_inv, from_state, u_own)]
    y_st = [_dot(jnp.where(incl2, pr[n2:2 * n2], 0.0), jnp.concatenate([us, vs], axis=0))
            for pr, us, vs in zip(prods, u_st, v_st)]
    for i, (q, p) in enumerate(chains):
        y_ref[q, :, p * LANES:(p + 1) * LANES] = (from_state[i][c:2 * c] + y_st[i][0:c] + y_st[i][c:n2]).astype(BF16)
        u = u_st[i][0:c] + u_st[i][c:n2]
        upd = _dot_tn(jnp.concatenate([u, cols[q, p]["v"]], axis=0),
                      jnp.concatenate([cols[q, p]["be"], cols[q, p]["ke"]], axis=0))
        s_ref[q, p] = s_ref[q, p] * cols[q, p]["w"] + jnp.where(same, upd, 0.0)


def _rwkv_scan(st, r, lw, kd, v, kk, ab, direction):
    d = st.d
    nq = SCAN_SEQS if st.b % SCAN_SEQS == 0 else 1
    nc, ncc = st.t // CHUNK, st.ctx // CHUNK
    if direction == 0:
        def cidx(s):
            return s
    else:
        def cidx(s):
            return jnp.where(s < ncc, ncc - 1 - s, nc - 1 - s + ncc)
    shared = pl.BlockSpec((nq, CHUNK, d), lambda b, s: (b, cidx(s), 0))
    per_dir = pl.BlockSpec((1, nq, CHUNK, d), lambda b, s: (direction, b, cidx(s), 0))
    return pl.pallas_call(
        functools.partial(_scan_body, reverse=direction == 1),
        grid=(st.b // nq, nc),
        in_specs=[shared, per_dir, per_dir, shared, shared, per_dir],
        out_specs=shared,
        out_shape=st.out_shape(dtype=BF16),
        scratch_shapes=[pltpu.VMEM((nq, d // LANES, LANES, LANES), F32)],
        compiler_params=_cparams("parallel", "arbitrary"),
        name="rwkv_scan_bwd" if direction else "rwkv_scan_fwd",
    )(r, lw, kd, v, kk, ab)


def _rwkv_out_body(x_ref, mod_ref, g_ref, yf_ref, yb_ref, r_ref, kd_ref, v_ref, gate_ref,
                   lnw_ref, lnb_ref, rk_ref, wo_ref, ones_ref, o_ref):
    r, v = r_ref[0].astype(F32), v_ref[0]
    inv_n = 1.0 / RWKV_HEAD
    o = None
    for dr, y_ref in enumerate((yf_ref, yb_ref)):
        y = y_ref[0].astype(F32)
        cen = y - _seg_sum(y, ones_ref, 1) * inv_n
        var = _seg_sum(cen * cen, ones_ref, 1) * inv_n
        yn = cen * lax.rsqrt(var + GN_EPS) * lnw_ref[dr] + lnb_ref[dr]
        bonus = _seg_sum(r * kd_ref[dr, 0].astype(F32) * rk_ref[...], ones_ref, 1) * v
        od = (yn + bonus) * gate_ref[dr, 0].astype(F32)
        o = od if o is None else o + od
    y_mix = _dot(o, wo_ref[...])
    m = mod_ref[0]
    o_ref[0] = x_ref[0] + m[5:6] * _rms(y_mix, g_ref[...][3:4])


def _rwkv_out(st, x, mod, g, y_f, y_b, r, kd, v, gate, p, seg):
    d = st.d
    dir_spec = pl.BlockSpec((2, 1, TILE, d), lambda b, t: (0, b, t, 0))
    consts = [p["ln_w"], p["ln_b"], p["r_k"], p["wo"]] + [seg]
    return pl.pallas_call(
        _rwkv_out_body,
        grid=(st.b, st.n_tiles),
        in_specs=[st.tile_spec(), st.mod_spec(), _const_spec(g.shape), st.tile_spec(), st.tile_spec(),
                  st.tile_spec(), dir_spec, st.tile_spec(), dir_spec] + [_const_spec(c.shape) for c in consts],
        out_specs=st.tile_spec(),
        out_shape=st.out_shape(),
        compiler_params=_cparams("parallel", "parallel"),
        name="rwkv_readout",
    )(x, mod, g, y_f, y_b, r, kd, v, gate, *consts)


def _qkv_body(x_ref, mod_ref, g_ref, wq_ref, wk_ref, wv_ref, gq_ref, gk_ref, ones_ref,
              cos_ref, sin_ref, q_out, k_out, v_out, *, qk_norm, q_scale):
    m = mod_ref[0]
    n = _norm_mod(x_ref[0], g_ref[...][2:3], m[4:5], m[3:4]).astype(BF16)
    cos, sin = cos_ref[...], sin_ref[...]
    lane = lax.broadcasted_iota(jnp.int32, (1, LANES), 1)
    second_half = (lane & N_FREQ) != 0
    tail = lane >= HEAD_DIM

    def rotated(w_ref, gain_ref, out_ref, scale):
        t_all = jnp.dot(n, w_ref[...], preferred_element_type=F32)
        if qk_norm:
            mean_sq = _seg_sum(t_all * t_all, ones_ref, 2) * (1.0 / HEAD_DIM)
            t_all = t_all * lax.rsqrt(mean_sq + NORM_EPS) * gain_ref[...]
        for j in range(t_all.shape[1] // LANES):
            t = t_all[:, j * LANES:(j + 1) * LANES]
            partner = jnp.where(second_half, pltpu.roll(t, N_FREQ, 1), pltpu.roll(t, LANES - N_FREQ, 1))
            out = t * cos + partner * sin
            out = (out * scale if scale != 1.0 else out).astype(BF16)
            out_ref[0, 2 * j] = out[:, 0:HEAD_DIM]
            out_ref[0, 2 * j + 1] = out[:, HEAD_DIM:LANES]

    rotated(wq_ref, gq_ref, q_out, q_scale)
    rotated(wk_ref, gk_ref, k_out, 1.0)
    v_all = jnp.dot(n, wv_ref[...], preferred_element_type=F32)
    for j in range(v_all.shape[1] // LANES):
        v2 = v_all[:, j * LANES:(j + 1) * LANES]
        v_out[0, 2 * j] = jnp.where(tail, 1.0, v2).astype(BF16)
        v_out[0, 2 * j + 1] = jnp.where(tail, 1.0, pltpu.roll(v2, HEAD_DIM, 1)).astype(BF16)


def _qkv_project(st, x, mod, g, a, rope, seg, qk_norm):
    nh, nkv = a["wq"].shape[1] // HEAD_DIM, a["wk"].shape[1] // HEAD_DIM
    consts = [a["wq"], a["wk"], a["wv"], a["gq"], a["gk"], seg]
    rope_spec = pl.BlockSpec((TILE, LANES), lambda b, t: (t, 0))

    def head_spec(n, width):
        return pl.BlockSpec((1, n, TILE, width), lambda b, t: (b, 0, t, 0))

    def head_shape(n, width):
        return jax.ShapeDtypeStruct((st.b, n, st.t, width), BF16)

    return pl.pallas_call(
        functools.partial(_qkv_body, qk_norm=qk_norm, q_scale=HEAD_DIM ** -0.5 * LOG2_E),
        grid=(st.b, st.n_tiles),
        in_specs=[st.tile_spec(), st.mod_spec(), _const_spec(g.shape)]
                 + [_const_spec(c.shape) for c in consts] + [rope_spec, rope_spec],
        out_specs=[head_spec(nh, HEAD_DIM), head_spec(nkv, HEAD_DIM), head_spec(nkv, LANES)],
        out_shape=[head_shape(nh, HEAD_DIM), head_shape(nkv, HEAD_DIM), head_shape(nkv, LANES)],
        compiler_params=_cparams("parallel", "parallel"),
        name="qkv_rope",
    )(x, mod, g, *consts, *rope)


def _key_chunks(k0, k1):
    n = -(-(k1 - k0) // MAX_KEYS)
    size = -(-(k1 - k0) // (n * LANES)) * LANES
    return [(a, min(size, k1 - a)) for a in range(k0, k1, size)]


def _attn_body(*refs, ctx_len, n_ctx_tiles, total, windowed):
    if windowed:
        sink_ref, q_ref, k_ref, v_ref, o_ref = refs
    else:
        q_ref, k_ref, v_ref, o_ref = refs
    heads = q_ref.shape[1]
    group = heads // k_ref.shape[1]
    t = pl.program_id(2)
    tail = lax.broadcasted_iota(jnp.int32, (1, LANES), 1) >= HEAD_DIM
    chains = range(heads)
    qs = [q_ref[0, c] for c in chains]

    def scores(k0, size):
        return [_dot_nt(q, k_ref[0, c // group, pl.ds(k0, size), :]) for c, q in zip(chains, qs)]

    def weigh(ps, k0, size):
        return [jnp.dot(p.astype(BF16), v_ref[0, c // group, pl.ds(k0, size), :], preferred_element_type=F32)
                for c, p in zip(chains, ps)]

    def finish(accs):
        invs = [1.0 / jnp.where(tail, acc, 1.0) for acc in accs]
        cols = []
        for c in range(0, heads, 2):
            first = accs[c] * pltpu.roll(invs[c], HEAD_DIM, 1)
            second = pltpu.roll(accs[c + 1], HEAD_DIM, 1) * invs[c + 1]
            cols.append(jnp.where(tail, second, first))
        return jnp.concatenate(cols, axis=1)

    def row_max(xs):
        return [jnp.max(x, axis=-1, keepdims=True) for x in xs]

    if windowed:
        sinks = [jnp.full((TILE, 1), sink_ref[pl.program_id(1) * heads + c], F32) * LOG2_E for c in chains]

        def with_sink(s_parts, spans):
            ms = sinks
            for part in s_parts:
                ms = [jnp.maximum(m, pm) for m, pm in zip(ms, row_max(part))]
            accs = [jnp.where(tail, jnp.exp2(sk - m), 0.0) for sk, m in zip(sinks, ms)]
            for part, (k0, size) in zip(s_parts, spans):
                pvs = weigh([jnp.exp2(s - m) for s, m in zip(part, ms)], k0, size)
                accs = [acc + pv for acc, pv in zip(accs, pvs)]
            return finish(accs)

        def context_queries():
            return with_sink([scores(0, ctx_len)], [(0, ctx_len)])

        def latent_queries():
            k0 = pl.multiple_of(jnp.clip((t - 1) * TILE, 0, total - LOCAL_KEYS), TILE)
            k_pos = k0 + lax.broadcasted_iota(jnp.int32, (TILE, LOCAL_KEYS), 1)
            q_pos = t * TILE + lax.broadcasted_iota(jnp.int32, (TILE, LOCAL_KEYS), 0)
            mask = jnp.logical_and(jnp.abs(k_pos - q_pos) <= WINDOW, k_pos >= ctx_len)
            s_loc = [jnp.where(mask, s, NEG_INF) for s in scores(k0, LOCAL_KEYS)]
            return with_sink([scores(0, ctx_len), s_loc], [(0, ctx_len), (k0, LOCAL_KEYS)])
    else:
        def online(spans):
            ms = accs = None
            for k0, size in spans:
                ss = scores(k0, size)
                m_chunk = row_max(ss)
                if ms is None:
                    ms, accs = m_chunk, weigh([jnp.exp2(s - m) for s, m in zip(ss, m_chunk)], k0, size)
                else:
                    m_new = [jnp.maximum(m, mc) for m, mc in zip(ms, m_chunk)]
                    pvs = weigh([jnp.exp2(s - m) for s, m in zip(ss, m_new)], k0, size)
                    accs = [jnp.exp2(m - mn) * acc + pv for m, mn, acc, pv in zip(ms, m_new, accs, pvs)]
                    ms = m_new
            return finish(accs)

        def context_queries():
            return online([(0, ctx_len)])

        def latent_queries():
            return online(_key_chunks(0, total))

    o_ref[0] = lax.cond(t < n_ctx_tiles, context_queries, latent_queries).astype(BF16)


def _attention(st, q, k, v, sink):
    nh, nkv = q.shape[1], k.shape[1]
    group = nh // nkv
    windowed = sink is not None
    assert st.t >= LOCAL_KEYS
    kvs = ATTN_KV_HEADS if nkv % ATTN_KV_HEADS == 0 else 1
    q_spec = pl.BlockSpec((1, kvs * group, TILE, HEAD_DIM), lambda b, h, t: (b, h, t, 0))
    k_spec = pl.BlockSpec((1, kvs, st.t, HEAD_DIM), lambda b, h, t: (b, h, 0, 0))
    v_spec = pl.BlockSpec((1, kvs, st.t, LANES), lambda b, h, t: (b, h, 0, 0))
    specs, args = [q_spec, k_spec, v_spec], [q, k, v]
    if windowed:
        specs = [pl.BlockSpec(memory_space=pltpu.SMEM)] + specs
        args = [sink] + args
    return pl.pallas_call(
        functools.partial(_attn_body, ctx_len=st.ctx, n_ctx_tiles=st.n_ctx_tiles, total=st.t,
                          windowed=windowed),
        grid=(st.b, nkv // kvs, st.n_tiles),
        in_specs=specs,
        out_specs=pl.BlockSpec((1, TILE, kvs * group * HEAD_DIM), lambda b, h, t: (b, t, h)),
        out_shape=jax.ShapeDtypeStruct((st.b, st.t, nh * HEAD_DIM), BF16),
        compiler_params=_cparams("parallel", "parallel", "parallel"),
        name="window_attention" if windowed else "global_attention",
    )(*args)


def _pad_to(w, axis, mult):
    pad = (-w.shape[axis]) % mult
    if pad == 0:
        return w
    widths = [(0, 0)] * w.ndim
    widths[axis] = (0, pad)
    return jnp.pad(w, widths)


def _rope_tables(st):
    rows = (st.t - st.ctx) // GRID_W
    row = jnp.broadcast_to(jnp.arange(rows)[:, None], (rows, GRID_W)).reshape(-1)
    col = jnp.broadcast_to(jnp.arange(GRID_W)[None, :], (rows, GRID_W)).reshape(-1)
    inv_freq = ROPE_THETA ** (-jnp.arange(N_FREQ, dtype=F32) / N_FREQ)
    ang = jnp.stack([row, col], axis=-1).astype(F32)[:, :, None] * inv_freq
    cos, sin = jnp.cos(ang), jnp.sin(ang)
    cos_t = jnp.stack([cos, cos], axis=2).reshape(-1, HEAD_DIM)
    sin_t = jnp.stack([-sin, sin], axis=2).reshape(-1, HEAD_DIM)
    cos_t = jnp.concatenate([jnp.ones((st.ctx, HEAD_DIM), F32), cos_t], axis=0)
    sin_t = jnp.concatenate([jnp.zeros((st.ctx, HEAD_DIM), F32), sin_t], axis=0)
    return jnp.tile(cos_t, (1, LANES // HEAD_DIM)), jnp.tile(sin_t, (1, LANES // HEAD_DIM))


def _attn_params(wq, wk, wv, wo, gq, gk):
    nh, nkv = wq.shape[1] // HEAD_DIM, wk.shape[1] // HEAD_DIM
    return {"wq": wq.astype(BF16), "wk": wk.astype(BF16), "wv": wv.astype(BF16), "wo": wo.astype(BF16),
            "gq": jnp.tile(gq, nh)[None], "gk": jnp.tile(gk, nkv)[None]}


def _segment_ones():
    head = np.arange(SEG_TILE) // RWKV_HEAD
    return jnp.asarray(head[:, None] == head[None, :], dtype=BF16)


def kernel(x, c, ctx, c_ctx, mod_w, mod_b, norm_g, ffn_w1, ffn_w3, ffn_w2, rwkv_mu, rwkv_wr, rwkv_wk, rwkv_wv, rwkv_wo, rwkv_k_k, rwkv_k_a, rwkv_r_k, rwkv_w0, rwkv_w1, rwkv_w2, rwkv_a0, rwkv_a1, rwkv_a2, rwkv_g1, rwkv_g2, rwkv_ln_w, rwkv_ln_b, rwkv_v0, rwkv_v1, rwkv_v2, gattn_wq, gattn_wk, gattn_wv, gattn_wo, gattn_q_norm, gattn_k_norm, wattn_wq, wattn_wk, wattn_wv, wattn_wo, wattn_sink):
    batch, seq, d = x.shape
    depth = mod_w.shape[0]
    st = _Stream(batch, ctx.shape[1], seq, d)

    rows = -(-(batch + FFN_SEQS) // 8) * 8
    c_rows = jnp.zeros((rows, d), F32).at[:batch].set(c).at[batch:batch + FFN_SEQS].set(c_ctx)
    mods = _modulation(c_rows, mod_w, mod_b).reshape(depth, rows, N_MOD, d)

    ffn_w = (ffn_w1.astype(BF16), ffn_w3.astype(BF16), ffn_w2.astype(BF16))
    rope = _rope_tables(st)
    seg = _segment_ones()
    bf = lambda w: w.astype(BF16)

    h = (ctx, x)
    v_first = None
    for i in range(depth):
        kind, j = i % 3, i // 3
        g, mod = norm_g[i], mods[i]
        h = _ffn_half(st, h, mod, g, ffn_w, i, 0)
        mixer = None
        if kind == 0:
            p = {"mu": rwkv_mu[j], "wr": bf(rwkv_wr[j]), "wk": bf(rwkv_wk[j]), "wv": bf(rwkv_wv[j]),
                 "wo": bf(rwkv_wo[j]), "k_k": rwkv_k_k[j][None], "k_a": rwkv_k_a[j][None],
                 "r_k": rwkv_r_k[j].reshape(1, d),
                 "w0": rwkv_w0[j][:, None], "w1": bf(_pad_to(rwkv_w1[j], 2, LANES)), "w2": bf(_pad_to(rwkv_w2[j], 1, LANES)),
                 "a0": rwkv_a0[j][:, None], "a1": bf(_pad_to(rwkv_a1[j], 2, LANES)), "a2": bf(_pad_to(rwkv_a2[j], 1, LANES)),
                 "g1": bf(_pad_to(rwkv_g1[j], 2, LANES)), "g2": bf(_pad_to(rwkv_g2[j], 1, LANES)),
                 "ln_w": rwkv_ln_w[j][:, None], "ln_b": rwkv_ln_b[j][:, None]}
            vres = None
            if j > 0:
                vres = (v_first, rwkv_v0[j - 1][None], bf(_pad_to(rwkv_v1[j - 1], 1, LANES)),
                        bf(_pad_to(rwkv_v2[j - 1], 0, LANES)))
            r, v, kk, lw, kd, ab, gate = _rwkv_features(st, h, mod, g, p, seg, vres)
            if j == 0:
                v_first = v
            y_f = _rwkv_scan(st, r, lw, kd, v, kk, ab, 0)
            y_b = _rwkv_scan(st, r, lw, kd, v, kk, ab, 1)
            h = _rwkv_out(st, h, mod, g, y_f, y_b, r, kd, v, gate, p, seg)
        else:
            if kind == 1:
                a = _attn_params(gattn_wq[j], gattn_wk[j], gattn_wv[j], gattn_wo[j],
                                 gattn_q_norm[j], gattn_k_norm[j])
                sink = None
            else:
                ones = jnp.ones((HEAD_DIM,), F32)
                a = _attn_params(wattn_wq[j], wattn_wk[j], wattn_wv[j], wattn_wo[j], ones, ones)
                sink = wattn_sink[j]
            q, k, v = _qkv_project(st, h, mod, g, a, rope, seg, qk_norm=kind == 1)
            mixer = (_attention(st, q, k, v, sink), a["wo"])
        h = _ffn_half(st, h, mod, g, ffn_w, i, 1, latent_only=i == depth - 1, mixer=mixer)
    return h
```

```python
import functools

import numpy as np
import jax
import jax.numpy as jnp
from jax import lax
from jax.experimental import pallas as pl
from jax.experimental.pallas import tpu as pltpu

F32 = jnp.float32
BF16 = jnp.bfloat16
HIGHEST = lax.Precision.HIGHEST

GRID_W = 64
N_MOD = 9
HEAD_DIM = 64
N_KV_HEADS = 4
WINDOW = 128
ROPE_THETA = 10000.0
N_FREQ = HEAD_DIM // 4
RWKV_HEAD = 64
NORM_EPS = 1e-6
GN_EPS = 64e-5
L2_EPS = 1e-12
NEG_INF = -1e30
LOG2_E = 1.4426950408889634
EXP_NEG_HALF = 0.6065306597126334

LANES = 128
SUBLANES = 8
TILE = 256
ROW_SEQS = 2
SCAN_SEQS = 4
ATTN_KV_HEADS = 2
CHUNK = 64
MAX_KEYS = 1152
SEG_TILE = 256
LOCAL_KEYS = 3 * TILE
VMEM_LIMIT = 56 * 1024 * 1024


def _cparams(*sem):
    return pltpu.CompilerParams(dimension_semantics=sem, vmem_limit_bytes=VMEM_LIMIT)


def _const_spec(shape):
    nd = len(shape)
    return pl.BlockSpec(shape, lambda *_: (0,) * nd, pipeline_mode=pl.Buffered(1))


def _dot(a, b):
    return jnp.dot(a.astype(BF16), b.astype(BF16), preferred_element_type=F32)


def _dot_nt(a, b):
    return lax.dot_general(a.astype(BF16), b.astype(BF16), (((1,), (1,)), ((), ())),
                           preferred_element_type=F32)


def _dot_tn(a, b):
    return lax.dot_general(a.astype(BF16), b.astype(BF16), (((0,), (0,)), ((), ())),
                           preferred_element_type=F32)


def _dot_f32(a, b):
    return jnp.dot(a, b, precision=HIGHEST, preferred_element_type=F32)


def _sigmoid(x):
    return 1.0 / (1.0 + jnp.exp2(x * -LOG2_E))


def _rms(x, g):
    return x * lax.rsqrt(jnp.mean(x * x, axis=-1, keepdims=True) + NORM_EPS) * g


def _norm_mod(x, g, scale, shift):
    return _rms(x, g) * (1.0 + scale) + shift


def _seg_sum(x, ones_ref, pieces):
    ones = ones_ref[...]
    parts, rest = [], x
    for i in range(pieces):
        part = rest.astype(BF16)
        parts.append(part)
        if i + 1 < pieces:
            rest = rest - part.astype(F32)
    cols = []
    for j in range(x.shape[1] // SEG_TILE):
        sl = slice(j * SEG_TILE, (j + 1) * SEG_TILE)
        cols.append(sum(jnp.dot(part[:, sl], ones, preferred_element_type=F32) for part in parts))
    return jnp.concatenate(cols, axis=1)


def _mod_body(c_ref, w_ref, b_ref, o_ref):
    c = c_ref[...]
    o_ref[0] = _dot_f32(c * _sigmoid(c), w_ref[0]) + b_ref[0]


def _modulation(c_rows, mod_w, mod_b):
    depth, d, nd = mod_w.shape
    rows = c_rows.shape[0]
    tn = nd // 6
    return pl.pallas_call(
        _mod_body,
        grid=(depth, nd // tn),
        in_specs=[_const_spec((rows, d)),
                  pl.BlockSpec((1, d, tn), lambda i, j: (i, 0, j)),
                  pl.BlockSpec((1, 1, tn), lambda i, j: (i, 0, j))],
        out_specs=pl.BlockSpec((1, rows, tn), lambda i, j: (i, 0, j)),
        out_shape=jax.ShapeDtypeStruct((depth, rows, nd), F32),
        compiler_params=_cparams("arbitrary", "arbitrary"),
        name="adaln_rows",
    )(c_rows, mod_w, mod_b.reshape(depth, 1, nd))


class _Stream:
    def __init__(self, batch, ctx_len, seq, d):
        assert ctx_len % TILE == 0 and seq % TILE == 0 and d % LANES == 0
        self.b, self.t, self.d = batch, ctx_len + seq, d
        self.ctx = ctx_len
        self.n_tiles = self.t // TILE
        self.n_ctx_tiles = ctx_len // TILE

        self.seqs = ROW_SEQS if batch % ROW_SEQS == 0 else 1

    def tile_spec(self, width=None, seqs=1):
        return pl.BlockSpec((seqs, TILE, width or self.d), lambda b, t: (b, t, 0))

    def mod_spec(self, seqs=1):
        nct, ctx_blk = self.n_ctx_tiles, self.b // seqs
        return pl.BlockSpec((seqs, N_MOD, self.d), lambda b, t: (jnp.where(t < nct, ctx_blk, b), 0, 0))

    def out_shape(self, width=None, dtype=F32):
        return jax.ShapeDtypeStruct((self.b, self.t, width or self.d), dtype)


def _ffn_body(*refs, mod_row, g_row, with_mixer, n_ctx_tiles, split_input):
    refs = list(refs)
    if split_input:
        ctx_ref, lat_ref = refs[:2]
        x = jnp.where(pl.program_id(1) < n_ctx_tiles, ctx_ref[...], lat_ref[...])
        refs = refs[2:]
    else:
        x = refs.pop(0)[...]
    if with_mixer:
        mod_ref, g_ref, w1_ref, w3_ref, w2_ref, o_ref, wo_ref, out_ref = refs
    else:
        mod_ref, g_ref, w1_ref, w3_ref, w2_ref, out_ref = refs
    m = mod_ref[...]
    g = g_ref[...]
    if with_mixer:
        o = o_ref[...]
        mixed = jnp.dot(o.reshape(-1, o.shape[-1]), wo_ref[...], preferred_element_type=F32).reshape(x.shape)
        x = x + m[:, 5:6] * _rms(mixed, g[3:4])
    shift, scale, gate = m[:, mod_row:mod_row + 1], m[:, mod_row + 1:mod_row + 2], m[:, mod_row + 2:mod_row + 3]
    n = _norm_mod(x, g[g_row:g_row + 1], scale, shift).astype(BF16).reshape(-1, x.shape[-1])
    a = jnp.dot(n, w1_ref[0, 0], preferred_element_type=F32)
    b = jnp.dot(n, w3_ref[0, 0], preferred_element_type=F32)
    h = (a * _sigmoid(a) * b).astype(BF16)
    y = jnp.dot(h, w2_ref[0, 0], preferred_element_type=F32).reshape(x.shape)
    out_ref[...] = x + 0.5 * gate * _rms(y, g[g_row + 1:g_row + 2])


def _ffn_half(st, x, mod, g, weights, layer, half, latent_only=False, mixer=None):
    w1, w3, w2 = weights
    d, f = w1.shape[-2:]
    nb = st.seqs
    nct, ctx_blk = st.n_ctx_tiles, st.b // nb
    skip = nct if latent_only else 0
    split_input = isinstance(x, tuple)

    def rows_spec(width):
        return pl.BlockSpec((nb, TILE, width), lambda b, t: (b, t + skip, 0))

    def weight_spec(w):
        return pl.BlockSpec((1, 1) + w.shape[-2:], lambda b, t: (layer, half, 0, 0), pipeline_mode=pl.Buffered(1))

    if split_input:
        specs = [pl.BlockSpec((nb, TILE, d), lambda b, t: (b, jnp.minimum(t, nct - 1), 0)),
                 pl.BlockSpec((nb, TILE, d), lambda b, t: (b, jnp.maximum(t - nct, 0), 0))]
        args = list(x)
    else:
        specs, args = [rows_spec(d)], [x]
    mod_spec = pl.BlockSpec((nb, N_MOD, d), lambda b, t: (jnp.where(t + skip < nct, ctx_blk, b), 0, 0))
    specs += [mod_spec, _const_spec(g.shape), weight_spec(w1), weight_spec(w3), weight_spec(w2)]
    args += [mod, g, w1, w3, w2]
    if mixer is not None:
        o, wo = mixer
        specs += [rows_spec(o.shape[-1]), _const_spec(wo.shape)]
        args += [o, wo]
    return pl.pallas_call(
        functools.partial(_ffn_body, mod_row=6 * half, g_row=4 * half, with_mixer=mixer is not None,
                          n_ctx_tiles=nct, split_input=split_input),
        grid=(st.b // nb, st.n_tiles - skip),
        in_specs=specs,
        out_specs=pl.BlockSpec((nb, TILE, d), lambda b, t: (b, t, 0)),
        out_shape=jax.ShapeDtypeStruct((st.b, st.t - skip * TILE, d), F32),
        compiler_params=_cparams("parallel", "parallel"),
        name="swiglu_half",
    )(*args)


def _rwkv_feat_body(*refs, n_ctx_tiles, n_tiles, with_vres):
    (x_ref, xp_ref, xn_ref, mod_ref, g_ref, mu_ref, wr_ref, wk_ref, wv_ref, kk_ref, ka_ref,
     w0_ref, w1_ref, w2_ref, a0_ref, a1_ref, a2_ref, g1_ref, g2_ref, ones_ref) = refs[:20]
    rest = refs[20:]
    if with_vres:
        vf_ref, v0_ref, v1_ref, v2_ref = rest[:4]
        rest = rest[4:]
    r_out, v_out, kk_out, lw_out, kd_out, ab_out, g_out = rest

    t = pl.program_id(1)
    m = mod_ref[0]
    gn = g_ref[...][2:3]
    shift, scale = m[3:4], m[4:5]
    n = _norm_mod(x_ref[0], gn, scale, shift)
    first = jnp.logical_or(t == 0, t == n_ctx_tiles)
    last = jnp.logical_or(t == n_ctx_tiles - 1, t == n_tiles - 1)
    n_before = jnp.where(first, 0.0, _norm_mod(xp_ref[0][7:8], gn, scale, shift))
    n_after = jnp.where(last, 0.0, _norm_mod(xn_ref[0][0:1], gn, scale, shift))
    row = lax.broadcasted_iota(jnp.int32, (TILE, 1), 0)
    n_prev = jnp.where(row == 0, n_before, pltpu.roll(n, 1, 0))
    n_next = jnp.where(row == TILE - 1, n_after, pltpu.roll(n, TILE - 1, 0))
    xx = 0.5 * (n_prev + n_next) - n
    mu = mu_ref[...]
    xr, xw, xk, xv, xa, xg = (n + xx * mu[i:i + 1] for i in range(6))

    r = _dot(xr, wr_ref[...])
    k = _dot(xk, wk_ref[...])
    v = _dot(xv, wv_ref[...])
    if with_vres:
        lora = _dot(_dot(xv, v1_ref[...]), v2_ref[...])
        v = v + (vf_ref[0] - v) * _sigmoid(v0_ref[...] + lora)
    kk = k * kk_ref[...]
    kk = kk / jnp.maximum(jnp.sqrt(_seg_sum(kk * kk, ones_ref, 2)), L2_EPS)
    r_out[0] = r.astype(BF16)
    v_out[0] = v
    kk_out[0] = kk
    k_ka = k * ka_ref[...]
    k_rest = k - k_ka
    for dr in range(2):
        wl = w0_ref[dr] + _dot(jnp.tanh(_dot(xw, w1_ref[dr])), w2_ref[dr])
        a = _sigmoid(a0_ref[dr] + _dot(_dot(xa, a1_ref[dr]), a2_ref[dr]))
        lw_out[dr, 0] = -EXP_NEG_HALF * _sigmoid(wl)
        kd_out[dr, 0] = (k_rest + k_ka * a).astype(BF16)
        ab_out[dr, 0] = kk * a
        g_out[dr, 0] = _dot(_sigmoid(_dot(xg, g1_ref[dr])), g2_ref[dr]).astype(BF16)


def _rwkv_features(st, x, mod, g, p, seg, vres):
    d = st.d
    n8 = st.t // 8
    per8 = TILE // 8
    prev_spec = pl.BlockSpec((1, 8, d), lambda b, t: (b, jnp.maximum(t * per8 - 1, 0), 0))
    next_spec = pl.BlockSpec((1, 8, d), lambda b, t: (b, jnp.minimum((t + 1) * per8, n8 - 1), 0))
    dir_spec = pl.BlockSpec((2, 1, TILE, d), lambda b, t: (0, b, t, 0))
    dir_f32 = jax.ShapeDtypeStruct((2, st.b, st.t, d), F32)
    dir_bf16 = jax.ShapeDtypeStruct((2, st.b, st.t, d), BF16)
    names = ("mu", "wr", "wk", "wv", "k_k", "k_a", "w0", "w1", "w2", "a0", "a1", "a2", "g1", "g2")
    consts = [p[nm] for nm in names] + [seg]
    args = [x, x, x, mod, g] + consts
    specs = [st.tile_spec(), prev_spec, next_spec, st.mod_spec(), _const_spec(g.shape)]
    specs += [_const_spec(c.shape) for c in consts]
    if vres is not None:
        v_first, v0, v1, v2 = vres
        args += [v_first, v0, v1, v2]
        specs += [st.tile_spec(), _const_spec(v0.shape), _const_spec(v1.shape), _const_spec(v2.shape)]
    return pl.pallas_call(
        functools.partial(_rwkv_feat_body, n_ctx_tiles=st.n_ctx_tiles, n_tiles=st.n_tiles,
                          with_vres=vres is not None),
        grid=(st.b, st.n_tiles),
        in_specs=specs,
        out_specs=[st.tile_spec()] * 3 + [dir_spec] * 4,
        out_shape=[st.out_shape(dtype=BF16), st.out_shape(), st.out_shape(), dir_f32, dir_bf16, dir_f32, dir_bf16],
        compiler_params=_cparams("parallel", "parallel"),
        name="rwkv_features",
    )(*args)


def _unit_triangular_inverses(mats, ri, ci, size, upper):
    n = mats[0].shape[0]
    eye = (ri == ci).astype(F32)
    ts = [jnp.where((ri >> 1) == (ci >> 1), a, 0.0) + eye for a in mats]
    level = 1
    while (2 << level) <= size:
        m = 1 << level
        off = jnp.logical_and((ri >> (level + 1)) == (ci >> (level + 1)), (ri >> level) != (ci >> level))
        a_off = [jnp.where(off, a, 0.0) for a in mats]
        if m % SUBLANES == 0:
            moving = range(0 if upper else 1, n // m, 2)

            def rows(x):
                return jnp.concatenate([x[b * m:(b + 1) * m] for b in moving], axis=0)

            zs = [_dot(_dot(rows(t), a), t) for t, a in zip(ts, a_off)]
            ts = [jnp.concatenate([t[b * m:(b + 1) * m] + z[(b // 2) * m:(b // 2 + 1) * m] if b in moving
                                   else t[b * m:(b + 1) * m] for b in range(n // m)], axis=0)
                  for t, z in zip(ts, zs)]
        else:
            ts = [t + _dot(_dot(t, a), t) for t, a in zip(ts, a_off)]
        level += 1
    return ts


def _scan_body(r_ref, lw_ref, k_ref, v_ref, kk_ref, ab_ref, y_ref, s_ref, *, reverse):
    c = CHUNK
    n2 = 2 * c

    @pl.when(pl.program_id(1) == 0)
    def _():
        s_ref[...] = jnp.zeros_like(s_ref)

    row = lax.broadcasted_iota(jnp.int32, (c, 1), 0)

    def running_sum(x):
        step = 1
        while step < c:
            if reverse:
                x = x + jnp.where(row < c - step, pltpu.roll(x, c - step, 0), 0.0)
            else:
                x = x + jnp.where(row >= step, pltpu.roll(x, step, 0), 0.0)
            step *= 2
        return x

    ri = lax.broadcasted_iota(jnp.int32, (n2, n2), 0)
    ci = lax.broadcasted_iota(jnp.int32, (n2, n2), 1)
    same = (ri >> 6) == (ci >> 6)
    tt, ss = ri & (c - 1), ci & (c - 1)
    before = (ss > tt) if reverse else (ss < tt)
    strict = jnp.logical_and(same, before)
    incl = jnp.logical_and(same, jnp.logical_or(before, ss == tt))
    incl2 = jnp.concatenate([incl, incl], axis=1)
    even = lax.broadcasted_iota(jnp.int32, (1, LANES), 1) < RWKV_HEAD

    def stack(x):
        return jnp.concatenate([jnp.where(even, x, 0.0), jnp.where(even, 0.0, x)], axis=0)

    n_pairs = r_ref.shape[-1] // LANES
    chains = [(q, p) for q in range(r_ref.shape[0]) for p in range(n_pairs)]
    cols = {}
    for q in range(r_ref.shape[0]):
        r, k = r_ref[q].astype(F32), k_ref[0, q].astype(F32)
        lw, v, kk, ab = lw_ref[0, q], v_ref[q], kk_ref[q], ab_ref[0, q]
        cum = running_sum(lw)
        tot = cum[0:1] if reverse else cum[c - 1:c]
        inv = jnp.exp(-cum)
        to_end = jnp.exp(tot - cum)
        full = {"a": -kk * jnp.exp(cum - lw), "r": r * jnp.exp(cum), "b": ab * inv, "k": k * inv,
                "be": ab * to_end, "ke": k * to_end, "w": jnp.exp(tot), "v": v}
        for p in range(n_pairs):
            cols[q, p] = {nm: x[:, p * LANES:(p + 1) * LANES] for nm, x in full.items()}

    prods = [_dot_nt(jnp.concatenate([stack(cols[ch]["a"]), stack(cols[ch]["r"])], axis=0),
                     jnp.concatenate([stack(cols[ch]["b"]), stack(cols[ch]["k"])], axis=0)) for ch in chains]
    a_b = [jnp.where(strict, pr[0:n2, 0:n2], 0.0) for pr in prods]
    t_inv = _unit_triangular_inverses(a_b, ri, ci, c, reverse)
    v_st = [stack(cols[ch]["v"]) for ch in chains]
    u_own = [_dot(jnp.where(strict, pr[0:n2, n2:2 * n2], 0.0), vs) for pr, vs in zip(prods, v_st)]
    from_state = [_dot_nt(jnp.concatenate([cols[ch]["a"], cols[ch]["r"]], axis=0), s_ref[ch])
                  for ch in chains]
    u_st = [_dot(t, stack(fs[0:c]) + uo) for t, fs, uo in zip(t_inv, from_state, u_own)]
    y_st = [_dot(jnp.where(incl2, pr[n2:2 * n2], 0.0), jnp.concatenate([us, vs], axis=0))
            for pr, us, vs in zip(prods, u_st, v_st)]
    for i, (q, p) in enumerate(chains):
        y_ref[q, :, p * LANES:(p + 1) * LANES] = (from_state[i][c:2 * c] + y_st[i][0:c] + y_st[i][c:n2]).astype(BF16)
        u = u_st[i][0:c] + u_st[i][c:n2]
        upd = _dot_tn(jnp.concatenate([u, cols[q, p]["v"]], axis=0),
                      jnp.concatenate([cols[q, p]["be"], cols[q, p]["ke"]], axis=0))
        s_ref[q, p] = s_ref[q, p] * cols[q, p]["w"] + jnp.where(same, upd, 0.0)


def _rwkv_scan(st, r, lw, kd, v, kk, ab, direction):
    d = st.d
    nq = SCAN_SEQS if st.b % SCAN_SEQS == 0 else 1
    nc, ncc = st.t // CHUNK, st.ctx // CHUNK
    if direction == 0:
        def cidx(s):
            return s
    else:
        def cidx(s):
            return jnp.where(s < ncc, ncc - 1 - s, nc - 1 - s + ncc)
    shared = pl.BlockSpec((nq, CHUNK, d), lambda b, s: (b, cidx(s), 0))
    per_dir = pl.BlockSpec((1, nq, CHUNK, d), lambda b, s: (direction, b, cidx(s), 0))
    return pl.pallas_call(
        functools.partial(_scan_body, reverse=direction == 1),
        grid=(st.b // nq, nc),
        in_specs=[shared, per_dir, per_dir, shared, shared, per_dir],
        out_specs=shared,
        out_shape=st.out_shape(dtype=BF16),
        scratch_shapes=[pltpu.VMEM((nq, d // LANES, LANES, LANES), F32)],
        compiler_params=_cparams("parallel", "arbitrary"),
        name="rwkv_scan_bwd" if direction else "rwkv_scan_fwd",
    )(r, lw, kd, v, kk, ab)


def _rwkv_out_body(x_ref, mod_ref, g_ref, yf_ref, yb_ref, r_ref, kd_ref, v_ref, gate_ref,
                   lnw_ref, lnb_ref, rk_ref, wo_ref, ones_ref, o_ref):
    x = x_ref[...]
    flat = (-1, x.shape[-1])
    r, v = r_ref[...].astype(F32).reshape(flat), v_ref[...].reshape(flat)
    inv_n = 1.0 / RWKV_HEAD
    o = None
    for dr, y_ref in enumerate((yf_ref, yb_ref)):
        y = y_ref[...].astype(F32).reshape(flat)
        cen = y - _seg_sum(y, ones_ref, 1) * inv_n
        var = _seg_sum(cen * cen, ones_ref, 1) * inv_n
        yn = cen * lax.rsqrt(var + GN_EPS) * lnw_ref[dr] + lnb_ref[dr]
        bonus = _seg_sum(r * kd_ref[dr].astype(F32).reshape(flat) * rk_ref[...], ones_ref, 1) * v
        od = (yn + bonus) * gate_ref[dr].astype(F32).reshape(flat)
        o = od if o is None else o + od
    y_mix = _dot(o, wo_ref[...]).reshape(x.shape)
    o_ref[...] = x + mod_ref[...][:, 5:6] * _rms(y_mix, g_ref[...][3:4])


def _rwkv_out(st, x, mod, g, y_f, y_b, r, kd, v, gate, p, seg):
    d, nb = st.d, st.seqs
    rows = st.tile_spec(seqs=nb)
    dir_spec = pl.BlockSpec((2, nb, TILE, d), lambda b, t: (0, b, t, 0))
    consts = [p["ln_w"], p["ln_b"], p["r_k"], p["wo"]] + [seg]
    return pl.pallas_call(
        _rwkv_out_body,
        grid=(st.b // nb, st.n_tiles),
        in_specs=[rows, st.mod_spec(nb), _const_spec(g.shape), rows, rows, rows, dir_spec, rows, dir_spec]
                 + [_const_spec(c.shape) for c in consts],
        out_specs=rows,
        out_shape=st.out_shape(),
        compiler_params=_cparams("parallel", "parallel"),
        name="rwkv_readout",
    )(x, mod, g, y_f, y_b, r, kd, v, gate, *consts)


def _qkv_body(x_ref, mod_ref, g_ref, wq_ref, wk_ref, wv_ref, gq_ref, gk_ref, ones_ref,
              cos_ref, sin_ref, q_out, k_out, v_out, *, qk_norm, q_scale):
    x, m = x_ref[...], mod_ref[...]
    nb = x.shape[0]
    n = _norm_mod(x, g_ref[...][2:3], m[:, 4:5], m[:, 3:4]).astype(BF16).reshape(-1, x.shape[-1])
    cos = jnp.concatenate([cos_ref[...]] * nb, axis=0)
    sin = jnp.concatenate([sin_ref[...]] * nb, axis=0)
    lane = lax.broadcasted_iota(jnp.int32, (1, LANES), 1)
    second_half = (lane & N_FREQ) != 0
    tail = lane >= HEAD_DIM

    def rotated(w_ref, gain_ref, out_ref, scale):
        t_all = jnp.dot(n, w_ref[...], preferred_element_type=F32)
        if qk_norm:
            mean_sq = _seg_sum(t_all * t_all, ones_ref, 2) * (1.0 / HEAD_DIM)
            t_all = t_all * lax.rsqrt(mean_sq + NORM_EPS) * gain_ref[...]
        for j in range(t_all.shape[1] // LANES):
            t = t_all[:, j * LANES:(j + 1) * LANES]
            partner = jnp.where(second_half, pltpu.roll(t, N_FREQ, 1), pltpu.roll(t, LANES - N_FREQ, 1))
            out = t * cos + partner * sin
            out = (out * scale if scale != 1.0 else out).astype(BF16)
            for q in range(nb):
                out_ref[q, 2 * j] = out[q * TILE:(q + 1) * TILE, 0:HEAD_DIM]
                out_ref[q, 2 * j + 1] = out[q * TILE:(q + 1) * TILE, HEAD_DIM:LANES]

    rotated(wq_ref, gq_ref, q_out, q_scale)
    rotated(wk_ref, gk_ref, k_out, 1.0)
    v_all = jnp.dot(n, wv_ref[...], preferred_element_type=F32)
    for j in range(v_all.shape[1] // LANES):
        v2 = v_all[:, j * LANES:(j + 1) * LANES]
        first = jnp.where(tail, 1.0, v2).astype(BF16)
        second = jnp.where(tail, 1.0, pltpu.roll(v2, HEAD_DIM, 1)).astype(BF16)
        for q in range(nb):
            v_out[q, 2 * j] = first[q * TILE:(q + 1) * TILE]
            v_out[q, 2 * j + 1] = second[q * TILE:(q + 1) * TILE]


def _qkv_project(st, x, mod, g, a, rope, seg, qk_norm):
    nh, nkv = a["wq"].shape[1] // HEAD_DIM, a["wk"].shape[1] // HEAD_DIM
    consts = [a["wq"], a["wk"], a["wv"], a["gq"], a["gk"], seg]
    rope_spec = pl.BlockSpec((TILE, LANES), lambda b, t: (t, 0))

    nb = st.seqs

    def head_spec(n, width):
        return pl.BlockSpec((nb, n, TILE, width), lambda b, t: (b, 0, t, 0))

    def head_shape(n, width):
        return jax.ShapeDtypeStruct((st.b, n, st.t, width), BF16)

    return pl.pallas_call(
        functools.partial(_qkv_body, qk_norm=qk_norm, q_scale=HEAD_DIM ** -0.5 * LOG2_E),
        grid=(st.b // nb, st.n_tiles),
        in_specs=[st.tile_spec(seqs=nb), st.mod_spec(nb), _const_spec(g.shape)]
                 + [_const_spec(c.shape) for c in consts] + [rope_spec, rope_spec],
        out_specs=[head_spec(nh, HEAD_DIM), head_spec(nkv, HEAD_DIM), head_spec(nkv, LANES)],
        out_shape=[head_shape(nh, HEAD_DIM), head_shape(nkv, HEAD_DIM), head_shape(nkv, LANES)],
        compiler_params=_cparams("parallel", "parallel"),
        name="qkv_rope",
    )(x, mod, g, *consts, *rope)


def _key_chunks(k0, k1):
    n = -(-(k1 - k0) // MAX_KEYS)
    size = -(-(k1 - k0) // (n * LANES)) * LANES
    return [(a, min(size, k1 - a)) for a in range(k0, k1, size)]


def _attn_body(*refs, ctx_len, n_ctx_tiles, total, windowed):
    if windowed:
        sink_ref, q_ref, k_ref, v_ref, o_ref = refs
    else:
        q_ref, k_ref, v_ref, o_ref = refs
    heads = q_ref.shape[1]
    group = heads // k_ref.shape[1]
    t = pl.program_id(2)
    tail = lax.broadcasted_iota(jnp.int32, (1, LANES), 1) >= HEAD_DIM
    chains = range(heads)
    qs = [q_ref[0, c] for c in chains]

    def scores(k0, size):
        return [_dot_nt(q, k_ref[0, c // group, pl.ds(k0, size), :]) for c, q in zip(chains, qs)]

    def weigh(ps, k0, size):
        return [jnp.dot(p.astype(BF16), v_ref[0, c // group, pl.ds(k0, size), :], preferred_element_type=F32)
                for c, p in zip(chains, ps)]

    def finish(accs):
        invs = [1.0 / jnp.where(tail, acc, 1.0) for acc in accs]
        cols = []
        for c in range(0, heads, 2):
            first = accs[c] * pltpu.roll(invs[c], HEAD_DIM, 1)
            second = pltpu.roll(accs[c + 1], HEAD_DIM, 1) * invs[c + 1]
            cols.append(jnp.where(tail, second, first))
        return jnp.concatenate(cols, axis=1)

    def row_max(xs):
        return [jnp.max(x, axis=-1, keepdims=True) for x in xs]

    if windowed:
        sinks = [jnp.full((TILE, 1), sink_ref[pl.program_id(1) * heads + c], F32) * LOG2_E for c in chains]

        def with_sink(s_parts, spans):
            ms = sinks
            for part in s_parts:
                ms = [jnp.maximum(m, pm) for m, pm in zip(ms, row_max(part))]
            accs = [jnp.where(tail, jnp.exp2(sk - m), 0.0) for sk, m in zip(sinks, ms)]
            for part, (k0, size) in zip(s_parts, spans):
                pvs = weigh([jnp.exp2(s - m) for s, m in zip(part, ms)], k0, size)
                accs = [acc + pv for acc, pv in zip(accs, pvs)]
            return finish(accs)

        def context_queries():
            return with_sink([scores(0, ctx_len)], [(0, ctx_len)])

        def latent_queries():
            k0 = pl.multiple_of(jnp.clip((t - 1) * TILE, 0, total - LOCAL_KEYS), TILE)
            k_pos = k0 + lax.broadcasted_iota(jnp.int32, (TILE, LOCAL_KEYS), 1)
            q_pos = t * TILE + lax.broadcasted_iota(jnp.int32, (TILE, LOCAL_KEYS), 0)
            mask = jnp.logical_and(jnp.abs(k_pos - q_pos) <= WINDOW, k_pos >= ctx_len)
            s_loc = [jnp.where(mask, s, NEG_INF) for s in scores(k0, LOCAL_KEYS)]
            return with_sink([scores(0, ctx_len), s_loc], [(0, ctx_len), (k0, LOCAL_KEYS)])
    else:
        def online(spans):
            ms = accs = None
            for k0, size in spans:
                ss = scores(k0, size)
                m_chunk = row_max(ss)
                if ms is None:
                    ms, accs = m_chunk, weigh([jnp.exp2(s - m) for s, m in zip(ss, m_chunk)], k0, size)
                else:
                    m_new = [jnp.maximum(m, mc) for m, mc in zip(ms, m_chunk)]
                    pvs = weigh([jnp.exp2(s - m) for s, m in zip(ss, m_new)], k0, size)
                    accs = [jnp.exp2(m - mn) * acc + pv for m, mn, acc, pv in zip(ms, m_new, accs, pvs)]
                    ms = m_new
            return finish(accs)

        def context_queries():
            return online([(0, ctx_len)])

        def latent_queries():
            return online(_key_chunks(0, total))

    o_ref[0] = lax.cond(t < n_ctx_tiles, context_queries, latent_queries).astype(BF16)


def _attention(st, q, k, v, sink):
    nh, nkv = q.shape[1], k.shape[1]
    group = nh // nkv
    windowed = sink is not None
    assert st.t >= LOCAL_KEYS
    kvs = ATTN_KV_HEADS if nkv % ATTN_KV_HEADS == 0 else 1
    q_spec = pl.BlockSpec((1, kvs * group, TILE, HEAD_DIM), lambda b, h, t: (b, h, t, 0))
    k_spec = pl.BlockSpec((1, kvs, st.t, HEAD_DIM), lambda b, h, t: (b, h, 0, 0))
    v_spec = pl.BlockSpec((1, kvs, st.t, LANES), lambda b, h, t: (b, h, 0, 0))
    specs, args = [q_spec, k_spec, v_spec], [q, k, v]
    if windowed:
        specs = [pl.BlockSpec(memory_space=pltpu.SMEM)] + specs
        args = [sink] + args
    return pl.pallas_call(
        functools.partial(_attn_body, ctx_len=st.ctx, n_ctx_tiles=st.n_ctx_tiles, total=st.t,
                          windowed=windowed),
        grid=(st.b, nkv // kvs, st.n_tiles),
        in_specs=specs,
        out_specs=pl.BlockSpec((1, TILE, kvs * group * HEAD_DIM), lambda b, h, t: (b, t, h)),
        out_shape=jax.ShapeDtypeStruct((st.b, st.t, nh * HEAD_DIM), BF16),
        compiler_params=_cparams("parallel", "parallel", "parallel"),
        name="window_attention" if windowed else "global_attention",
    )(*args)


def _pad_to(w, axis, mult):
    pad = (-w.shape[axis]) % mult
    if pad == 0:
        return w
    widths = [(0, 0)] * w.ndim
    widths[axis] = (0, pad)
    return jnp.pad(w, widths)


def _rope_tables(st):
    rows = (st.t - st.ctx) // GRID_W
    row = jnp.broadcast_to(jnp.arange(rows)[:, None], (rows, GRID_W)).reshape(-1)
    col = jnp.broadcast_to(jnp.arange(GRID_W)[None, :], (rows, GRID_W)).reshape(-1)
    inv_freq = ROPE_THETA ** (-jnp.arange(N_FREQ, dtype=F32) / N_FREQ)
    ang = jnp.stack([row, col], axis=-1).astype(F32)[:, :, None] * inv_freq
    cos, sin = jnp.cos(ang), jnp.sin(ang)
    cos_t = jnp.stack([cos, cos], axis=2).reshape(-1, HEAD_DIM)
    sin_t = jnp.stack([-sin, sin], axis=2).reshape(-1, HEAD_DIM)
    cos_t = jnp.concatenate([jnp.ones((st.ctx, HEAD_DIM), F32), cos_t], axis=0)
    sin_t = jnp.concatenate([jnp.zeros((st.ctx, HEAD_DIM), F32), sin_t], axis=0)
    return jnp.tile(cos_t, (1, LANES // HEAD_DIM)), jnp.tile(sin_t, (1, LANES // HEAD_DIM))


def _attn_params(wq, wk, wv, wo, gq, gk):
    nh, nkv = wq.shape[1] // HEAD_DIM, wk.shape[1] // HEAD_DIM
    return {"wq": wq.astype(BF16), "wk": wk.astype(BF16), "wv": wv.astype(BF16), "wo": wo.astype(BF16),
            "gq": jnp.tile(gq, nh)[None], "gk": jnp.tile(gk, nkv)[None]}


def _segment_ones():
    head = np.arange(SEG_TILE) // RWKV_HEAD
    return jnp.asarray(head[:, None] == head[None, :], dtype=BF16)


def kernel(x, c, ctx, c_ctx, mod_w, mod_b, norm_g, ffn_w1, ffn_w3, ffn_w2, rwkv_mu, rwkv_wr, rwkv_wk, rwkv_wv, rwkv_wo, rwkv_k_k, rwkv_k_a, rwkv_r_k, rwkv_w0, rwkv_w1, rwkv_w2, rwkv_a0, rwkv_a1, rwkv_a2, rwkv_g1, rwkv_g2, rwkv_ln_w, rwkv_ln_b, rwkv_v0, rwkv_v1, rwkv_v2, gattn_wq, gattn_wk, gattn_wv, gattn_wo, gattn_q_norm, gattn_k_norm, wattn_wq, wattn_wk, wattn_wv, wattn_wo, wattn_sink):
    batch, seq, d = x.shape
    depth = mod_w.shape[0]
    st = _Stream(batch, ctx.shape[1], seq, d)

    rows = -(-(batch + ROW_SEQS) // SUBLANES) * SUBLANES
    c_rows = jnp.zeros((rows, d), F32).at[:batch].set(c).at[batch:batch + ROW_SEQS].set(c_ctx)
    mods = _modulation(c_rows, mod_w, mod_b).reshape(depth, rows, N_MOD, d)

    ffn_w = (ffn_w1.astype(BF16), ffn_w3.astype(BF16), ffn_w2.astype(BF16))
    rope = _rope_tables(st)
    seg = _segment_ones()
    bf = lambda w: w.astype(BF16)

    h = (ctx, x)
    v_first = None
    for i in range(depth):
        kind, j = i % 3, i // 3
        g, mod = norm_g[i], mods[i]
        h = _ffn_half(st, h, mod, g, ffn_w, i, 0)
        mixer = None
        if kind == 0:
            p = {"mu": rwkv_mu[j], "wr": bf(rwkv_wr[j]), "wk": bf(rwkv_wk[j]), "wv": bf(rwkv_wv[j]),
                 "wo": bf(rwkv_wo[j]), "k_k": rwkv_k_k[j][None], "k_a": rwkv_k_a[j][None],
                 "r_k": rwkv_r_k[j].reshape(1, d),
                 "w0": rwkv_w0[j][:, None], "w1": bf(_pad_to(rwkv_w1[j], 2, LANES)), "w2": bf(_pad_to(rwkv_w2[j], 1, LANES)),
                 "a0": rwkv_a0[j][:, None], "a1": bf(_pad_to(rwkv_a1[j], 2, LANES)), "a2": bf(_pad_to(rwkv_a2[j], 1, LANES)),
                 "g1": bf(_pad_to(rwkv_g1[j], 2, LANES)), "g2": bf(_pad_to(rwkv_g2[j], 1, LANES)),
                 "ln_w": rwkv_ln_w[j][:, None], "ln_b": rwkv_ln_b[j][:, None]}
            vres = None
            if j > 0:
                vres = (v_first, rwkv_v0[j - 1][None], bf(_pad_to(rwkv_v1[j - 1], 1, LANES)),
                        bf(_pad_to(rwkv_v2[j - 1], 0, LANES)))
            r, v, kk, lw, kd, ab, gate = _rwkv_features(st, h, mod, g, p, seg, vres)
            if j == 0:
                v_first = v
            y_f = _rwkv_scan(st, r, lw, kd, v, kk, ab, 0)
            y_b = _rwkv_scan(st, r, lw, kd, v, kk, ab, 1)
            h = _rwkv_out(st, h, mod, g, y_f, y_b, r, kd, v, gate, p, seg)
        else:
            if kind == 1:
                a = _attn_params(gattn_wq[j], gattn_wk[j], gattn_wv[j], gattn_wo[j],
                                 gattn_q_norm[j], gattn_k_norm[j])
                sink = None
            else:
                ones = jnp.ones((HEAD_DIM,), F32)
                a = _attn_params(wattn_wq[j], wattn_wk[j], wattn_wv[j], wattn_wo[j], ones, ones)
                sink = wattn_sink[j]
            q, k, v = _qkv_project(st, h, mod, g, a, rope, seg, qk_norm=kind == 1)
            mixer = (_attention(st, q, k, v, sink), a["wo"])
        h = _ffn_half(st, h, mod, g, ffn_w, i, 1, latent_only=i == depth - 1, mixer=mixer)
    return h
```

```python
import functools

import numpy as np
import jax
import jax.numpy as jnp
from jax import lax
from jax.experimental import pallas as pl
from jax.experimental.pallas import tpu as pltpu

F32 = jnp.float32
BF16 = jnp.bfloat16
HIGHEST = lax.Precision.HIGHEST

GRID_W = 64
N_MOD = 9
HEAD_DIM = 64
WINDOW = 128
ROPE_THETA = 10000.0
N_FREQ = HEAD_DIM // 4
RWKV_HEAD = 64
NORM_EPS = 1e-6
GN_EPS = 64e-5
L2_EPS = 1e-12
NEG_INF = -1e30
LOG2_E = 1.4426950408889634
EXP_NEG_HALF = 0.6065306597126334

LANES = 128
SUBLANES = 8
TILE = 256
ROW_SEQS = 2
FFN_SEQS = 2
FFN_CHAIN_SEQS = 1
FFN_FF_CHUNKS = 1
SCAN_SEQS = 4
ATTN_KV_HEADS = 2
CHUNK = 64
MAX_KEYS = 1152
SEG_TILE = 256
LOCAL_KEYS = 3 * TILE
VMEM_LIMIT = 56 * 1024 * 1024

assert RWKV_HEAD == CHUNK and 2 * CHUNK == LANES and HEAD_DIM == RWKV_HEAD and SEG_TILE % RWKV_HEAD == 0
CHUNK_SHIFT = CHUNK.bit_length() - 1


def _cparams(*sem):
    return pltpu.CompilerParams(dimension_semantics=sem, vmem_limit_bytes=VMEM_LIMIT)


def _const_spec(shape):
    nd = len(shape)
    return pl.BlockSpec(shape, lambda *_: (0,) * nd, pipeline_mode=pl.Buffered(1))


def _dot(a, b):
    return jnp.dot(a.astype(BF16), b.astype(BF16), preferred_element_type=F32)


def _dot_nt(a, b):
    return lax.dot_general(a.astype(BF16), b.astype(BF16), (((1,), (1,)), ((), ())),
                           preferred_element_type=F32)


def _dot_tn(a, b):
    return lax.dot_general(a.astype(BF16), b.astype(BF16), (((0,), (0,)), ((), ())),
                           preferred_element_type=F32)


def _dot_f32(a, b):
    return jnp.dot(a, b, precision=HIGHEST, preferred_element_type=F32)


def _sigmoid(x):
    return 1.0 / (1.0 + jnp.exp2(x * -LOG2_E))


def _rms(x, g):
    return x * lax.rsqrt(jnp.mean(x * x, axis=-1, keepdims=True) + NORM_EPS) * g


def _norm_mod(x, g, scale, shift):
    return _rms(x, g) * (1.0 + scale) + shift


def _seg_sum(x, ones_ref, pieces):
    ones = ones_ref[...]
    parts, rest = [], x
    for i in range(pieces):
        part = rest.astype(BF16)
        parts.append(part)
        if i + 1 < pieces:
            rest = rest - part.astype(F32)
    cols = []
    for j in range(x.shape[1] // SEG_TILE):
        sl = slice(j * SEG_TILE, (j + 1) * SEG_TILE)
        cols.append(sum(jnp.dot(part[:, sl], ones, preferred_element_type=F32) for part in parts))
    return jnp.concatenate(cols, axis=1)


def _mod_body(c_ref, w_ref, b_ref, o_ref):
    c = c_ref[...]
    o_ref[0] = _dot_f32(c * _sigmoid(c), w_ref[0]) + b_ref[0]


def _modulation(c_rows, mod_w, mod_b):
    depth, d, nd = mod_w.shape
    rows = c_rows.shape[0]
    tn = nd // 6
    return pl.pallas_call(
        _mod_body,
        grid=(depth, nd // tn),
        in_specs=[_const_spec((rows, d)),
                  pl.BlockSpec((1, d, tn), lambda i, j: (i, 0, j)),
                  pl.BlockSpec((1, 1, tn), lambda i, j: (i, 0, j))],
        out_specs=pl.BlockSpec((1, rows, tn), lambda i, j: (i, 0, j)),
        out_shape=jax.ShapeDtypeStruct((depth, rows, nd), F32),
        compiler_params=_cparams("arbitrary", "arbitrary"),
        name="adaln_rows",
    )(c_rows, mod_w, mod_b.reshape(depth, 1, nd))


class _Stream:
    def __init__(self, batch, ctx_len, seq, d):
        assert ctx_len % TILE == 0 and seq % TILE == 0 and d % LANES == 0
        self.b, self.t, self.d = batch, ctx_len + seq, d
        self.ctx = ctx_len
        self.n_tiles = self.t // TILE
        self.n_ctx_tiles = ctx_len // TILE

        self.seqs = ROW_SEQS if batch % ROW_SEQS == 0 else 1

    def tile_spec(self, width=None, seqs=1):
        return pl.BlockSpec((seqs, TILE, width or self.d), lambda b, t: (b, t, 0))

    def mod_spec(self, seqs=1):
        nct, ctx_blk = self.n_ctx_tiles, self.b // seqs
        return pl.BlockSpec((seqs, N_MOD, self.d), lambda b, t: (jnp.where(t < nct, ctx_blk, b), 0, 0))

    def out_shape(self, width=None, dtype=F32):
        return jax.ShapeDtypeStruct((self.b, self.t, width or self.d), dtype)


def _ffn_body(*refs, mod_row, g_row, with_mixer, n_ctx_tiles, split_input):
    refs = list(refs)
    if split_input:
        ctx_ref, lat_ref = refs[:2]
        x = jnp.where(pl.program_id(1) < n_ctx_tiles, ctx_ref[...], lat_ref[...])
        refs = refs[2:]
    else:
        x = refs.pop(0)[...]
    if with_mixer:
        mod_ref, g_ref, w1_ref, w3_ref, w2_ref, o_ref, wo_ref, out_ref = refs
    else:
        mod_ref, g_ref, w1_ref, w3_ref, w2_ref, out_ref = refs
    m = mod_ref[...]
    g = g_ref[...]
    d = x.shape[-1]
    per = min(FFN_CHAIN_SEQS, x.shape[0])
    chains = [slice(q, q + per) for q in range(0, x.shape[0], per)]
    xs, ms = [x[c] for c in chains], [m[c] for c in chains]
    if with_mixer:
        mixed = [jnp.dot(o_ref[c].reshape(-1, o_ref.shape[-1]), wo_ref[...], preferred_element_type=F32)
                 for c in chains]
        xs = [xc + mc[:, 5:6] * _rms(mx.reshape(xc.shape), g[3:4]) for xc, mc, mx in zip(xs, ms, mixed)]
    ns = [_norm_mod(xc, g[g_row:g_row + 1], mc[:, mod_row + 1:mod_row + 2], mc[:, mod_row:mod_row + 1])
          .astype(BF16).reshape(-1, d) for xc, mc in zip(xs, ms)]
    f = w1_ref.shape[-1]
    ys = [None] * len(chains)
    for f0 in range(0, f, f // FFN_FF_CHUNKS):
        cols = slice(f0, f0 + f // FFN_FF_CHUNKS)
        a = [jnp.dot(n, w1_ref[0, 0, :, cols], preferred_element_type=F32) for n in ns]
        b = [jnp.dot(n, w3_ref[0, 0, :, cols], preferred_element_type=F32) for n in ns]
        hs = [(ac * _sigmoid(ac) * bc).astype(BF16) for ac, bc in zip(a, b)]
        part = [jnp.dot(h, w2_ref[0, 0, cols, :], preferred_element_type=F32) for h in hs]
        ys = [p if y is None else y + p for y, p in zip(ys, part)]
    for c, xc, mc, y in zip(chains, xs, ms, ys):
        out_ref[c] = xc + 0.5 * mc[:, mod_row + 2:mod_row + 3] * _rms(y.reshape(xc.shape), g[g_row + 1:g_row + 2])


def _ffn_half(st, x, mod, g, weights, layer, half, latent_only=False, mixer=None):
    w1, w3, w2 = weights
    d, f = w1.shape[-2:]
    nb = FFN_SEQS if st.b % FFN_SEQS == 0 else st.seqs
    nct, ctx_blk = st.n_ctx_tiles, st.b // nb
    skip = nct if latent_only else 0
    split_input = isinstance(x, tuple)

    def rows_spec(width):
        return pl.BlockSpec((nb, TILE, width), lambda b, t: (b, t + skip, 0))

    def weight_spec(w):
        return pl.BlockSpec((1, 1) + w.shape[-2:], lambda b, t: (layer, half, 0, 0), pipeline_mode=pl.Buffered(1))

    if split_input:
        specs = [pl.BlockSpec((nb, TILE, d), lambda b, t: (b, jnp.minimum(t, nct - 1), 0)),
                 pl.BlockSpec((nb, TILE, d), lambda b, t: (b, jnp.maximum(t - nct, 0), 0))]
        args = list(x)
    else:
        specs, args = [rows_spec(d)], [x]
    mod_spec = pl.BlockSpec((nb, N_MOD, d), lambda b, t: (jnp.where(t + skip < nct, ctx_blk, b), 0, 0))
    specs += [mod_spec, _const_spec(g.shape), weight_spec(w1), weight_spec(w3), weight_spec(w2)]
    args += [mod, g, w1, w3, w2]
    if mixer is not None:
        o, wo = mixer
        specs += [rows_spec(o.shape[-1]), _const_spec(wo.shape)]
        args += [o, wo]
    return pl.pallas_call(
        functools.partial(_ffn_body, mod_row=6 * half, g_row=4 * half, with_mixer=mixer is not None,
                          n_ctx_tiles=nct, split_input=split_input),
        grid=(st.b // nb, st.n_tiles - skip),
        in_specs=specs,
        out_specs=pl.BlockSpec((nb, TILE, d), lambda b, t: (b, t, 0)),
        out_shape=jax.ShapeDtypeStruct((st.b, st.t - skip * TILE, d), F32),
        compiler_params=_cparams("parallel", "parallel"),
        name="swiglu_half",
    )(*args)


def _rwkv_feat_body(*refs, n_ctx_tiles, n_tiles, with_vres):
    (x_ref, xp_ref, xn_ref, mod_ref, g_ref, mu_ref, wr_ref, wk_ref, wv_ref, kk_ref, ka_ref,
     w0_ref, w1_ref, w2_ref, a0_ref, a1_ref, a2_ref, g1_ref, g2_ref, ones_ref) = refs[:20]
    rest = refs[20:]
    if with_vres:
        vf_ref, v0_ref, v1_ref, v2_ref = rest[:4]
        rest = rest[4:]
    r_out, v_out, kk_out, lw_out, kd_out, ab_out, g_out = rest

    t = pl.program_id(1)
    m = mod_ref[0]
    gn = g_ref[...][2:3]
    shift, scale = m[3:4], m[4:5]
    n = _norm_mod(x_ref[0], gn, scale, shift)
    first = jnp.logical_or(t == 0, t == n_ctx_tiles)
    last = jnp.logical_or(t == n_ctx_tiles - 1, t == n_tiles - 1)
    n_before = jnp.where(first, 0.0, _norm_mod(xp_ref[0][7:8], gn, scale, shift))
    n_after = jnp.where(last, 0.0, _norm_mod(xn_ref[0][0:1], gn, scale, shift))
    row = lax.broadcasted_iota(jnp.int32, (TILE, 1), 0)
    n_prev = jnp.where(row == 0, n_before, pltpu.roll(n, 1, 0))
    n_next = jnp.where(row == TILE - 1, n_after, pltpu.roll(n, TILE - 1, 0))
    xx = 0.5 * (n_prev + n_next) - n
    mu = mu_ref[...]
    xr, xw, xk, xv, xa, xg = (n + xx * mu[i:i + 1] for i in range(6))

    r = _dot(xr, wr_ref[...])
    k = _dot(xk, wk_ref[...])
    v = _dot(xv, wv_ref[...])
    if with_vres:
        lora = _dot(_dot(xv, v1_ref[...]), v2_ref[...])
        v = v + (vf_ref[0] - v) * _sigmoid(v0_ref[...] + lora)
    kk = k * kk_ref[...]
    kk = kk / jnp.maximum(jnp.sqrt(_seg_sum(kk * kk, ones_ref, 2)), L2_EPS)
    r_out[0] = r.astype(BF16)
    v_out[0] = v
    kk_out[0] = kk
    k_ka = k * ka_ref[...]
    k_rest = k - k_ka
    for dr in range(2):
        wl = w0_ref[dr] + _dot(jnp.tanh(_dot(xw, w1_ref[dr])), w2_ref[dr])
        a = _sigmoid(a0_ref[dr] + _dot(_dot(xa, a1_ref[dr]), a2_ref[dr]))
        lw_out[dr, 0] = -EXP_NEG_HALF * _sigmoid(wl)
        kd_out[dr, 0] = (k_rest + k_ka * a).astype(BF16)
        ab_out[dr, 0] = kk * a
        g_out[dr, 0] = _dot(_sigmoid(_dot(xg, g1_ref[dr])), g2_ref[dr]).astype(BF16)


def _rwkv_features(st, x, mod, g, p, seg, vres):
    d = st.d
    n8 = st.t // 8
    per8 = TILE // 8
    prev_spec = pl.BlockSpec((1, 8, d), lambda b, t: (b, jnp.maximum(t * per8 - 1, 0), 0))
    next_spec = pl.BlockSpec((1, 8, d), lambda b, t: (b, jnp.minimum((t + 1) * per8, n8 - 1), 0))
    dir_spec = pl.BlockSpec((2, 1, TILE, d), lambda b, t: (0, b, t, 0))
    dir_f32 = jax.ShapeDtypeStruct((2, st.b, st.t, d), F32)
    dir_bf16 = jax.ShapeDtypeStruct((2, st.b, st.t, d), BF16)
    names = ("mu", "wr", "wk", "wv", "k_k", "k_a", "w0", "w1", "w2", "a0", "a1", "a2", "g1", "g2")
    consts = [p[nm] for nm in names] + [seg]
    args = [x, x, x, mod, g] + consts
    specs = [st.tile_spec(), prev_spec, next_spec, st.mod_spec(), _const_spec(g.shape)]
    specs += [_const_spec(c.shape) for c in consts]
    if vres is not None:
        v_first, v0, v1, v2 = vres
        args += [v_first, v0, v1, v2]
        specs += [st.tile_spec(), _const_spec(v0.shape), _const_spec(v1.shape), _const_spec(v2.shape)]
    return pl.pallas_call(
        functools.partial(_rwkv_feat_body, n_ctx_tiles=st.n_ctx_tiles, n_tiles=st.n_tiles,
                          with_vres=vres is not None),
        grid=(st.b, st.n_tiles),
        in_specs=specs,
        out_specs=[st.tile_spec()] * 3 + [dir_spec] * 4,
        out_shape=[st.out_shape(dtype=BF16), st.out_shape(), st.out_shape(), dir_f32, dir_bf16, dir_f32, dir_bf16],
        compiler_params=_cparams("parallel", "parallel"),
        name="rwkv_features",
    )(*args)


def _unit_triangular_inverses(mats, ri, ci, size, upper):
    n = mats[0].shape[0]
    eye = (ri == ci).astype(F32)
    ts = [jnp.where((ri >> 1) == (ci >> 1), a, 0.0) + eye for a in mats]
    level = 1
    while (2 << level) <= size:
        m = 1 << level
        off = jnp.logical_and((ri >> (level + 1)) == (ci >> (level + 1)), (ri >> level) != (ci >> level))
        a_off = [jnp.where(off, a, 0.0) for a in mats]
        if m % SUBLANES == 0:
            moving = range(0 if upper else 1, n // m, 2)

            def rows(x):
                return jnp.concatenate([x[b * m:(b + 1) * m] for b in moving], axis=0)

            zs = [_dot(_dot(rows(t), a), t) for t, a in zip(ts, a_off)]
            ts = [jnp.concatenate([t[b * m:(b + 1) * m] + z[(b // 2) * m:(b // 2 + 1) * m] if b in moving
                                   else t[b * m:(b + 1) * m] for b in range(n // m)], axis=0)
                  for t, z in zip(ts, zs)]
        else:
            ts = [t + _dot(_dot(t, a), t) for t, a in zip(ts, a_off)]
        level += 1
    return ts


def _scan_body(r_ref, lw_ref, k_ref, v_ref, kk_ref, ab_ref, y_ref, s_ref, *, reverse):
    c = CHUNK
    n2 = 2 * c

    @pl.when(pl.program_id(1) == 0)
    def _():
        s_ref[...] = jnp.zeros_like(s_ref)

    row = lax.broadcasted_iota(jnp.int32, (c, 1), 0)

    def running_sum(x):
        step = 1
        while step < c:
            if reverse:
                x = x + jnp.where(row < c - step, pltpu.roll(x, c - step, 0), 0.0)
            else:
                x = x + jnp.where(row >= step, pltpu.roll(x, step, 0), 0.0)
            step *= 2
        return x

    ri = lax.broadcasted_iota(jnp.int32, (n2, n2), 0)
    ci = lax.broadcasted_iota(jnp.int32, (n2, n2), 1)
    same = (ri >> CHUNK_SHIFT) == (ci >> CHUNK_SHIFT)
    tt, ss = ri & (c - 1), ci & (c - 1)
    before = (ss > tt) if reverse else (ss < tt)
    strict = jnp.logical_and(same, before)
    incl = jnp.logical_and(same, jnp.logical_or(before, ss == tt))
    incl2 = jnp.concatenate([incl, incl], axis=1)
    even = lax.broadcasted_iota(jnp.int32, (1, LANES), 1) < RWKV_HEAD

    def stack(x):
        return jnp.concatenate([jnp.where(even, x, 0.0), jnp.where(even, 0.0, x)], axis=0)

    n_pairs = r_ref.shape[-1] // LANES
    chains = [(q, p) for q in range(r_ref.shape[0]) for p in range(n_pairs)]
    cols = {}
    for q in range(r_ref.shape[0]):
        r, k = r_ref[q].astype(F32), k_ref[0, q].astype(F32)
        lw, v, kk, ab = lw_ref[0, q], v_ref[q], kk_ref[q], ab_ref[0, q]
        cum = running_sum(lw)
        tot = cum[0:1] if reverse else cum[c - 1:c]
        inv = jnp.exp(-cum)
        to_end = jnp.exp(tot - cum)
        full = {"a": -kk * jnp.exp(cum - lw), "r": r * jnp.exp(cum), "b": ab * inv, "k": k * inv,
                "be": ab * to_end, "ke": k * to_end, "w": jnp.exp(tot), "v": v}
        for p in range(n_pairs):
            cols[q, p] = {nm: x[:, p * LANES:(p + 1) * LANES] for nm, x in full.items()}

    prods = [_dot_nt(jnp.concatenate([stack(cols[ch]["a"]), stack(cols[ch]["r"])], axis=0),
                     jnp.concatenate([stack(cols[ch]["b"]), stack(cols[ch]["k"])], axis=0)) for ch in chains]
    a_b = [jnp.where(strict, pr[0:n2, 0:n2], 0.0) for pr in prods]
    t_inv = _unit_triangular_inverses(a_b, ri, ci, c, reverse)
    v_st = [stack(cols[ch]["v"]) for ch in chains]
    u_own = [_dot(jnp.where(strict, pr[0:n2, n2:2 * n2], 0.0), vs) for pr, vs in zip(prods, v_st)]
    from_state = [_dot_nt(jnp.concatenate([cols[ch]["a"], cols[ch]["r"]], axis=0), s_ref[ch])
                  for ch in chains]
    u_st = [_dot(t, stack(fs[0:c]) + uo) for t, fs, uo in zip(t_inv, from_state, u_own)]
    y_st = [_dot(jnp.where(incl2, pr[n2:2 * n2], 0.0), jnp.concatenate([us, vs], axis=0))
            for pr, us, vs in zip(prods, u_st, v_st)]
    for i, (q, p) in enumerate(chains):
        y_ref[q, :, p * LANES:(p + 1) * LANES] = (from_state[i][c:2 * c] + y_st[i][0:c] + y_st[i][c:n2]).astype(BF16)
        u = u_st[i][0:c] + u_st[i][c:n2]
        upd = _dot_tn(jnp.concatenate([u, cols[q, p]["v"]], axis=0),
                      jnp.concatenate([cols[q, p]["be"], cols[q, p]["ke"]], axis=0))
        s_ref[q, p] = s_ref[q, p] * cols[q, p]["w"] + jnp.where(same, upd, 0.0)


def _rwkv_scan(st, r, lw, kd, v, kk, ab, direction):
    d = st.d
    nq = SCAN_SEQS if st.b % SCAN_SEQS == 0 else 1
    nc, ncc = st.t // CHUNK, st.ctx // CHUNK
    if direction == 0:
        def cidx(s):
            return s
    else:
        def cidx(s):
            return jnp.where(s < ncc, ncc - 1 - s, nc - 1 - s + ncc)
    shared = pl.BlockSpec((nq, CHUNK, d), lambda b, s: (b, cidx(s), 0))
    per_dir = pl.BlockSpec((1, nq, CHUNK, d), lambda b, s: (direction, b, cidx(s), 0))
    return pl.pallas_call(
        functools.partial(_scan_body, reverse=direction == 1),
        grid=(st.b // nq, nc),
        in_specs=[shared, per_dir, per_dir, shared, shared, per_dir],
        out_specs=shared,
        out_shape=st.out_shape(dtype=BF16),
        scratch_shapes=[pltpu.VMEM((nq, d // LANES, LANES, LANES), F32)],
        compiler_params=_cparams("parallel", "arbitrary"),
        name="rwkv_scan_bwd" if direction else "rwkv_scan_fwd",
    )(r, lw, kd, v, kk, ab)


def _rwkv_out_body(x_ref, mod_ref, g_ref, yf_ref, yb_ref, r_ref, kd_ref, v_ref, gate_ref,
                   lnw_ref, lnb_ref, rk_ref, wo_ref, ones_ref, o_ref):
    x = x_ref[...]
    flat = (-1, x.shape[-1])
    r, v = r_ref[...].astype(F32).reshape(flat), v_ref[...].reshape(flat)
    inv_n = 1.0 / RWKV_HEAD
    o = None
    for dr, y_ref in enumerate((yf_ref, yb_ref)):
        y = y_ref[...].astype(F32).reshape(flat)
        cen = y - _seg_sum(y, ones_ref, 1) * inv_n
        var = _seg_sum(cen * cen, ones_ref, 1) * inv_n
        yn = cen * lax.rsqrt(var + GN_EPS) * lnw_ref[dr] + lnb_ref[dr]
        bonus = _seg_sum(r * kd_ref[dr].astype(F32).reshape(flat) * rk_ref[...], ones_ref, 1) * v
        od = (yn + bonus) * gate_ref[dr].astype(F32).reshape(flat)
        o = od if o is None else o + od
    y_mix = _dot(o, wo_ref[...]).reshape(x.shape)
    o_ref[...] = x + mod_ref[...][:, 5:6] * _rms(y_mix, g_ref[...][3:4])


def _rwkv_out(st, x, mod, g, y_f, y_b, r, kd, v, gate, p, seg):
    d, nb = st.d, st.seqs
    rows = st.tile_spec(seqs=nb)
    dir_spec = pl.BlockSpec((2, nb, TILE, d), lambda b, t: (0, b, t, 0))
    consts = [p["ln_w"], p["ln_b"], p["r_k"], p["wo"]] + [seg]
    return pl.pallas_call(
        _rwkv_out_body,
        grid=(st.b // nb, st.n_tiles),
        in_specs=[rows, st.mod_spec(nb), _const_spec(g.shape), rows, rows, rows, dir_spec, rows, dir_spec]
                 + [_const_spec(c.shape) for c in consts],
        out_specs=rows,
        out_shape=st.out_shape(),
        compiler_params=_cparams("parallel", "parallel"),
        name="rwkv_readout",
    )(x, mod, g, y_f, y_b, r, kd, v, gate, *consts)


def _qkv_body(x_ref, mod_ref, g_ref, wq_ref, wk_ref, wv_ref, gq_ref, gk_ref, ones_ref,
              cos_ref, sin_ref, q_out, k_out, v_out, *, qk_norm, q_scale):
    x, m = x_ref[...], mod_ref[...]
    nb = x.shape[0]
    n = _norm_mod(x, g_ref[...][2:3], m[:, 4:5], m[:, 3:4]).astype(BF16).reshape(-1, x.shape[-1])
    cos = jnp.concatenate([cos_ref[...]] * nb, axis=0)
    sin = jnp.concatenate([sin_ref[...]] * nb, axis=0)
    lane = lax.broadcasted_iota(jnp.int32, (1, LANES), 1)
    second_half = (lane & N_FREQ) != 0
    tail = lane >= HEAD_DIM

    def rotated(w_ref, gain_ref, out_ref, scale):
        t_all = jnp.dot(n, w_ref[...], preferred_element_type=F32)
        if qk_norm:
            mean_sq = _seg_sum(t_all * t_all, ones_ref, 2) * (1.0 / HEAD_DIM)
            t_all = t_all * lax.rsqrt(mean_sq + NORM_EPS) * gain_ref[...]
        for j in range(t_all.shape[1] // LANES):
            t = t_all[:, j * LANES:(j + 1) * LANES]
            partner = jnp.where(second_half, pltpu.roll(t, N_FREQ, 1), pltpu.roll(t, LANES - N_FREQ, 1))
            out = t * cos + partner * sin
            out = (out * scale if scale != 1.0 else out).astype(BF16)
            for q in range(nb):
                out_ref[q, 2 * j] = out[q * TILE:(q + 1) * TILE, 0:HEAD_DIM]
                out_ref[q, 2 * j + 1] = out[q * TILE:(q + 1) * TILE, HEAD_DIM:LANES]

    rotated(wq_ref, gq_ref, q_out, q_scale)
    rotated(wk_ref, gk_ref, k_out, 1.0)
    v_all = jnp.dot(n, wv_ref[...], preferred_element_type=F32)
    for j in range(v_all.shape[1] // LANES):
        v2 = v_all[:, j * LANES:(j + 1) * LANES]
        first = jnp.where(tail, 1.0, v2).astype(BF16)
        second = jnp.where(tail, 1.0, pltpu.roll(v2, HEAD_DIM, 1)).astype(BF16)
        for q in range(nb):
            v_out[q, 2 * j] = first[q * TILE:(q + 1) * TILE]
            v_out[q, 2 * j + 1] = second[q * TILE:(q + 1) * TILE]


def _qkv_project(st, x, mod, g, a, rope, seg, qk_norm):
    nh, nkv = a["wq"].shape[1] // HEAD_DIM, a["wk"].shape[1] // HEAD_DIM
    consts = [a["wq"], a["wk"], a["wv"], a["gq"], a["gk"], seg]
    rope_spec = pl.BlockSpec((TILE, LANES), lambda b, t: (t, 0))

    nb = st.seqs

    def head_spec(n, width):
        return pl.BlockSpec((nb, n, TILE, width), lambda b, t: (b, 0, t, 0))

    def head_shape(n, width):
        return jax.ShapeDtypeStruct((st.b, n, st.t, width), BF16)

    return pl.pallas_call(
        functools.partial(_qkv_body, qk_norm=qk_norm, q_scale=HEAD_DIM ** -0.5 * LOG2_E),
        grid=(st.b // nb, st.n_tiles),
        in_specs=[st.tile_spec(seqs=nb), st.mod_spec(nb), _const_spec(g.shape)]
                 + [_const_spec(c.shape) for c in consts] + [rope_spec, rope_spec],
        out_specs=[head_spec(nh, HEAD_DIM), head_spec(nkv, HEAD_DIM), head_spec(nkv, LANES)],
        out_shape=[head_shape(nh, HEAD_DIM), head_shape(nkv, HEAD_DIM), head_shape(nkv, LANES)],
        compiler_params=_cparams("parallel", "parallel"),
        name="qkv_rope",
    )(x, mod, g, *consts, *rope)


def _key_chunks(k0, k1):
    n = -(-(k1 - k0) // MAX_KEYS)
    size = -(-(k1 - k0) // (n * LANES)) * LANES
    return [(a, min(size, k1 - a)) for a in range(k0, k1, size)]


def _attn_body(*refs, ctx_len, n_ctx_tiles, total, windowed):
    if windowed:
        sink_ref, q_ref, k_ref, v_ref, o_ref = refs
    else:
        q_ref, k_ref, v_ref, o_ref = refs
    heads = q_ref.shape[1]
    group = heads // k_ref.shape[1]
    t = pl.program_id(2)
    tail = lax.broadcasted_iota(jnp.int32, (1, LANES), 1) >= HEAD_DIM
    chains = range(heads)
    qs = [q_ref[0, c] for c in chains]

    def scores(k0, size):
        return [_dot_nt(q, k_ref[0, c // group, pl.ds(k0, size), :]) for c, q in zip(chains, qs)]

    def weigh(ps, k0, size):
        return [jnp.dot(p.astype(BF16), v_ref[0, c // group, pl.ds(k0, size), :], preferred_element_type=F32)
                for c, p in zip(chains, ps)]

    def finish(accs):
        invs = [1.0 / jnp.where(tail, acc, 1.0) for acc in accs]
        cols = []
        for c in range(0, heads, 2):
            first = accs[c] * pltpu.roll(invs[c], HEAD_DIM, 1)
            second = pltpu.roll(accs[c + 1], HEAD_DIM, 1) * invs[c + 1]
            cols.append(jnp.where(tail, second, first))
        return jnp.concatenate(cols, axis=1)

    def row_max(xs):
        return [jnp.max(x, axis=-1, keepdims=True) for x in xs]

    if windowed:
        sinks = [jnp.full((TILE, 1), sink_ref[pl.program_id(1) * heads + c], F32) * LOG2_E for c in chains]

        def with_sink(s_parts, spans):
            ms = sinks
            for part in s_parts:
                ms = [jnp.maximum(m, pm) for m, pm in zip(ms, row_max(part))]
            accs = [jnp.where(tail, jnp.exp2(sk - m), 0.0) for sk, m in zip(sinks, ms)]
            for part, (k0, size) in zip(s_parts, spans):
                pvs = weigh([jnp.exp2(s - m) for s, m in zip(part, ms)], k0, size)
                accs = [acc + pv for acc, pv in zip(accs, pvs)]
            return finish(accs)

        def context_queries():
            return with_sink([scores(0, ctx_len)], [(0, ctx_len)])

        def latent_queries():
            k0 = pl.multiple_of(jnp.clip((t - 1) * TILE, 0, total - LOCAL_KEYS), TILE)
            k_pos = k0 + lax.broadcasted_iota(jnp.int32, (TILE, LOCAL_KEYS), 1)
            q_pos = t * TILE + lax.broadcasted_iota(jnp.int32, (TILE, LOCAL_KEYS), 0)
            mask = jnp.logical_and(jnp.abs(k_pos - q_pos) <= WINDOW, k_pos >= ctx_len)
            s_loc = [jnp.where(mask, s, NEG_INF) for s in scores(k0, LOCAL_KEYS)]
            return with_sink([scores(0, ctx_len), s_loc], [(0, ctx_len), (k0, LOCAL_KEYS)])
    else:
        def online(spans):
            ms = accs = None
            for k0, size in spans:
                ss = scores(k0, size)
                m_chunk = row_max(ss)
                if ms is None:
                    ms, accs = m_chunk, weigh([jnp.exp2(s - m) for s, m in zip(ss, m_chunk)], k0, size)
                else:
                    m_new = [jnp.maximum(m, mc) for m, mc in zip(ms, m_chunk)]
                    pvs = weigh([jnp.exp2(s - m) for s, m in zip(ss, m_new)], k0, size)
                    accs = [jnp.exp2(m - mn) * acc + pv for m, mn, acc, pv in zip(ms, m_new, accs, pvs)]
                    ms = m_new
            return finish(accs)

        def context_queries():
            return online([(0, ctx_len)])

        def latent_queries():
            return online(_key_chunks(0, total))

    o_ref[0] = lax.cond(t < n_ctx_tiles, context_queries, latent_queries).astype(BF16)


def _attention(st, q, k, v, sink):
    nh, nkv = q.shape[1], k.shape[1]
    group = nh // nkv
    windowed = sink is not None
    assert st.t >= LOCAL_KEYS
    kvs = ATTN_KV_HEADS if nkv % ATTN_KV_HEADS == 0 else 1
    q_spec = pl.BlockSpec((1, kvs * group, TILE, HEAD_DIM), lambda b, h, t: (b, h, t, 0))
    k_spec = pl.BlockSpec((1, kvs, st.t, HEAD_DIM), lambda b, h, t: (b, h, 0, 0))
    v_spec = pl.BlockSpec((1, kvs, st.t, LANES), lambda b, h, t: (b, h, 0, 0))
    specs, args = [q_spec, k_spec, v_spec], [q, k, v]
    if windowed:
        specs = [pl.BlockSpec(memory_space=pltpu.SMEM)] + specs
        args = [sink] + args
    return pl.pallas_call(
        functools.partial(_attn_body, ctx_len=st.ctx, n_ctx_tiles=st.n_ctx_tiles, total=st.t,
                          windowed=windowed),
        grid=(st.b, nkv // kvs, st.n_tiles),
        in_specs=specs,
        out_specs=pl.BlockSpec((1, TILE, kvs * group * HEAD_DIM), lambda b, h, t: (b, t, h)),
        out_shape=jax.ShapeDtypeStruct((st.b, st.t, nh * HEAD_DIM), BF16),
        compiler_params=_cparams("parallel", "parallel", "parallel"),
        name="window_attention" if windowed else "global_attention",
    )(*args)


def _pad_to(w, axis, mult):
    pad = (-w.shape[axis]) % mult
    if pad == 0:
        return w
    widths = [(0, 0)] * w.ndim
    widths[axis] = (0, pad)
    return jnp.pad(w, widths)


def _rope_tables(st):
    rows = (st.t - st.ctx) // GRID_W
    row = jnp.broadcast_to(jnp.arange(rows)[:, None], (rows, GRID_W)).reshape(-1)
    col = jnp.broadcast_to(jnp.arange(GRID_W)[None, :], (rows, GRID_W)).reshape(-1)
    inv_freq = ROPE_THETA ** (-jnp.arange(N_FREQ, dtype=F32) / N_FREQ)
    ang = jnp.stack([row, col], axis=-1).astype(F32)[:, :, None] * inv_freq
    cos, sin = jnp.cos(ang), jnp.sin(ang)
    cos_t = jnp.stack([cos, cos], axis=2).reshape(-1, HEAD_DIM)
    sin_t = jnp.stack([-sin, sin], axis=2).reshape(-1, HEAD_DIM)
    cos_t = jnp.concatenate([jnp.ones((st.ctx, HEAD_DIM), F32), cos_t], axis=0)
    sin_t = jnp.concatenate([jnp.zeros((st.ctx, HEAD_DIM), F32), sin_t], axis=0)
    return jnp.tile(cos_t, (1, LANES // HEAD_DIM)), jnp.tile(sin_t, (1, LANES // HEAD_DIM))


def _attn_params(wq, wk, wv, wo, gq, gk):
    nh, nkv = wq.shape[1] // HEAD_DIM, wk.shape[1] // HEAD_DIM
    return {"wq": wq.astype(BF16), "wk": wk.astype(BF16), "wv": wv.astype(BF16), "wo": wo.astype(BF16),
            "gq": jnp.tile(gq, nh)[None], "gk": jnp.tile(gk, nkv)[None]}


def _segment_ones():
    head = np.arange(SEG_TILE) // RWKV_HEAD
    return jnp.asarray(head[:, None] == head[None, :], dtype=BF16)


def kernel(x, c, ctx, c_ctx, mod_w, mod_b, norm_g, ffn_w1, ffn_w3, ffn_w2, rwkv_mu, rwkv_wr, rwkv_wk, rwkv_wv, rwkv_wo, rwkv_k_k, rwkv_k_a, rwkv_r_k, rwkv_w0, rwkv_w1, rwkv_w2, rwkv_a0, rwkv_a1, rwkv_a2, rwkv_g1, rwkv_g2, rwkv_ln_w, rwkv_ln_b, rwkv_v0, rwkv_v1, rwkv_v2, gattn_wq, gattn_wk, gattn_wv, gattn_wo, gattn_q_norm, gattn_k_norm, wattn_wq, wattn_wk, wattn_wv, wattn_wo, wattn_sink):
    batch, seq, d = x.shape
    depth = mod_w.shape[0]
    st = _Stream(batch, ctx.shape[1], seq, d)

    ctx_rows = max(ROW_SEQS, FFN_SEQS)
    rows = -(-(batch + ctx_rows) // SUBLANES) * SUBLANES
    c_rows = jnp.zeros((rows, d), F32).at[:batch].set(c).at[batch:batch + ctx_rows].set(c_ctx)
    mods = _modulation(c_rows, mod_w, mod_b).reshape(depth, rows, N_MOD, d)

    ffn_w = (ffn_w1.astype(BF16), ffn_w3.astype(BF16), ffn_w2.astype(BF16))
    rope = _rope_tables(st)
    seg = _segment_ones()
    bf = lambda w: w.astype(BF16)

    h = (ctx, x)
    v_first = None
    for i in range(depth):
        kind, j = i % 3, i // 3
        g, mod = norm_g[i], mods[i]
        h = _ffn_half(st, h, mod, g, ffn_w, i, 0)
        mixer = None
        if kind == 0:
            p = {"mu": rwkv_mu[j], "wr": bf(rwkv_wr[j]), "wk": bf(rwkv_wk[j]), "wv": bf(rwkv_wv[j]),
                 "wo": bf(rwkv_wo[j]), "k_k": rwkv_k_k[j][None], "k_a": rwkv_k_a[j][None],
                 "r_k": rwkv_r_k[j].reshape(1, d),
                 "w0": rwkv_w0[j][:, None], "w1": bf(_pad_to(rwkv_w1[j], 2, LANES)), "w2": bf(_pad_to(rwkv_w2[j], 1, LANES)),
                 "a0": rwkv_a0[j][:, None], "a1": bf(_pad_to(rwkv_a1[j], 2, LANES)), "a2": bf(_pad_to(rwkv_a2[j], 1, LANES)),
                 "g1": bf(_pad_to(rwkv_g1[j], 2, LANES)), "g2": bf(_pad_to(rwkv_g2[j], 1, LANES)),
                 "ln_w": rwkv_ln_w[j][:, None], "ln_b": rwkv_ln_b[j][:, None]}
            vres = None
            if j > 0:
                vres = (v_first, rwkv_v0[j - 1][None], bf(_pad_to(rwkv_v1[j - 1], 1, LANES)),
                        bf(_pad_to(rwkv_v2[j - 1], 0, LANES)))
            r, v, kk, lw, kd, ab, gate = _rwkv_features(st, h, mod, g, p, seg, vres)
            if j == 0:
                v_first = v
            y_f = _rwkv_scan(st, r, lw, kd, v, kk, ab, 0)
            y_b = _rwkv_scan(st, r, lw, kd, v, kk, ab, 1)
            h = _rwkv_out(st, h, mod, g, y_f, y_b, r, kd, v, gate, p, seg)
        else:
            if kind == 1:
                a = _attn_params(gattn_wq[j], gattn_wk[j], gattn_wv[j], gattn_wo[j],
                                 gattn_q_norm[j], gattn_k_norm[j])
                sink = None
            else:
                ones = jnp.ones((HEAD_DIM,), F32)
                a = _attn_params(wattn_wq[j], wattn_wk[j], wattn_wv[j], wattn_wo[j], ones, ones)
                sink = wattn_sink[j]
            q, k, v = _qkv_project(st, h, mod, g, a, rope, seg, qk_norm=kind == 1)
            mixer = (_attention(st, q, k, v, sink), a["wo"])
        h = _ffn_half(st, h, mod, g, ffn_w, i, 1, latent_only=i == depth - 1, mixer=mixer)
    return h
```

```python
import functools

import numpy as np
import jax
import jax.numpy as jnp
from jax import lax
from jax.experimental import pallas as pl
from jax.experimental.pallas import tpu as pltpu

F32 = jnp.float32
BF16 = jnp.bfloat16
HIGHEST = lax.Precision.HIGHEST

GRID_W = 64
N_MOD = 9
HEAD_DIM = 64
WINDOW = 128
ROPE_THETA = 10000.0
N_FREQ = HEAD_DIM // 4
RWKV_HEAD = 64
NORM_EPS = 1e-6
GN_EPS = 64e-5
L2_EPS = 1e-12
NEG_INF = -1e30
LOG2_E = 1.4426950408889634
EXP_NEG_HALF = 0.6065306597126334

LANES = 128
SUBLANES = 8
TILE = 256
ROW_SEQS = 2
FFN_SEQS = 2
FFN_CHAIN_SEQS = 1
FFN_FF_CHUNKS = 1
SCAN_SEQS = 4
ATTN_KV_HEADS = 2
WINDOW_KV_HEADS = 4
CHUNK = 64
MAX_KEYS = 4352
SEG_TILE = 256
LOCAL_KEYS = 3 * TILE
VMEM_LIMIT = 56 * 1024 * 1024

assert RWKV_HEAD == CHUNK and 2 * CHUNK == LANES and HEAD_DIM == RWKV_HEAD and SEG_TILE % RWKV_HEAD == 0
CHUNK_SHIFT = CHUNK.bit_length() - 1


def _cparams(*sem):
    return pltpu.CompilerParams(dimension_semantics=sem, vmem_limit_bytes=VMEM_LIMIT)


def _const_spec(shape):
    nd = len(shape)
    return pl.BlockSpec(shape, lambda *_: (0,) * nd, pipeline_mode=pl.Buffered(1))


def _dot(a, b):
    return jnp.dot(a.astype(BF16), b.astype(BF16), preferred_element_type=F32)


def _dot_nt(a, b):
    return lax.dot_general(a.astype(BF16), b.astype(BF16), (((1,), (1,)), ((), ())),
                           preferred_element_type=F32)


def _dot_tn(a, b):
    return lax.dot_general(a.astype(BF16), b.astype(BF16), (((0,), (0,)), ((), ())),
                           preferred_element_type=F32)


def _dot_f32(a, b):
    return jnp.dot(a, b, precision=HIGHEST, preferred_element_type=F32)


def _sigmoid(x):
    return 1.0 / (1.0 + jnp.exp2(x * -LOG2_E))


def _rms(x, g):
    return x * lax.rsqrt(jnp.mean(x * x, axis=-1, keepdims=True) + NORM_EPS) * g


def _norm_mod(x, g, scale, shift):
    return _rms(x, g) * (1.0 + scale) + shift


def _seg_sum(x, ones_ref, pieces):
    ones = ones_ref[...]
    parts, rest = [], x
    for i in range(pieces):
        part = rest.astype(BF16)
        parts.append(part)
        if i + 1 < pieces:
            rest = rest - part.astype(F32)
    cols = []
    for j in range(x.shape[1] // SEG_TILE):
        sl = slice(j * SEG_TILE, (j + 1) * SEG_TILE)
        cols.append(sum(jnp.dot(part[:, sl], ones, preferred_element_type=F32) for part in parts))
    return jnp.concatenate(cols, axis=1)


def _mod_body(c_ref, w_ref, b_ref, o_ref):
    c = c_ref[...]
    o_ref[0] = _dot_f32(c * _sigmoid(c), w_ref[0]) + b_ref[0]


def _modulation(c_rows, mod_w, mod_b):
    depth, d, nd = mod_w.shape
    rows = c_rows.shape[0]
    tn = nd // 6
    return pl.pallas_call(
        _mod_body,
        grid=(depth, nd // tn),
        in_specs=[_const_spec((rows, d)),
                  pl.BlockSpec((1, d, tn), lambda i, j: (i, 0, j)),
                  pl.BlockSpec((1, 1, tn), lambda i, j: (i, 0, j))],
        out_specs=pl.BlockSpec((1, rows, tn), lambda i, j: (i, 0, j)),
        out_shape=jax.ShapeDtypeStruct((depth, rows, nd), F32),
        compiler_params=_cparams("arbitrary", "arbitrary"),
        name="adaln_rows",
    )(c_rows, mod_w, mod_b.reshape(depth, 1, nd))


class _Stream:
    def __init__(self, batch, ctx_len, seq, d):
        assert ctx_len % TILE == 0 and seq % TILE == 0 and d % LANES == 0
        self.b, self.t, self.d = batch, ctx_len + seq, d
        self.ctx = ctx_len
        self.n_tiles = self.t // TILE
        self.n_ctx_tiles = ctx_len // TILE

        self.seqs = ROW_SEQS if batch % ROW_SEQS == 0 else 1

    def tile_spec(self, width=None, seqs=1):
        return pl.BlockSpec((seqs, TILE, width or self.d), lambda b, t: (b, t, 0))

    def mod_spec(self, seqs=1):
        nct, ctx_blk = self.n_ctx_tiles, self.b // seqs
        return pl.BlockSpec((seqs, N_MOD, self.d), lambda b, t: (jnp.where(t < nct, ctx_blk, b), 0, 0))

    def out_shape(self, width=None, dtype=F32):
        return jax.ShapeDtypeStruct((self.b, self.t, width or self.d), dtype)


def _ffn_body(*refs, mod_row, g_row, with_mixer, n_ctx_tiles, split_input):
    refs = list(refs)
    if split_input:
        ctx_ref, lat_ref = refs[:2]
        x = jnp.where(pl.program_id(1) < n_ctx_tiles, ctx_ref[...], lat_ref[...])
        refs = refs[2:]
    else:
        x = refs.pop(0)[...]
    if with_mixer:
        mod_ref, g_ref, w1_ref, w3_ref, w2_ref, o_ref, wo_ref, out_ref = refs
    else:
        mod_ref, g_ref, w1_ref, w3_ref, w2_ref, out_ref = refs
    m = mod_ref[...]
    g = g_ref[...]
    d = x.shape[-1]
    per = min(FFN_CHAIN_SEQS, x.shape[0])
    chains = [slice(q, q + per) for q in range(0, x.shape[0], per)]
    xs, ms = [x[c] for c in chains], [m[c] for c in chains]
    if with_mixer:
        mixed = [jnp.dot(o_ref[c].reshape(-1, o_ref.shape[-1]), wo_ref[...], preferred_element_type=F32)
                 for c in chains]
        xs = [xc + mc[:, 5:6] * _rms(mx.reshape(xc.shape), g[3:4]) for xc, mc, mx in zip(xs, ms, mixed)]
    ns = [_norm_mod(xc, g[g_row:g_row + 1], mc[:, mod_row + 1:mod_row + 2], mc[:, mod_row:mod_row + 1])
          .astype(BF16).reshape(-1, d) for xc, mc in zip(xs, ms)]
    f = w1_ref.shape[-1]
    ys = [None] * len(chains)
    for f0 in range(0, f, f // FFN_FF_CHUNKS):
        cols = slice(f0, f0 + f // FFN_FF_CHUNKS)
        a = [jnp.dot(n, w1_ref[0, 0, :, cols], preferred_element_type=F32) for n in ns]
        b = [jnp.dot(n, w3_ref[0, 0, :, cols], preferred_element_type=F32) for n in ns]
        hs = [(ac * _sigmoid(ac) * bc).astype(BF16) for ac, bc in zip(a, b)]
        part = [jnp.dot(h, w2_ref[0, 0, cols, :], preferred_element_type=F32) for h in hs]
        ys = [p if y is None else y + p for y, p in zip(ys, part)]
    for c, xc, mc, y in zip(chains, xs, ms, ys):
        out_ref[c] = xc + 0.5 * mc[:, mod_row + 2:mod_row + 3] * _rms(y.reshape(xc.shape), g[g_row + 1:g_row + 2])


def _ffn_half(st, x, mod, g, weights, layer, half, latent_only=False, mixer=None):
    w1, w3, w2 = weights
    d, f = w1.shape[-2:]
    nb = FFN_SEQS if st.b % FFN_SEQS == 0 else st.seqs
    nct, ctx_blk = st.n_ctx_tiles, st.b // nb
    skip = nct if latent_only else 0
    split_input = isinstance(x, tuple)

    def rows_spec(width):
        return pl.BlockSpec((nb, TILE, width), lambda b, t: (b, t + skip, 0))

    def weight_spec(w):
        return pl.BlockSpec((1, 1) + w.shape[-2:], lambda b, t: (layer, half, 0, 0), pipeline_mode=pl.Buffered(1))

    if split_input:
        specs = [pl.BlockSpec((nb, TILE, d), lambda b, t: (b, jnp.minimum(t, nct - 1), 0)),
                 pl.BlockSpec((nb, TILE, d), lambda b, t: (b, jnp.maximum(t - nct, 0), 0))]
        args = list(x)
    else:
        specs, args = [rows_spec(d)], [x]
    mod_spec = pl.BlockSpec((nb, N_MOD, d), lambda b, t: (jnp.where(t + skip < nct, ctx_blk, b), 0, 0))
    specs += [mod_spec, _const_spec(g.shape), weight_spec(w1), weight_spec(w3), weight_spec(w2)]
    args += [mod, g, w1, w3, w2]
    if mixer is not None:
        o, wo = mixer
        specs += [rows_spec(o.shape[-1]), _const_spec(wo.shape)]
        args += [o, wo]
    return pl.pallas_call(
        functools.partial(_ffn_body, mod_row=6 * half, g_row=4 * half, with_mixer=mixer is not None,
                          n_ctx_tiles=nct, split_input=split_input),
        grid=(st.b // nb, st.n_tiles - skip),
        in_specs=specs,
        out_specs=pl.BlockSpec((nb, TILE, d), lambda b, t: (b, t, 0)),
        out_shape=jax.ShapeDtypeStruct((st.b, st.t - skip * TILE, d), F32),
        compiler_params=_cparams("parallel", "parallel"),
        name="swiglu_half",
    )(*args)


def _rwkv_feat_body(*refs, n_ctx_tiles, n_tiles, with_vres):
    (x_ref, xp_ref, xn_ref, mod_ref, g_ref, mu_ref, wr_ref, wk_ref, wv_ref, kk_ref, ka_ref,
     w0_ref, w1_ref, w2_ref, a0_ref, a1_ref, a2_ref, g1_ref, g2_ref, ones_ref) = refs[:20]
    rest = refs[20:]
    if with_vres:
        vf_ref, v0_ref, v1_ref, v2_ref = rest[:4]
        rest = rest[4:]
    r_out, v_out, kk_out, lw_out, kd_out, ab_out, g_out = rest

    t = pl.program_id(1)
    m = mod_ref[0]
    gn = g_ref[...][2:3]
    shift, scale = m[3:4], m[4:5]
    n = _norm_mod(x_ref[0], gn, scale, shift)
    first = jnp.logical_or(t == 0, t == n_ctx_tiles)
    last = jnp.logical_or(t == n_ctx_tiles - 1, t == n_tiles - 1)
    n_before = jnp.where(first, 0.0, _norm_mod(xp_ref[0][7:8], gn, scale, shift))
    n_after = jnp.where(last, 0.0, _norm_mod(xn_ref[0][0:1], gn, scale, shift))
    row = lax.broadcasted_iota(jnp.int32, (TILE, 1), 0)
    n_prev = jnp.where(row == 0, n_before, pltpu.roll(n, 1, 0))
    n_next = jnp.where(row == TILE - 1, n_after, pltpu.roll(n, TILE - 1, 0))
    xx = 0.5 * (n_prev + n_next) - n
    mu = mu_ref[...]
    xr, xw, xk, xv, xa, xg = (n + xx * mu[i:i + 1] for i in range(6))

    r = _dot(xr, wr_ref[...])
    k = _dot(xk, wk_ref[...])
    v = _dot(xv, wv_ref[...])
    if with_vres:
        lora = _dot(_dot(xv, v1_ref[...]), v2_ref[...])
        v = v + (vf_ref[0] - v) * _sigmoid(v0_ref[...] + lora)
    kk = k * kk_ref[...]
    kk = kk / jnp.maximum(jnp.sqrt(_seg_sum(kk * kk, ones_ref, 2)), L2_EPS)
    r_out[0] = r.astype(BF16)
    v_out[0] = v
    kk_out[0] = kk
    k_ka = k * ka_ref[...]
    k_rest = k - k_ka
    for dr in range(2):
        wl = w0_ref[dr] + _dot(jnp.tanh(_dot(xw, w1_ref[dr])), w2_ref[dr])
        a = _sigmoid(a0_ref[dr] + _dot(_dot(xa, a1_ref[dr]), a2_ref[dr]))
        lw_out[dr, 0] = -EXP_NEG_HALF * _sigmoid(wl)
        kd_out[dr, 0] = (k_rest + k_ka * a).astype(BF16)
        ab_out[dr, 0] = kk * a
        g_out[dr, 0] = _dot(_sigmoid(_dot(xg, g1_ref[dr])), g2_ref[dr]).astype(BF16)


def _rwkv_features(st, x, mod, g, p, seg, vres):
    d = st.d
    n8 = st.t // 8
    per8 = TILE // 8
    prev_spec = pl.BlockSpec((1, 8, d), lambda b, t: (b, jnp.maximum(t * per8 - 1, 0), 0))
    next_spec = pl.BlockSpec((1, 8, d), lambda b, t: (b, jnp.minimum((t + 1) * per8, n8 - 1), 0))
    dir_spec = pl.BlockSpec((2, 1, TILE, d), lambda b, t: (0, b, t, 0))
    dir_f32 = jax.ShapeDtypeStruct((2, st.b, st.t, d), F32)
    dir_bf16 = jax.ShapeDtypeStruct((2, st.b, st.t, d), BF16)
    names = ("mu", "wr", "wk", "wv", "k_k", "k_a", "w0", "w1", "w2", "a0", "a1", "a2", "g1", "g2")
    consts = [p[nm] for nm in names] + [seg]
    args = [x, x, x, mod, g] + consts
    specs = [st.tile_spec(), prev_spec, next_spec, st.mod_spec(), _const_spec(g.shape)]
    specs += [_const_spec(c.shape) for c in consts]
    if vres is not None:
        v_first, v0, v1, v2 = vres
        args += [v_first, v0, v1, v2]
        specs += [st.tile_spec(), _const_spec(v0.shape), _const_spec(v1.shape), _const_spec(v2.shape)]
    return pl.pallas_call(
        functools.partial(_rwkv_feat_body, n_ctx_tiles=st.n_ctx_tiles, n_tiles=st.n_tiles,
                          with_vres=vres is not None),
        grid=(st.b, st.n_tiles),
        in_specs=specs,
        out_specs=[st.tile_spec()] * 3 + [dir_spec] * 4,
        out_shape=[st.out_shape(dtype=BF16), st.out_shape(), st.out_shape(), dir_f32, dir_bf16, dir_f32, dir_bf16],
        compiler_params=_cparams("parallel", "parallel"),
        name="rwkv_features",
    )(*args)


def _unit_triangular_inverses(mats, ri, ci, size, upper):
    n = mats[0].shape[0]
    eye = (ri == ci).astype(F32)
    ts = [jnp.where((ri >> 1) == (ci >> 1), a, 0.0) + eye for a in mats]
    level = 1
    while (2 << level) <= size:
        m = 1 << level
        off = jnp.logical_and((ri >> (level + 1)) == (ci >> (level + 1)), (ri >> level) != (ci >> level))
        a_off = [jnp.where(off, a, 0.0) for a in mats]
        if m % SUBLANES == 0:
            moving = range(0 if upper else 1, n // m, 2)

            def rows(x):
                return jnp.concatenate([x[b * m:(b + 1) * m] for b in moving], axis=0)

            zs = [_dot(_dot(rows(t), a), t) for t, a in zip(ts, a_off)]
            ts = [jnp.concatenate([t[b * m:(b + 1) * m] + z[(b // 2) * m:(b // 2 + 1) * m] if b in moving
                                   else t[b * m:(b + 1) * m] for b in range(n // m)], axis=0)
                  for t, z in zip(ts, zs)]
        else:
            ts = [t + _dot(_dot(t, a), t) for t, a in zip(ts, a_off)]
        level += 1
    return ts


def _scan_body(r_ref, lw_ref, k_ref, v_ref, kk_ref, ab_ref, y_ref, s_ref, *, reverse):
    c = CHUNK
    n2 = 2 * c

    @pl.when(pl.program_id(1) == 0)
    def _():
        s_ref[...] = jnp.zeros_like(s_ref)

    row = lax.broadcasted_iota(jnp.int32, (c, 1), 0)

    def running_sum(x):
        step = 1
        while step < c:
            if reverse:
                x = x + jnp.where(row < c - step, pltpu.roll(x, c - step, 0), 0.0)
            else:
                x = x + jnp.where(row >= step, pltpu.roll(x, step, 0), 0.0)
            step *= 2
        return x

    ri = lax.broadcasted_iota(jnp.int32, (n2, n2), 0)
    ci = lax.broadcasted_iota(jnp.int32, (n2, n2), 1)
    same = (ri >> CHUNK_SHIFT) == (ci >> CHUNK_SHIFT)
    tt, ss = ri & (c - 1), ci & (c - 1)
    before = (ss > tt) if reverse else (ss < tt)
    strict = jnp.logical_and(same, before)
    incl = jnp.logical_and(same, jnp.logical_or(before, ss == tt))
    incl2 = jnp.concatenate([incl, incl], axis=1)
    even = lax.broadcasted_iota(jnp.int32, (1, LANES), 1) < RWKV_HEAD

    def stack(x):
        return jnp.concatenate([jnp.where(even, x, 0.0), jnp.where(even, 0.0, x)], axis=0)

    n_pairs = r_ref.shape[-1] // LANES
    chains = [(q, p) for q in range(r_ref.shape[0]) for p in range(n_pairs)]
    cols = {}
    for q in range(r_ref.shape[0]):
        r, k = r_ref[q].astype(F32), k_ref[0, q].astype(F32)
        lw, v, kk, ab = lw_ref[0, q], v_ref[q], kk_ref[q], ab_ref[0, q]
        cum = running_sum(lw)
        tot = cum[0:1] if reverse else cum[c - 1:c]
        inv = jnp.exp(-cum)
        to_end = jnp.exp(tot - cum)
        full = {"a": -kk * jnp.exp(cum - lw), "r": r * jnp.exp(cum), "b": ab * inv, "k": k * inv,
                "be": ab * to_end, "ke": k * to_end, "w": jnp.exp(tot), "v": v}
        for p in range(n_pairs):
            cols[q, p] = {nm: x[:, p * LANES:(p + 1) * LANES] for nm, x in full.items()}

    prods = [_dot_nt(jnp.concatenate([stack(cols[ch]["a"]), stack(cols[ch]["r"])], axis=0),
                     jnp.concatenate([stack(cols[ch]["b"]), stack(cols[ch]["k"])], axis=0)) for ch in chains]
    a_b = [jnp.where(strict, pr[0:n2, 0:n2], 0.0) for pr in prods]
    t_inv = _unit_triangular_inverses(a_b, ri, ci, c, reverse)
    v_st = [stack(cols[ch]["v"]) for ch in chains]
    u_own = [_dot(jnp.where(strict, pr[0:n2, n2:2 * n2], 0.0), vs) for pr, vs in zip(prods, v_st)]
    from_state = [_dot_nt(jnp.concatenate([cols[ch]["a"], cols[ch]["r"]], axis=0), s_ref[ch])
                  for ch in chains]
    u_st = [_dot(t, stack(fs[0:c]) + uo) for t, fs, uo in zip(t_inv, from_state, u_own)]
    y_st = [_dot(jnp.where(incl2, pr[n2:2 * n2], 0.0), jnp.concatenate([us, vs], axis=0))
            for pr, us, vs in zip(prods, u_st, v_st)]
    for i, (q, p) in enumerate(chains):
        y_ref[q, :, p * LANES:(p + 1) * LANES] = (from_state[i][c:2 * c] + y_st[i][0:c] + y_st[i][c:n2]).astype(BF16)
        u = u_st[i][0:c] + u_st[i][c:n2]
        upd = _dot_tn(jnp.concatenate([u, cols[q, p]["v"]], axis=0),
                      jnp.concatenate([cols[q, p]["be"], cols[q, p]["ke"]], axis=0))
        s_ref[q, p] = s_ref[q, p] * cols[q, p]["w"] + jnp.where(same, upd, 0.0)


def _rwkv_scan(st, r, lw, kd, v, kk, ab, direction):
    d = st.d
    nq = SCAN_SEQS if st.b % SCAN_SEQS == 0 else 1
    nc, ncc = st.t // CHUNK, st.ctx // CHUNK
    if direction == 0:
        def cidx(s):
            return s
    else:
        def cidx(s):
            return jnp.where(s < ncc, ncc - 1 - s, nc - 1 - s + ncc)
    shared = pl.BlockSpec((nq, CHUNK, d), lambda b, s: (b, cidx(s), 0))
    per_dir = pl.BlockSpec((1, nq, CHUNK, d), lambda b, s: (direction, b, cidx(s), 0))
    return pl.pallas_call(
        functools.partial(_scan_body, reverse=direction == 1),
        grid=(st.b // nq, nc),
        in_specs=[shared, per_dir, per_dir, shared, shared, per_dir],
        out_specs=shared,
        out_shape=st.out_shape(dtype=BF16),
        scratch_shapes=[pltpu.VMEM((nq, d // LANES, LANES, LANES), F32)],
        compiler_params=_cparams("parallel", "arbitrary"),
        name="rwkv_scan_bwd" if direction else "rwkv_scan_fwd",
    )(r, lw, kd, v, kk, ab)


def _rwkv_out_body(x_ref, mod_ref, g_ref, yf_ref, yb_ref, r_ref, kd_ref, v_ref, gate_ref,
                   lnw_ref, lnb_ref, rk_ref, wo_ref, ones_ref, o_ref):
    x = x_ref[...]
    flat = (-1, x.shape[-1])
    r, v = r_ref[...].astype(F32).reshape(flat), v_ref[...].reshape(flat)
    inv_n = 1.0 / RWKV_HEAD
    o = None
    for dr, y_ref in enumerate((yf_ref, yb_ref)):
        y = y_ref[...].astype(F32).reshape(flat)
        cen = y - _seg_sum(y, ones_ref, 1) * inv_n
        var = _seg_sum(cen * cen, ones_ref, 1) * inv_n
        yn = cen * lax.rsqrt(var + GN_EPS) * lnw_ref[dr] + lnb_ref[dr]
        bonus = _seg_sum(r * kd_ref[dr].astype(F32).reshape(flat) * rk_ref[...], ones_ref, 1) * v
        od = (yn + bonus) * gate_ref[dr].astype(F32).reshape(flat)
        o = od if o is None else o + od
    y_mix = _dot(o, wo_ref[...]).reshape(x.shape)
    o_ref[...] = x + mod_ref[...][:, 5:6] * _rms(y_mix, g_ref[...][3:4])


def _rwkv_out(st, x, mod, g, y_f, y_b, r, kd, v, gate, p, seg):
    d, nb = st.d, st.seqs
    rows = st.tile_spec(seqs=nb)
    dir_spec = pl.BlockSpec((2, nb, TILE, d), lambda b, t: (0, b, t, 0))
    consts = [p["ln_w"], p["ln_b"], p["r_k"], p["wo"]] + [seg]
    return pl.pallas_call(
        _rwkv_out_body,
        grid=(st.b // nb, st.n_tiles),
        in_specs=[rows, st.mod_spec(nb), _const_spec(g.shape), rows, rows, rows, dir_spec, rows, dir_spec]
                 + [_const_spec(c.shape) for c in consts],
        out_specs=rows,
        out_shape=st.out_shape(),
        compiler_params=_cparams("parallel", "parallel"),
        name="rwkv_readout",
    )(x, mod, g, y_f, y_b, r, kd, v, gate, *consts)


def _qkv_body(x_ref, mod_ref, g_ref, wq_ref, wk_ref, wv_ref, gq_ref, gk_ref, ones_ref,
              cos_ref, sin_ref, q_out, k_out, v_out, *, qk_norm, q_scale):
    x, m = x_ref[...], mod_ref[...]
    nb = x.shape[0]
    n = _norm_mod(x, g_ref[...][2:3], m[:, 4:5], m[:, 3:4]).astype(BF16).reshape(-1, x.shape[-1])
    cos = jnp.concatenate([cos_ref[...]] * nb, axis=0)
    sin = jnp.concatenate([sin_ref[...]] * nb, axis=0)
    lane = lax.broadcasted_iota(jnp.int32, (1, LANES), 1)
    second_half = (lane & N_FREQ) != 0
    tail = lane >= HEAD_DIM

    def rotated(w_ref, gain_ref, out_ref, scale):
        t_all = jnp.dot(n, w_ref[...], preferred_element_type=F32)
        if qk_norm:
            mean_sq = _seg_sum(t_all * t_all, ones_ref, 2) * (1.0 / HEAD_DIM)
            t_all = t_all * lax.rsqrt(mean_sq + NORM_EPS) * gain_ref[...]
        for j in range(t_all.shape[1] // LANES):
            t = t_all[:, j * LANES:(j + 1) * LANES]
            partner = jnp.where(second_half, pltpu.roll(t, N_FREQ, 1), pltpu.roll(t, LANES - N_FREQ, 1))
            out = t * cos + partner * sin
            out = (out * scale if scale != 1.0 else out).astype(BF16)
            for q in range(nb):
                out_ref[q, 2 * j] = out[q * TILE:(q + 1) * TILE, 0:HEAD_DIM]
                out_ref[q, 2 * j + 1] = out[q * TILE:(q + 1) * TILE, HEAD_DIM:LANES]

    rotated(wq_ref, gq_ref, q_out, q_scale)
    rotated(wk_ref, gk_ref, k_out, 1.0)
    v_all = jnp.dot(n, wv_ref[...], preferred_element_type=F32)
    for j in range(v_all.shape[1] // LANES):
        v2 = v_all[:, j * LANES:(j + 1) * LANES]
        first = jnp.where(tail, 1.0, v2).astype(BF16)
        second = jnp.where(tail, 1.0, pltpu.roll(v2, HEAD_DIM, 1)).astype(BF16)
        for q in range(nb):
            v_out[q, 2 * j] = first[q * TILE:(q + 1) * TILE]
            v_out[q, 2 * j + 1] = second[q * TILE:(q + 1) * TILE]


def _qkv_project(st, x, mod, g, a, rope, seg, qk_norm):
    nh, nkv = a["wq"].shape[1] // HEAD_DIM, a["wk"].shape[1] // HEAD_DIM
    consts = [a["wq"], a["wk"], a["wv"], a["gq"], a["gk"], seg]
    rope_spec = pl.BlockSpec((TILE, LANES), lambda b, t: (t, 0))

    nb = st.seqs

    def head_spec(n, width):
        return pl.BlockSpec((nb, n, TILE, width), lambda b, t: (b, 0, t, 0))

    def head_shape(n, width):
        return jax.ShapeDtypeStruct((st.b, n, st.t, width), BF16)

    return pl.pallas_call(
        functools.partial(_qkv_body, qk_norm=qk_norm, q_scale=HEAD_DIM ** -0.5 * LOG2_E),
        grid=(st.b // nb, st.n_tiles),
        in_specs=[st.tile_spec(seqs=nb), st.mod_spec(nb), _const_spec(g.shape)]
                 + [_const_spec(c.shape) for c in consts] + [rope_spec, rope_spec],
        out_specs=[head_spec(nh, HEAD_DIM), head_spec(nkv, HEAD_DIM), head_spec(nkv, LANES)],
        out_shape=[head_shape(nh, HEAD_DIM), head_shape(nkv, HEAD_DIM), head_shape(nkv, LANES)],
        compiler_params=_cparams("parallel", "parallel"),
        name="qkv_rope",
    )(x, mod, g, *consts, *rope)


def _key_chunks(k0, k1):
    n = -(-(k1 - k0) // MAX_KEYS)
    size = -(-(k1 - k0) // (n * LANES)) * LANES
    return [(a, min(size, k1 - a)) for a in range(k0, k1, size)]


def _attn_body(*refs, ctx_len, n_ctx_tiles, total, windowed):
    if windowed:
        sink_ref, q_ref, k_ref, v_ref, o_ref = refs
    else:
        q_ref, k_ref, v_ref, o_ref = refs
    heads = q_ref.shape[1]
    group = heads // k_ref.shape[1]
    t = pl.program_id(2)
    tail = lax.broadcasted_iota(jnp.int32, (1, LANES), 1) >= HEAD_DIM
    chains = range(heads)
    qs = [q_ref[0, c] for c in chains]

    def scores(k0, size):
        return [_dot_nt(q, k_ref[0, c // group, pl.ds(k0, size), :]) for c, q in zip(chains, qs)]

    def weigh(ps, k0, size):
        return [jnp.dot(p.astype(BF16), v_ref[0, c // group, pl.ds(k0, size), :], preferred_element_type=F32)
                for c, p in zip(chains, ps)]

    def finish(accs):
        invs = [1.0 / jnp.where(tail, acc, 1.0) for acc in accs]
        cols = []
        for c in range(0, heads, 2):
            first = accs[c] * pltpu.roll(invs[c], HEAD_DIM, 1)
            second = pltpu.roll(accs[c + 1], HEAD_DIM, 1) * invs[c + 1]
            cols.append(jnp.where(tail, second, first))
        return jnp.concatenate(cols, axis=1)

    def row_max(xs):
        return [jnp.max(x, axis=-1, keepdims=True) for x in xs]

    if windowed:
        sinks = [jnp.full((TILE, 1), sink_ref[pl.program_id(1) * heads + c], F32) * LOG2_E for c in chains]

        def with_sink(s_parts, spans):
            ms = sinks
            for part in s_parts:
                ms = [jnp.maximum(m, pm) for m, pm in zip(ms, row_max(part))]
            accs = [jnp.where(tail, jnp.exp2(sk - m), 0.0) for sk, m in zip(sinks, ms)]
            for part, (k0, size) in zip(s_parts, spans):
                pvs = weigh([jnp.exp2(s - m) for s, m in zip(part, ms)], k0, size)
                accs = [acc + pv for acc, pv in zip(accs, pvs)]
            return finish(accs)

        def context_queries():
            return with_sink([scores(0, ctx_len)], [(0, ctx_len)])

        def latent_queries():
            k0 = pl.multiple_of(jnp.clip((t - 1) * TILE, 0, total - LOCAL_KEYS), TILE)
            k_pos = k0 + lax.broadcasted_iota(jnp.int32, (TILE, LOCAL_KEYS), 1)
            q_pos = t * TILE + lax.broadcasted_iota(jnp.int32, (TILE, LOCAL_KEYS), 0)
            mask = jnp.logical_and(jnp.abs(k_pos - q_pos) <= WINDOW, k_pos >= ctx_len)
            s_loc = [jnp.where(mask, s, NEG_INF) for s in scores(k0, LOCAL_KEYS)]
            return with_sink([scores(0, ctx_len), s_loc], [(0, ctx_len), (k0, LOCAL_KEYS)])
    else:
        def online(spans):
            ms = accs = None
            for k0, size in spans:
                ss = scores(k0, size)
                m_chunk = row_max(ss)
                if ms is None:
                    ms, accs = m_chunk, weigh([jnp.exp2(s - m) for s, m in zip(ss, m_chunk)], k0, size)
                else:
                    m_new = [jnp.maximum(m, mc) for m, mc in zip(ms, m_chunk)]
                    pvs = weigh([jnp.exp2(s - m) for s, m in zip(ss, m_new)], k0, size)
                    accs = [jnp.exp2(m - mn) * acc + pv for m, mn, acc, pv in zip(ms, m_new, accs, pvs)]
                    ms = m_new
            return finish(accs)

        def context_queries():
            return online([(0, ctx_len)])

        def latent_queries():
            return online(_key_chunks(0, total))

    o_ref[0] = lax.cond(t < n_ctx_tiles, context_queries, latent_queries).astype(BF16)


def _attention(st, q, k, v, sink):
    nh, nkv = q.shape[1], k.shape[1]
    group = nh // nkv
    windowed = sink is not None
    assert st.t >= LOCAL_KEYS
    want = WINDOW_KV_HEADS if windowed else ATTN_KV_HEADS
    kvs = want if nkv % want == 0 else 1
    q_spec = pl.BlockSpec((1, kvs * group, TILE, HEAD_DIM), lambda b, h, t: (b, h, t, 0))
    k_spec = pl.BlockSpec((1, kvs, st.t, HEAD_DIM), lambda b, h, t: (b, h, 0, 0))
    v_spec = pl.BlockSpec((1, kvs, st.t, LANES), lambda b, h, t: (b, h, 0, 0))
    specs, args = [q_spec, k_spec, v_spec], [q, k, v]
    if windowed:
        specs = [pl.BlockSpec(memory_space=pltpu.SMEM)] + specs
        args = [sink] + args
    return pl.pallas_call(
        functools.partial(_attn_body, ctx_len=st.ctx, n_ctx_tiles=st.n_ctx_tiles, total=st.t,
                          windowed=windowed),
        grid=(st.b, nkv // kvs, st.n_tiles),
        in_specs=specs,
        out_specs=pl.BlockSpec((1, TILE, kvs * group * HEAD_DIM), lambda b, h, t: (b, t, h)),
        out_shape=jax.ShapeDtypeStruct((st.b, st.t, nh * HEAD_DIM), BF16),
        compiler_params=_cparams("parallel", "parallel", "parallel"),
        name="window_attention" if windowed else "global_attention",
    )(*args)


def _pad_to(w, axis, mult):
    pad = (-w.shape[axis]) % mult
    if pad == 0:
        return w
    widths = [(0, 0)] * w.ndim
    widths[axis] = (0, pad)
    return jnp.pad(w, widths)


def _rope_tables(st):
    rows = (st.t - st.ctx) // GRID_W
    row = jnp.broadcast_to(jnp.arange(rows)[:, None], (rows, GRID_W)).reshape(-1)
    col = jnp.broadcast_to(jnp.arange(GRID_W)[None, :], (rows, GRID_W)).reshape(-1)
    inv_freq = ROPE_THETA ** (-jnp.arange(N_FREQ, dtype=F32) / N_FREQ)
    ang = jnp.stack([row, col], axis=-1).astype(F32)[:, :, None] * inv_freq
    cos, sin = jnp.cos(ang), jnp.sin(ang)
    cos_t = jnp.stack([cos, cos], axis=2).reshape(-1, HEAD_DIM)
    sin_t = jnp.stack([-sin, sin], axis=2).reshape(-1, HEAD_DIM)
    cos_t = jnp.concatenate([jnp.ones((st.ctx, HEAD_DIM), F32), cos_t], axis=0)
    sin_t = jnp.concatenate([jnp.zeros((st.ctx, HEAD_DIM), F32), sin_t], axis=0)
    return jnp.tile(cos_t, (1, LANES // HEAD_DIM)), jnp.tile(sin_t, (1, LANES // HEAD_DIM))


def _attn_params(wq, wk, wv, wo, gq, gk):
    nh, nkv = wq.shape[1] // HEAD_DIM, wk.shape[1] // HEAD_DIM
    return {"wq": wq.astype(BF16), "wk": wk.astype(BF16), "wv": wv.astype(BF16), "wo": wo.astype(BF16),
            "gq": jnp.tile(gq, nh)[None], "gk": jnp.tile(gk, nkv)[None]}


def _segment_ones():
    head = np.arange(SEG_TILE) // RWKV_HEAD
    return jnp.asarray(head[:, None] == head[None, :], dtype=BF16)


def kernel(x, c, ctx, c_ctx, mod_w, mod_b, norm_g, ffn_w1, ffn_w3, ffn_w2, rwkv_mu, rwkv_wr, rwkv_wk, rwkv_wv, rwkv_wo, rwkv_k_k, rwkv_k_a, rwkv_r_k, rwkv_w0, rwkv_w1, rwkv_w2, rwkv_a0, rwkv_a1, rwkv_a2, rwkv_g1, rwkv_g2, rwkv_ln_w, rwkv_ln_b, rwkv_v0, rwkv_v1, rwkv_v2, gattn_wq, gattn_wk, gattn_wv, gattn_wo, gattn_q_norm, gattn_k_norm, wattn_wq, wattn_wk, wattn_wv, wattn_wo, wattn_sink):
    batch, seq, d = x.shape
    depth = mod_w.shape[0]
    st = _Stream(batch, ctx.shape[1], seq, d)

    ctx_rows = max(ROW_SEQS, FFN_SEQS)
    rows = -(-(batch + ctx_rows) // SUBLANES) * SUBLANES
    c_rows = jnp.zeros((rows, d), F32).at[:batch].set(c).at[batch:batch + ctx_rows].set(c_ctx)
    mods = _modulation(c_rows, mod_w, mod_b).reshape(depth, rows, N_MOD, d)

    ffn_w = (ffn_w1.astype(BF16), ffn_w3.astype(BF16), ffn_w2.astype(BF16))
    rope = _rope_tables(st)
    seg = _segment_ones()
    bf = lambda w: w.astype(BF16)

    h = (ctx, x)
    v_first = None
    for i in range(depth):
        kind, j = i % 3, i // 3
        g, mod = norm_g[i], mods[i]
        h = _ffn_half(st, h, mod, g, ffn_w, i, 0)
        mixer = None
        if kind == 0:
            p = {"mu": rwkv_mu[j], "wr": bf(rwkv_wr[j]), "wk": bf(rwkv_wk[j]), "wv": bf(rwkv_wv[j]),
                 "wo": bf(rwkv_wo[j]), "k_k": rwkv_k_k[j][None], "k_a": rwkv_k_a[j][None],
                 "r_k": rwkv_r_k[j].reshape(1, d),
                 "w0": rwkv_w0[j][:, None], "w1": bf(_pad_to(rwkv_w1[j], 2, LANES)), "w2": bf(_pad_to(rwkv_w2[j], 1, LANES)),
                 "a0": rwkv_a0[j][:, None], "a1": bf(_pad_to(rwkv_a1[j], 2, LANES)), "a2": bf(_pad_to(rwkv_a2[j], 1, LANES)),
                 "g1": bf(_pad_to(rwkv_g1[j], 2, LANES)), "g2": bf(_pad_to(rwkv_g2[j], 1, LANES)),
                 "ln_w": rwkv_ln_w[j][:, None], "ln_b": rwkv_ln_b[j][:, None]}
            vres = None
            if j > 0:
                vres = (v_first, rwkv_v0[j - 1][None], bf(_pad_to(rwkv_v1[j - 1], 1, LANES)),
                        bf(_pad_to(rwkv_v2[j - 1], 0, LANES)))
            r, v, kk, lw, kd, ab, gate = _rwkv_features(st, h, mod, g, p, seg, vres)
            if j == 0:
                v_first = v
            y_f = _rwkv_scan(st, r, lw, kd, v, kk, ab, 0)
            y_b = _rwkv_scan(st, r, lw, kd, v, kk, ab, 1)
            h = _rwkv_out(st, h, mod, g, y_f, y_b, r, kd, v, gate, p, seg)
        else:
            if kind == 1:
                a = _attn_params(gattn_wq[j], gattn_wk[j], gattn_wv[j], gattn_wo[j],
                                 gattn_q_norm[j], gattn_k_norm[j])
                sink = None
            else:
                ones = jnp.ones((HEAD_DIM,), F32)
                a = _attn_params(wattn_wq[j], wattn_wk[j], wattn_wv[j], wattn_wo[j], ones, ones)
                sink = wattn_sink[j]
            q, k, v = _qkv_project(st, h, mod, g, a, rope, seg, qk_norm=kind == 1)
            mixer = (_attention(st, q, k, v, sink), a["wo"])
        h = _ffn_half(st, h, mod, g, ffn_w, i, 1, latent_only=i == depth - 1, mixer=mixer)
    return h
```

```python
import functools

import numpy as np
import jax
import jax.numpy as jnp
from jax import lax
from jax.experimental import pallas as pl
from jax.experimental.pallas import tpu as pltpu

F32 = jnp.float32
BF16 = jnp.bfloat16
HIGHEST = lax.Precision.HIGHEST

GRID_W = 64
N_MOD = 9
HEAD_DIM = 64
WINDOW = 128
ROPE_THETA = 10000.0
N_FREQ = HEAD_DIM // 4
RWKV_HEAD = 64
NORM_EPS = 1e-6
GN_EPS = 64e-5
L2_EPS = 1e-12
NEG_INF = -1e30
LOG2_E = 1.4426950408889634
EXP_NEG_HALF = 0.6065306597126334

LANES = 128
SUBLANES = 8
TILE = 256
ROW_SEQS = 2
FFN_SEQS = 2
FFN_CHAIN_SEQS = 1
SCAN_SEQS = 4
ATTN_KV_HEADS = 2
WINDOW_KV_HEADS = 4
CHUNK = 64
MAX_KEYS = 4352
SEG_TILE = 256
LOCAL_KEYS = 3 * TILE
VMEM_LIMIT = 56 * 1024 * 1024

assert RWKV_HEAD == CHUNK and 2 * CHUNK == LANES and HEAD_DIM == RWKV_HEAD and SEG_TILE % RWKV_HEAD == 0
CHUNK_SHIFT = CHUNK.bit_length() - 1


def _cparams(*sem):
    return pltpu.CompilerParams(dimension_semantics=sem, vmem_limit_bytes=VMEM_LIMIT)


def _const_spec(shape):
    nd = len(shape)
    return pl.BlockSpec(shape, lambda *_: (0,) * nd, pipeline_mode=pl.Buffered(1))


def _dot(a, b):
    return jnp.dot(a.astype(BF16), b.astype(BF16), preferred_element_type=F32)


def _dot_nt(a, b):
    return lax.dot_general(a.astype(BF16), b.astype(BF16), (((1,), (1,)), ((), ())),
                           preferred_element_type=F32)


def _dot_tn(a, b):
    return lax.dot_general(a.astype(BF16), b.astype(BF16), (((0,), (0,)), ((), ())),
                           preferred_element_type=F32)


def _dot_f32(a, b):
    return jnp.dot(a, b, precision=HIGHEST, preferred_element_type=F32)


def _sigmoid(x):
    return 1.0 / (1.0 + jnp.exp2(x * -LOG2_E))


def _rms(x, g):
    return x * lax.rsqrt(jnp.mean(x * x, axis=-1, keepdims=True) + NORM_EPS) * g


def _norm_mod(x, g, scale, shift):
    return _rms(x, g) * (1.0 + scale) + shift


def _seg_sum(x, ones_ref, pieces):
    ones = ones_ref[...]
    parts, rest = [], x
    for i in range(pieces):
        part = rest.astype(BF16)
        parts.append(part)
        if i + 1 < pieces:
            rest = rest - part.astype(F32)
    cols = []
    for j in range(x.shape[1] // SEG_TILE):
        sl = slice(j * SEG_TILE, (j + 1) * SEG_TILE)
        cols.append(sum(jnp.dot(part[:, sl], ones, preferred_element_type=F32) for part in parts))
    return jnp.concatenate(cols, axis=1)


def _mod_body(c_ref, w_ref, b_ref, o_ref):
    c = c_ref[...]
    o_ref[0] = _dot_f32(c * _sigmoid(c), w_ref[0]) + b_ref[0]


def _modulation(c_rows, mod_w, mod_b):
    depth, d, nd = mod_w.shape
    rows = c_rows.shape[0]
    tn = nd // 6
    return pl.pallas_call(
        _mod_body,
        grid=(depth, nd // tn),
        in_specs=[_const_spec((rows, d)),
                  pl.BlockSpec((1, d, tn), lambda i, j: (i, 0, j)),
                  pl.BlockSpec((1, 1, tn), lambda i, j: (i, 0, j))],
        out_specs=pl.BlockSpec((1, rows, tn), lambda i, j: (i, 0, j)),
        out_shape=jax.ShapeDtypeStruct((depth, rows, nd), F32),
        compiler_params=_cparams("arbitrary", "arbitrary"),
        name="adaln_rows",
    )(c_rows, mod_w, mod_b.reshape(depth, 1, nd))


class _Stream:
    def __init__(self, batch, ctx_len, seq, d):
        assert ctx_len % TILE == 0 and seq % TILE == 0 and d % LANES == 0
        self.b, self.t, self.d = batch, ctx_len + seq, d
        self.ctx = ctx_len
        self.n_tiles = self.t // TILE
        self.n_ctx_tiles = ctx_len // TILE

        self.seqs = ROW_SEQS if batch % ROW_SEQS == 0 else 1

    def tile_spec(self, width=None, seqs=1):
        return pl.BlockSpec((seqs, TILE, width or self.d), lambda b, t: (b, t, 0))

    def mod_spec(self, seqs=1):
        nct, ctx_blk = self.n_ctx_tiles, self.b // seqs
        return pl.BlockSpec((seqs, N_MOD, self.d), lambda b, t: (jnp.where(t < nct, ctx_blk, b), 0, 0))

    def out_shape(self, width=None, dtype=F32):
        return jax.ShapeDtypeStruct((self.b, self.t, width or self.d), dtype)


def _ffn_body(*refs, mod_row, g_row, with_mixer, n_ctx_tiles, split_input):
    refs = list(refs)
    if split_input:
        ctx_ref, lat_ref = refs[:2]
        x = jnp.where(pl.program_id(1) < n_ctx_tiles, ctx_ref[...], lat_ref[...])
        refs = refs[2:]
    else:
        x = refs.pop(0)[...]
    if with_mixer:
        mod_ref, g_ref, w1_ref, w3_ref, w2_ref, o_ref, wo_ref, out_ref = refs
    else:
        mod_ref, g_ref, w1_ref, w3_ref, w2_ref, out_ref = refs
    m = mod_ref[...]
    g = g_ref[...]
    d = x.shape[-1]
    per = min(FFN_CHAIN_SEQS, x.shape[0])
    chains = [slice(q, q + per) for q in range(0, x.shape[0], per)]
    xs, ms = [x[c] for c in chains], [m[c] for c in chains]
    if with_mixer:
        mixed = [jnp.dot(o_ref[c].reshape(-1, o_ref.shape[-1]), wo_ref[...], preferred_element_type=F32)
                 for c in chains]
        xs = [xc + mc[:, 5:6] * _rms(mx.reshape(xc.shape), g[3:4]) for xc, mc, mx in zip(xs, ms, mixed)]
    ns = [_norm_mod(xc, g[g_row:g_row + 1], mc[:, mod_row + 1:mod_row + 2], mc[:, mod_row:mod_row + 1])
          .astype(BF16).reshape(-1, d) for xc, mc in zip(xs, ms)]
    a = [jnp.dot(n, w1_ref[0, 0], preferred_element_type=F32) for n in ns]
    b = [jnp.dot(n, w3_ref[0, 0], preferred_element_type=F32) for n in ns]
    hs = [(ac * _sigmoid(ac) * bc).astype(BF16) for ac, bc in zip(a, b)]
    ys = [jnp.dot(h, w2_ref[0, 0], preferred_element_type=F32) for h in hs]
    for c, xc, mc, y in zip(chains, xs, ms, ys):
        out_ref[c] = xc + 0.5 * mc[:, mod_row + 2:mod_row + 3] * _rms(y.reshape(xc.shape), g[g_row + 1:g_row + 2])


def _ffn_half(st, x, mod, g, weights, layer, half, latent_only=False, mixer=None):
    w1, w3, w2 = weights
    d, f = w1.shape[-2:]
    nb = FFN_SEQS if st.b % FFN_SEQS == 0 else st.seqs
    nct, ctx_blk = st.n_ctx_tiles, st.b // nb
    skip = nct if latent_only else 0
    split_input = isinstance(x, tuple)

    def rows_spec(width):
        return pl.BlockSpec((nb, TILE, width), lambda b, t: (b, t + skip, 0))

    def weight_spec(w):
        return pl.BlockSpec((1, 1) + w.shape[-2:], lambda b, t: (layer, half, 0, 0), pipeline_mode=pl.Buffered(1))

    if split_input:
        specs = [pl.BlockSpec((nb, TILE, d), lambda b, t: (b, jnp.minimum(t, nct - 1), 0)),
                 pl.BlockSpec((nb, TILE, d), lambda b, t: (b, jnp.maximum(t - nct, 0), 0))]
        args = list(x)
    else:
        specs, args = [rows_spec(d)], [x]
    mod_spec = pl.BlockSpec((nb, N_MOD, d), lambda b, t: (jnp.where(t + skip < nct, ctx_blk, b), 0, 0))
    specs += [mod_spec, _const_spec(g.shape), weight_spec(w1), weight_spec(w3), weight_spec(w2)]
    args += [mod, g, w1, w3, w2]
    if mixer is not None:
        o, wo = mixer
        specs += [rows_spec(o.shape[-1]), _const_spec(wo.shape)]
        args += [o, wo]
    return pl.pallas_call(
        functools.partial(_ffn_body, mod_row=6 * half, g_row=4 * half, with_mixer=mixer is not None,
                          n_ctx_tiles=nct, split_input=split_input),
        grid=(st.b // nb, st.n_tiles - skip),
        in_specs=specs,
        out_specs=pl.BlockSpec((nb, TILE, d), lambda b, t: (b, t, 0)),
        out_shape=jax.ShapeDtypeStruct((st.b, st.t - skip * TILE, d), F32),
        compiler_params=_cparams("parallel", "parallel"),
        name="swiglu_half",
    )(*args)


def _rwkv_feat_body(*refs, n_ctx_tiles, n_tiles, with_vres):
    (x_ref, xp_ref, xn_ref, mod_ref, g_ref, mu_ref, wr_ref, wk_ref, wv_ref, kk_ref, ka_ref,
     w0_ref, w1_ref, w2_ref, a0_ref, a1_ref, a2_ref, g1_ref, g2_ref, ones_ref) = refs[:20]
    rest = refs[20:]
    if with_vres:
        vf_ref, v0_ref, v1_ref, v2_ref = rest[:4]
        rest = rest[4:]
    r_out, v_out, kk_out, lw_out, kd_out, ab_out, g_out = rest

    t = pl.program_id(1)
    m = mod_ref[0]
    gn = g_ref[...][2:3]
    shift, scale = m[3:4], m[4:5]
    n = _norm_mod(x_ref[0], gn, scale, shift)
    first = jnp.logical_or(t == 0, t == n_ctx_tiles)
    last = jnp.logical_or(t == n_ctx_tiles - 1, t == n_tiles - 1)
    n_before = jnp.where(first, 0.0, _norm_mod(xp_ref[0][7:8], gn, scale, shift))
    n_after = jnp.where(last, 0.0, _norm_mod(xn_ref[0][0:1], gn, scale, shift))
    row = lax.broadcasted_iota(jnp.int32, (TILE, 1), 0)
    n_prev = jnp.where(row == 0, n_before, pltpu.roll(n, 1, 0))
    n_next = jnp.where(row == TILE - 1, n_after, pltpu.roll(n, TILE - 1, 0))
    xx = 0.5 * (n_prev + n_next) - n
    mu = mu_ref[...]
    xr, xw, xk, xv, xa, xg = (n + xx * mu[i:i + 1] for i in range(6))

    r_out[0] = _dot(xr, wr_ref[...]).astype(BF16)
    v = _dot(xv, wv_ref[...])
    if with_vres:
        lora = _dot(_dot(xv, v1_ref[...]), v2_ref[...])
        v = v + (vf_ref[0] - v) * _sigmoid(v0_ref[...] + lora)
    v_out[0] = v
    k = _dot(xk, wk_ref[...])
    kk = k * kk_ref[...]
    kk = kk / jnp.maximum(jnp.sqrt(_seg_sum(kk * kk, ones_ref, 2)), L2_EPS)
    kk_out[0] = kk
    k_ka = k * ka_ref[...]
    k_rest = k - k_ka
    for dr in range(2):
        wl = w0_ref[dr] + _dot(jnp.tanh(_dot(xw, w1_ref[dr])), w2_ref[dr])
        a = _sigmoid(a0_ref[dr] + _dot(_dot(xa, a1_ref[dr]), a2_ref[dr]))
        lw_out[dr, 0] = -EXP_NEG_HALF * _sigmoid(wl)
        kd_out[dr, 0] = (k_rest + k_ka * a).astype(BF16)
        ab_out[dr, 0] = kk * a
        g_out[dr, 0] = _dot(_sigmoid(_dot(xg, g1_ref[dr])), g2_ref[dr]).astype(BF16)


def _rwkv_features(st, x, mod, g, p, seg, vres):
    d = st.d
    n8 = st.t // 8
    per8 = TILE // 8
    prev_spec = pl.BlockSpec((1, 8, d), lambda b, t: (b, jnp.maximum(t * per8 - 1, 0), 0))
    next_spec = pl.BlockSpec((1, 8, d), lambda b, t: (b, jnp.minimum((t + 1) * per8, n8 - 1), 0))
    dir_spec = pl.BlockSpec((2, 1, TILE, d), lambda b, t: (0, b, t, 0))
    dir_f32 = jax.ShapeDtypeStruct((2, st.b, st.t, d), F32)
    dir_bf16 = jax.ShapeDtypeStruct((2, st.b, st.t, d), BF16)
    names = ("mu", "wr", "wk", "wv", "k_k", "k_a", "w0", "w1", "w2", "a0", "a1", "a2", "g1", "g2")
    consts = [p[nm] for nm in names] + [seg]
    args = [x, x, x, mod, g] + consts
    specs = [st.tile_spec(), prev_spec, next_spec, st.mod_spec(), _const_spec(g.shape)]
    specs += [_const_spec(c.shape) for c in consts]
    if vres is not None:
        v_first, v0, v1, v2 = vres
        args += [v_first, v0, v1, v2]
        specs += [st.tile_spec(), _const_spec(v0.shape), _const_spec(v1.shape), _const_spec(v2.shape)]
    return pl.pallas_call(
        functools.partial(_rwkv_feat_body, n_ctx_tiles=st.n_ctx_tiles, n_tiles=st.n_tiles,
                          with_vres=vres is not None),
        grid=(st.b, st.n_tiles),
        in_specs=specs,
        out_specs=[st.tile_spec()] * 3 + [dir_spec] * 4,
        out_shape=[st.out_shape(dtype=BF16), st.out_shape(), st.out_shape(), dir_f32, dir_bf16, dir_f32, dir_bf16],
        compiler_params=_cparams("parallel", "parallel"),
        name="rwkv_features",
    )(*args)


def _unit_triangular_inverses(mats, ri, ci, size, upper):
    n = mats[0].shape[0]
    eye = (ri == ci).astype(F32)
    ts = [jnp.where((ri >> 1) == (ci >> 1), a, 0.0) + eye for a in mats]
    level = 1
    while (2 << level) <= size:
        m = 1 << level
        off = jnp.logical_and((ri >> (level + 1)) == (ci >> (level + 1)), (ri >> level) != (ci >> level))
        a_off = [jnp.where(off, a, 0.0) for a in mats]
        if m % SUBLANES == 0:
            moving = range(0 if upper else 1, n // m, 2)

            def rows(x):
                return jnp.concatenate([x[b * m:(b + 1) * m] for b in moving], axis=0)

            zs = [_dot(_dot(rows(t), a), t) for t, a in zip(ts, a_off)]
            ts = [jnp.concatenate([t[b * m:(b + 1) * m] + z[(b // 2) * m:(b // 2 + 1) * m] if b in moving
                                   else t[b * m:(b + 1) * m] for b in range(n // m)], axis=0)
                  for t, z in zip(ts, zs)]
        else:
            ts = [t + _dot(_dot(t, a), t) for t, a in zip(ts, a_off)]
        level += 1
    return ts


def _scan_body(r_ref, lw_ref, k_ref, v_ref, kk_ref, ab_ref, y_ref, s_ref, *, reverse):
    c = CHUNK
    n2 = 2 * c

    @pl.when(pl.program_id(1) == 0)
    def _():
        s_ref[...] = jnp.zeros_like(s_ref)

    row = lax.broadcasted_iota(jnp.int32, (c, 1), 0)

    def running_sum(x):
        step = 1
        while step < c:
            if reverse:
                x = x + jnp.where(row < c - step, pltpu.roll(x, c - step, 0), 0.0)
            else:
                x = x + jnp.where(row >= step, pltpu.roll(x, step, 0), 0.0)
            step *= 2
        return x

    ri = lax.broadcasted_iota(jnp.int32, (n2, n2), 0)
    ci = lax.broadcasted_iota(jnp.int32, (n2, n2), 1)
    same = (ri >> CHUNK_SHIFT) == (ci >> CHUNK_SHIFT)
    tt, ss = ri & (c - 1), ci & (c - 1)
    before = (ss > tt) if reverse else (ss < tt)
    strict = jnp.logical_and(same, before)
    incl = jnp.logical_and(same, jnp.logical_or(before, ss == tt))
    incl2 = jnp.concatenate([incl, incl], axis=1)
    even = lax.broadcasted_iota(jnp.int32, (1, LANES), 1) < RWKV_HEAD

    def stack(x):
        return jnp.concatenate([jnp.where(even, x, 0.0), jnp.where(even, 0.0, x)], axis=0)

    n_pairs = r_ref.shape[-1] // LANES
    chains = [(q, p) for q in range(r_ref.shape[0]) for p in range(n_pairs)]
    cols = {}
    for q in range(r_ref.shape[0]):
        r, k = r_ref[q].astype(F32), k_ref[0, q].astype(F32)
        lw, v, kk, ab = lw_ref[0, q], v_ref[q], kk_ref[q], ab_ref[0, q]
        cum = running_sum(lw)
        tot = cum[0:1] if reverse else cum[c - 1:c]
        inv = jnp.exp(-cum)
        to_end = jnp.exp(tot - cum)
        full = {"a": -kk * jnp.exp(cum - lw), "r": r * jnp.exp(cum), "b": ab * inv, "k": k * inv,
                "be": ab * to_end, "ke": k * to_end, "w": jnp.exp(tot), "v": v}
        for p in range(n_pairs):
            cols[q, p] = {nm: x[:, p * LANES:(p + 1) * LANES] for nm, x in full.items()}

    prods = [_dot_nt(jnp.concatenate([stack(cols[ch]["a"]), stack(cols[ch]["r"])], axis=0),
                     jnp.concatenate([stack(cols[ch]["b"]), stack(cols[ch]["k"])], axis=0)) for ch in chains]
    a_b = [jnp.where(strict, pr[0:n2, 0:n2], 0.0) for pr in prods]
    t_inv = _unit_triangular_inverses(a_b, ri, ci, c, reverse)
    v_st = [stack(cols[ch]["v"]) for ch in chains]
    u_own = [_dot(jnp.where(strict, pr[0:n2, n2:2 * n2], 0.0), vs) for pr, vs in zip(prods, v_st)]
    from_state = [_dot_nt(jnp.concatenate([cols[ch]["a"], cols[ch]["r"]], axis=0), s_ref[ch])
                  for ch in chains]
    u_st = [_dot(t, stack(fs[0:c]) + uo) for t, fs, uo in zip(t_inv, from_state, u_own)]
    y_st = [_dot(jnp.where(incl2, pr[n2:2 * n2], 0.0), jnp.concatenate([us, vs], axis=0))
            for pr, us, vs in zip(prods, u_st, v_st)]
    for i, (q, p) in enumerate(chains):
        y_ref[q, :, p * LANES:(p + 1) * LANES] = (from_state[i][c:2 * c] + y_st[i][0:c] + y_st[i][c:n2]).astype(BF16)
        u = u_st[i][0:c] + u_st[i][c:n2]
        upd = _dot_tn(jnp.concatenate([u, cols[q, p]["v"]], axis=0),
                      jnp.concatenate([cols[q, p]["be"], cols[q, p]["ke"]], axis=0))
        s_ref[q, p] = s_ref[q, p] * cols[q, p]["w"] + jnp.where(same, upd, 0.0)


def _rwkv_scan(st, r, lw, kd, v, kk, ab, direction):
    d = st.d
    nq = SCAN_SEQS if st.b % SCAN_SEQS == 0 else 1
    nc, ncc = st.t // CHUNK, st.ctx // CHUNK
    if direction == 0:
        def cidx(s):
            return s
    else:
        def cidx(s):
            return jnp.where(s < ncc, ncc - 1 - s, nc - 1 - s + ncc)
    shared = pl.BlockSpec((nq, CHUNK, d), lambda b, s: (b, cidx(s), 0))
    per_dir = pl.BlockSpec((1, nq, CHUNK, d), lambda b, s: (direction, b, cidx(s), 0))
    return pl.pallas_call(
        functools.partial(_scan_body, reverse=direction == 1),
        grid=(st.b // nq, nc),
        in_specs=[shared, per_dir, per_dir, shared, shared, per_dir],
        out_specs=shared,
        out_shape=st.out_shape(dtype=BF16),
        scratch_shapes=[pltpu.VMEM((nq, d // LANES, LANES, LANES), F32)],
        compiler_params=_cparams("parallel", "arbitrary"),
        name="rwkv_scan_bwd" if direction else "rwkv_scan_fwd",
    )(r, lw, kd, v, kk, ab)


def _rwkv_out_body(x_ref, mod_ref, g_ref, yf_ref, yb_ref, r_ref, kd_ref, v_ref, gate_ref,
                   lnw_ref, lnb_ref, rk_ref, wo_ref, ones_ref, o_ref):
    x = x_ref[...]
    flat = (-1, x.shape[-1])
    r, v = r_ref[...].astype(F32).reshape(flat), v_ref[...].reshape(flat)
    inv_n = 1.0 / RWKV_HEAD
    o = None
    for dr, y_ref in enumerate((yf_ref, yb_ref)):
        y = y_ref[...].astype(F32).reshape(flat)
        cen = y - _seg_sum(y, ones_ref, 1) * inv_n
        var = _seg_sum(cen * cen, ones_ref, 1) * inv_n
        yn = cen * lax.rsqrt(var + GN_EPS) * lnw_ref[dr] + lnb_ref[dr]
        bonus = _seg_sum(r * kd_ref[dr].astype(F32).reshape(flat) * rk_ref[...], ones_ref, 1) * v
        od = (yn + bonus) * gate_ref[dr].astype(F32).reshape(flat)
        o = od if o is None else o + od
    y_mix = _dot(o, wo_ref[...]).reshape(x.shape)
    o_ref[...] = x + mod_ref[...][:, 5:6] * _rms(y_mix, g_ref[...][3:4])


def _rwkv_out(st, x, mod, g, y_f, y_b, r, kd, v, gate, p, seg):
    d, nb = st.d, st.seqs
    rows = st.tile_spec(seqs=nb)
    dir_spec = pl.BlockSpec((2, nb, TILE, d), lambda b, t: (0, b, t, 0))
    consts = [p["ln_w"], p["ln_b"], p["r_k"], p["wo"]] + [seg]
    return pl.pallas_call(
        _rwkv_out_body,
        grid=(st.b // nb, st.n_tiles),
        in_specs=[rows, st.mod_spec(nb), _const_spec(g.shape), rows, rows, rows, dir_spec, rows, dir_spec]
                 + [_const_spec(c.shape) for c in consts],
        out_specs=rows,
        out_shape=st.out_shape(),
        compiler_params=_cparams("parallel", "parallel"),
        name="rwkv_readout",
    )(x, mod, g, y_f, y_b, r, kd, v, gate, *consts)


def _qkv_body(x_ref, mod_ref, g_ref, wq_ref, wk_ref, wv_ref, gq_ref, gk_ref, ones_ref,
              cos_ref, sin_ref, q_out, k_out, v_out, *, qk_norm, q_scale):
    x, m = x_ref[...], mod_ref[...]
    nb = x.shape[0]
    n = _norm_mod(x, g_ref[...][2:3], m[:, 4:5], m[:, 3:4]).astype(BF16).reshape(-1, x.shape[-1])
    cos = jnp.concatenate([cos_ref[...]] * nb, axis=0)
    sin = jnp.concatenate([sin_ref[...]] * nb, axis=0)
    lane = lax.broadcasted_iota(jnp.int32, (1, LANES), 1)
    second_half = (lane & N_FREQ) != 0
    tail = lane >= HEAD_DIM

    def rotated(w_ref, gain_ref, out_ref, scale):
        t_all = jnp.dot(n, w_ref[...], preferred_element_type=F32)
        if qk_norm:
            mean_sq = _seg_sum(t_all * t_all, ones_ref, 2) * (1.0 / HEAD_DIM)
            t_all = t_all * lax.rsqrt(mean_sq + NORM_EPS) * gain_ref[...]
        for j in range(t_all.shape[1] // LANES):
            t = t_all[:, j * LANES:(j + 1) * LANES]
            partner = jnp.where(second_half, pltpu.roll(t, N_FREQ, 1), pltpu.roll(t, LANES - N_FREQ, 1))
            out = t * cos + partner * sin
            out = (out * scale if scale != 1.0 else out).astype(BF16)
            for q in range(nb):
                out_ref[q, 2 * j] = out[q * TILE:(q + 1) * TILE, 0:HEAD_DIM]
                out_ref[q, 2 * j + 1] = out[q * TILE:(q + 1) * TILE, HEAD_DIM:LANES]

    rotated(wq_ref, gq_ref, q_out, q_scale)
    rotated(wk_ref, gk_ref, k_out, 1.0)
    v_all = jnp.dot(n, wv_ref[...], preferred_element_type=F32)
    for j in range(v_all.shape[1] // LANES):
        v2 = v_all[:, j * LANES:(j + 1) * LANES]
        first = jnp.where(tail, 1.0, v2).astype(BF16)
        second = jnp.where(tail, 1.0, pltpu.roll(v2, HEAD_DIM, 1)).astype(BF16)
        for q in range(nb):
            v_out[q, 2 * j] = first[q * TILE:(q + 1) * TILE]
            v_out[q, 2 * j + 1] = second[q * TILE:(q + 1) * TILE]


def _qkv_project(st, x, mod, g, a, rope, seg, qk_norm):
    nh, nkv = a["wq"].shape[1] // HEAD_DIM, a["wk"].shape[1] // HEAD_DIM
    consts = [a["wq"], a["wk"], a["wv"], a["gq"], a["gk"], seg]
    rope_spec = pl.BlockSpec((TILE, LANES), lambda b, t: (t, 0))

    nb = st.seqs

    def head_spec(n, width):
        return pl.BlockSpec((nb, n, TILE, width), lambda b, t: (b, 0, t, 0))

    def head_shape(n, width):
        return jax.ShapeDtypeStruct((st.b, n, st.t, width), BF16)

    return pl.pallas_call(
        functools.partial(_qkv_body, qk_norm=qk_norm, q_scale=HEAD_DIM ** -0.5 * LOG2_E),
        grid=(st.b // nb, st.n_tiles),
        in_specs=[st.tile_spec(seqs=nb), st.mod_spec(nb), _const_spec(g.shape)]
                 + [_const_spec(c.shape) for c in consts] + [rope_spec, rope_spec],
        out_specs=[head_spec(nh, HEAD_DIM), head_spec(nkv, HEAD_DIM), head_spec(nkv, LANES)],
        out_shape=[head_shape(nh, HEAD_DIM), head_shape(nkv, HEAD_DIM), head_shape(nkv, LANES)],
        compiler_params=_cparams("parallel", "parallel"),
        name="qkv_rope",
    )(x, mod, g, *consts, *rope)


def _key_chunks(k0, k1):
    n = -(-(k1 - k0) // MAX_KEYS)
    size = -(-(k1 - k0) // (n * LANES)) * LANES
    return [(a, min(size, k1 - a)) for a in range(k0, k1, size)]


def _attn_body(*refs, ctx_len, n_ctx_tiles, total, windowed):
    if windowed:
        sink_ref, q_ref, k_ref, v_ref, o_ref = refs
    else:
        q_ref, k_ref, v_ref, o_ref = refs
    heads = q_ref.shape[1]
    group = heads // k_ref.shape[1]
    t = pl.program_id(2)
    tail = lax.broadcasted_iota(jnp.int32, (1, LANES), 1) >= HEAD_DIM
    chains = range(heads)
    qs = [q_ref[0, c] for c in chains]

    def scores(k0, size):
        return [_dot_nt(q, k_ref[0, c // group, pl.ds(k0, size), :]) for c, q in zip(chains, qs)]

    def weigh(ps, k0, size):
        return [jnp.dot(p.astype(BF16), v_ref[0, c // group, pl.ds(k0, size), :], preferred_element_type=F32)
                for c, p in zip(chains, ps)]

    def finish(accs):
        invs = [1.0 / jnp.where(tail, acc, 1.0) for acc in accs]
        cols = []
        for c in range(0, heads, 2):
            first = accs[c] * pltpu.roll(invs[c], HEAD_DIM, 1)
            second = pltpu.roll(accs[c + 1], HEAD_DIM, 1) * invs[c + 1]
            cols.append(jnp.where(tail, second, first))
        return jnp.concatenate(cols, axis=1)

    def row_max(xs):
        return [jnp.max(x, axis=-1, keepdims=True) for x in xs]

    if windowed:
        sinks = [jnp.full((TILE, 1), sink_ref[pl.program_id(1) * heads + c], F32) * LOG2_E for c in chains]

        def with_sink(s_parts, spans):
            ms = sinks
            for part in s_parts:
                ms = [jnp.maximum(m, pm) for m, pm in zip(ms, row_max(part))]
            accs = [jnp.where(tail, jnp.exp2(sk - m), 0.0) for sk, m in zip(sinks, ms)]
            for part, (k0, size) in zip(s_parts, spans):
                pvs = weigh([jnp.exp2(s - m) for s, m in zip(part, ms)], k0, size)
                accs = [acc + pv for acc, pv in zip(accs, pvs)]
            return finish(accs)

        def context_queries():
            return with_sink([scores(0, ctx_len)], [(0, ctx_len)])

        def latent_queries():
            k0 = pl.multiple_of(jnp.clip((t - 1) * TILE, 0, total - LOCAL_KEYS), TILE)
            k_pos = k0 + lax.broadcasted_iota(jnp.int32, (TILE, LOCAL_KEYS), 1)
            q_pos = t * TILE + lax.broadcasted_iota(jnp.int32, (TILE, LOCAL_KEYS), 0)
            mask = jnp.logical_and(jnp.abs(k_pos - q_pos) <= WINDOW, k_pos >= ctx_len)
            s_loc = [jnp.where(mask, s, NEG_INF) for s in scores(k0, LOCAL_KEYS)]
            return with_sink([scores(0, ctx_len), s_loc], [(0, ctx_len), (k0, LOCAL_KEYS)])
    else:
        def online(spans):
            ms = accs = None
            for k0, size in spans:
                ss = scores(k0, size)
                m_chunk = row_max(ss)
                if ms is None:
                    ms, accs = m_chunk, weigh([jnp.exp2(s - m) for s, m in zip(ss, m_chunk)], k0, size)
                else:
                    m_new = [jnp.maximum(m, mc) for m, mc in zip(ms, m_chunk)]
                    pvs = weigh([jnp.exp2(s - m) for s, m in zip(ss, m_new)], k0, size)
                    accs = [jnp.exp2(m - mn) * acc + pv for m, mn, acc, pv in zip(ms, m_new, accs, pvs)]
                    ms = m_new
            return finish(accs)

        def context_queries():
            return online([(0, ctx_len)])

        def latent_queries():
            return online(_key_chunks(0, total))

    o_ref[0] = lax.cond(t < n_ctx_tiles, context_queries, latent_queries).astype(BF16)


def _attention(st, q, k, v, sink):
    nh, nkv = q.shape[1], k.shape[1]
    group = nh // nkv
    windowed = sink is not None
    assert st.t >= LOCAL_KEYS
    want = WINDOW_KV_HEADS if windowed else ATTN_KV_HEADS
    kvs = want if nkv % want == 0 else 1
    q_spec = pl.BlockSpec((1, kvs * group, TILE, HEAD_DIM), lambda b, h, t: (b, h, t, 0))
    k_spec = pl.BlockSpec((1, kvs, st.t, HEAD_DIM), lambda b, h, t: (b, h, 0, 0))
    v_spec = pl.BlockSpec((1, kvs, st.t, LANES), lambda b, h, t: (b, h, 0, 0))
    specs, args = [q_spec, k_spec, v_spec], [q, k, v]
    if windowed:
        specs = [pl.BlockSpec(memory_space=pltpu.SMEM)] + specs
        args = [sink] + args
    return pl.pallas_call(
        functools.partial(_attn_body, ctx_len=st.ctx, n_ctx_tiles=st.n_ctx_tiles, total=st.t,
                          windowed=windowed),
        grid=(st.b, nkv // kvs, st.n_tiles),
        in_specs=specs,
        out_specs=pl.BlockSpec((1, TILE, kvs * group * HEAD_DIM), lambda b, h, t: (b, t, h)),
        out_shape=jax.ShapeDtypeStruct((st.b, st.t, nh * HEAD_DIM), BF16),
        compiler_params=_cparams("parallel", "parallel", "parallel"),
        name="window_attention" if windowed else "global_attention",
    )(*args)


def _pad_to(w, axis, mult):
    pad = (-w.shape[axis]) % mult
    if pad == 0:
        return w
    widths = [(0, 0)] * w.ndim
    widths[axis] = (0, pad)
    return jnp.pad(w, widths)


def _rope_tables(st):
    rows = (st.t - st.ctx) // GRID_W
    row = jnp.broadcast_to(jnp.arange(rows)[:, None], (rows, GRID_W)).reshape(-1)
    col = jnp.broadcast_to(jnp.arange(GRID_W)[None, :], (rows, GRID_W)).reshape(-1)
    inv_freq = ROPE_THETA ** (-jnp.arange(N_FREQ, dtype=F32) / N_FREQ)
    ang = jnp.stack([row, col], axis=-1).astype(F32)[:, :, None] * inv_freq
    cos, sin = jnp.cos(ang), jnp.sin(ang)
    cos_t = jnp.stack([cos, cos], axis=2).reshape(-1, HEAD_DIM)
    sin_t = jnp.stack([-sin, sin], axis=2).reshape(-1, HEAD_DIM)
    cos_t = jnp.concatenate([jnp.ones((st.ctx, HEAD_DIM), F32), cos_t], axis=0)
    sin_t = jnp.concatenate([jnp.zeros((st.ctx, HEAD_DIM), F32), sin_t], axis=0)
    return jnp.tile(cos_t, (1, LANES // HEAD_DIM)), jnp.tile(sin_t, (1, LANES // HEAD_DIM))


def _attn_params(wq, wk, wv, wo, gq, gk):
    nh, nkv = wq.shape[1] // HEAD_DIM, wk.shape[1] // HEAD_DIM
    return {"wq": wq.astype(BF16), "wk": wk.astype(BF16), "wv": wv.astype(BF16), "wo": wo.astype(BF16),
            "gq": jnp.tile(gq, nh)[None], "gk": jnp.tile(gk, nkv)[None]}


def _segment_ones():
    head = np.arange(SEG_TILE) // RWKV_HEAD
    return jnp.asarray(head[:, None] == head[None, :], dtype=BF16)


def kernel(x, c, ctx, c_ctx, mod_w, mod_b, norm_g, ffn_w1, ffn_w3, ffn_w2, rwkv_mu, rwkv_wr, rwkv_wk, rwkv_wv, rwkv_wo, rwkv_k_k, rwkv_k_a, rwkv_r_k, rwkv_w0, rwkv_w1, rwkv_w2, rwkv_a0, rwkv_a1, rwkv_a2, rwkv_g1, rwkv_g2, rwkv_ln_w, rwkv_ln_b, rwkv_v0, rwkv_v1, rwkv_v2, gattn_wq, gattn_wk, gattn_wv, gattn_wo, gattn_q_norm, gattn_k_norm, wattn_wq, wattn_wk, wattn_wv, wattn_wo, wattn_sink):
    batch, seq, d = x.shape
    depth = mod_w.shape[0]
    st = _Stream(batch, ctx.shape[1], seq, d)

    ctx_rows = max(ROW_SEQS, FFN_SEQS)
    rows = -(-(batch + ctx_rows) // SUBLANES) * SUBLANES
    c_rows = jnp.zeros((rows, d), F32).at[:batch].set(c).at[batch:batch + ctx_rows].set(c_ctx)
    mods = _modulation(c_rows, mod_w, mod_b).reshape(depth, rows, N_MOD, d)

    ffn_w = (ffn_w1.astype(BF16), ffn_w3.astype(BF16), ffn_w2.astype(BF16))
    rope = _rope_tables(st)
    seg = _segment_ones()
    bf = lambda w: w.astype(BF16)

    h = (ctx, x)
    v_first = None
    for i in range(depth):
        kind, j = i % 3, i // 3
        g, mod = norm_g[i], mods[i]
        h = _ffn_half(st, h, mod, g, ffn_w, i, 0)
        mixer = None
        if kind == 0:
            p = {"mu": rwkv_mu[j], "wr": bf(rwkv_wr[j]), "wk": bf(rwkv_wk[j]), "wv": bf(rwkv_wv[j]),
                 "wo": bf(rwkv_wo[j]), "k_k": rwkv_k_k[j][None], "k_a": rwkv_k_a[j][None],
                 "r_k": rwkv_r_k[j].reshape(1, d),
                 "w0": rwkv_w0[j][:, None], "w1": bf(_pad_to(rwkv_w1[j], 2, LANES)), "w2": bf(_pad_to(rwkv_w2[j], 1, LANES)),
                 "a0": rwkv_a0[j][:, None], "a1": bf(_pad_to(rwkv_a1[j], 2, LANES)), "a2": bf(_pad_to(rwkv_a2[j], 1, LANES)),
                 "g1": bf(_pad_to(rwkv_g1[j], 2, LANES)), "g2": bf(_pad_to(rwkv_g2[j], 1, LANES)),
                 "ln_w": rwkv_ln_w[j][:, None], "ln_b": rwkv_ln_b[j][:, None]}
            vres = None
            if j > 0:
                vres = (v_first, rwkv_v0[j - 1][None], bf(_pad_to(rwkv_v1[j - 1], 1, LANES)),
                        bf(_pad_to(rwkv_v2[j - 1], 0, LANES)))
            r, v, kk, lw, kd, ab, gate = _rwkv_features(st, h, mod, g, p, seg, vres)
            if j == 0:
                v_first = v
            y_f = _rwkv_scan(st, r, lw, kd, v, kk, ab, 0)
            y_b = _rwkv_scan(st, r, lw, kd, v, kk, ab, 1)
            h = _rwkv_out(st, h, mod, g, y_f, y_b, r, kd, v, gate, p, seg)
        else:
            if kind == 1:
                a = _attn_params(gattn_wq[j], gattn_wk[j], gattn_wv[j], gattn_wo[j],
                                 gattn_q_norm[j], gattn_k_norm[j])
                sink = None
            else:
                ones = jnp.ones((HEAD_DIM,), F32)
                a = _attn_params(wattn_wq[j], wattn_wk[j], wattn_wv[j], wattn_wo[j], ones, ones)
                sink = wattn_sink[j]
            q, k, v = _qkv_project(st, h, mod, g, a, rope, seg, qk_norm=kind == 1)
            mixer = (_attention(st, q, k, v, sink), a["wo"])
        h = _ffn_half(st, h, mod, g, ffn_w, i, 1, latent_only=i == depth - 1, mixer=mixer)
    return h
```

```python
import functools

import numpy as np
import jax
import jax.numpy as jnp
from jax import lax
from jax.experimental import pallas as pl
from jax.experimental.pallas import tpu as pltpu

F32 = jnp.float32
BF16 = jnp.bfloat16

GRID_W = 64
N_MOD = 9
HEAD_DIM = 64
WINDOW = 128
ROPE_THETA = 10000.0
N_FREQ = HEAD_DIM // 4
RWKV_HEAD = 64
NORM_EPS = 1e-6
GN_EPS = 64e-5
L2_EPS = 1e-12
NEG_INF = -1e30
LOG2_E = 1.4426950408889634
EXP_NEG_HALF = 0.6065306597126334

LANES = 128
SUBLANES = 8
TILE = 256
ROW_SEQS = 2
FFN_SEQS = 2
FFN_CHAIN_SEQS = 1
SCAN_SEQS = 4
ATTN_KV_HEADS = 2
WINDOW_KV_HEADS = 4
CHUNK = 64
MAX_KEYS = 4352
SEG_TILE = 256
LOCAL_KEYS = 3 * TILE
VMEM_LIMIT = 56 * 1024 * 1024

assert RWKV_HEAD == CHUNK and 2 * CHUNK == LANES and HEAD_DIM == RWKV_HEAD and SEG_TILE % RWKV_HEAD == 0
CHUNK_SHIFT = CHUNK.bit_length() - 1


def _cparams(*sem):
    return pltpu.CompilerParams(dimension_semantics=sem, vmem_limit_bytes=VMEM_LIMIT)


def _const_spec(shape):
    nd = len(shape)
    return pl.BlockSpec(shape, lambda *_: (0,) * nd, pipeline_mode=pl.Buffered(1))


def _dot(a, b):
    return jnp.dot(a.astype(BF16), b.astype(BF16), preferred_element_type=F32)


def _dot_nt(a, b):
    return lax.dot_general(a.astype(BF16), b.astype(BF16), (((1,), (1,)), ((), ())),
                           preferred_element_type=F32)


def _dot_tn(a, b):
    return lax.dot_general(a.astype(BF16), b.astype(BF16), (((0,), (0,)), ((), ())),
                           preferred_element_type=F32)


def _sigmoid(x):
    return 1.0 / (1.0 + jnp.exp2(x * -LOG2_E))


def _rms(x, g):
    return x * lax.rsqrt(jnp.mean(x * x, axis=-1, keepdims=True) + NORM_EPS) * g


def _norm_mod(x, g, scale, shift):
    return _rms(x, g) * (1.0 + scale) + shift


def _seg_sum(x, ones_ref, pieces):
    ones = ones_ref[...]
    parts, rest = [], x
    for i in range(pieces):
        part = rest.astype(BF16)
        parts.append(part)
        if i + 1 < pieces:
            rest = rest - part.astype(F32)
    cols = []
    for j in range(x.shape[1] // SEG_TILE):
        sl = slice(j * SEG_TILE, (j + 1) * SEG_TILE)
        cols.append(sum(jnp.dot(part[:, sl], ones, preferred_element_type=F32) for part in parts))
    return jnp.concatenate(cols, axis=1)


def _mod_body(c_ref, w_ref, b_ref, o_ref):
    c = c_ref[...]
    o_ref[0] = _dot(c * _sigmoid(c), w_ref[0]) + b_ref[0]


def _modulation(c_rows, mod_w, mod_b):
    depth, d, nd = mod_w.shape
    rows = c_rows.shape[0]
    tn = nd // 6
    return pl.pallas_call(
        _mod_body,
        grid=(depth, nd // tn),
        in_specs=[_const_spec((rows, d)),
                  pl.BlockSpec((1, d, tn), lambda i, j: (i, 0, j)),
                  pl.BlockSpec((1, 1, tn), lambda i, j: (i, 0, j))],
        out_specs=pl.BlockSpec((1, rows, tn), lambda i, j: (i, 0, j)),
        out_shape=jax.ShapeDtypeStruct((depth, rows, nd), F32),
        compiler_params=_cparams("arbitrary", "arbitrary"),
        name="adaln_rows",
    )(c_rows, mod_w, mod_b.reshape(depth, 1, nd))


class _Stream:
    def __init__(self, batch, ctx_len, seq, d):
        assert ctx_len % TILE == 0 and seq % TILE == 0 and d % LANES == 0
        self.b, self.t, self.d = batch, ctx_len + seq, d
        self.ctx = ctx_len
        self.n_tiles = self.t // TILE
        self.n_ctx_tiles = ctx_len // TILE

        self.seqs = ROW_SEQS if batch % ROW_SEQS == 0 else 1

    def tile_spec(self, width=None, seqs=1):
        return pl.BlockSpec((seqs, TILE, width or self.d), lambda b, t: (b, t, 0))

    def mod_spec(self, seqs=1):
        nct, ctx_blk = self.n_ctx_tiles, self.b // seqs
        return pl.BlockSpec((seqs, N_MOD, self.d), lambda b, t: (jnp.where(t < nct, ctx_blk, b), 0, 0))

    def out_shape(self, width=None, dtype=F32):
        return jax.ShapeDtypeStruct((self.b, self.t, width or self.d), dtype)


def _ffn_body(*refs, mod_row, g_row, with_mixer, n_ctx_tiles, split_input, qkv):
    refs = list(refs)
    if qkv is not None:
        refs, qkv_outs = refs[:-3], refs[-3:]
    if split_input:
        ctx_ref, lat_ref = refs[:2]
        x = jnp.where(pl.program_id(1) < n_ctx_tiles, ctx_ref[...], lat_ref[...])
        refs = refs[2:]
    else:
        x = refs.pop(0)[...]
    out_ref = refs.pop()
    if qkv is not None:
        refs, qkv_refs = refs[:-8], refs[-8:]
    if with_mixer:
        mod_ref, g_ref, w1_ref, w3_ref, w2_ref, o_ref, wo_ref = refs
    else:
        mod_ref, g_ref, w1_ref, w3_ref, w2_ref = refs
    m = mod_ref[...]
    g = g_ref[...]
    d = x.shape[-1]
    per = min(FFN_CHAIN_SEQS, x.shape[0])
    chains = [slice(q, q + per) for q in range(0, x.shape[0], per)]
    xs, ms = [x[c] for c in chains], [m[c] for c in chains]
    if with_mixer:
        mixed = [jnp.dot(o_ref[c].reshape(-1, o_ref.shape[-1]), wo_ref[...], preferred_element_type=F32)
                 for c in chains]
        xs = [xc + mc[:, 5:6] * _rms(mx.reshape(xc.shape), g[3:4]) for xc, mc, mx in zip(xs, ms, mixed)]
    ns = [_norm_mod(xc, g[g_row:g_row + 1], mc[:, mod_row + 1:mod_row + 2], mc[:, mod_row:mod_row + 1])
          .astype(BF16).reshape(-1, d) for xc, mc in zip(xs, ms)]
    a = [jnp.dot(n, w1_ref[0, 0], preferred_element_type=F32) for n in ns]
    b = [jnp.dot(n, w3_ref[0, 0], preferred_element_type=F32) for n in ns]
    hs = [(ac * _sigmoid(ac) * bc).astype(BF16) for ac, bc in zip(a, b)]
    ys = [jnp.dot(h, w2_ref[0, 0], preferred_element_type=F32) for h in hs]
    outs = [xc + 0.5 * mc[:, mod_row + 2:mod_row + 3] * _rms(y.reshape(xc.shape), g[g_row + 1:g_row + 2])
            for xc, mc, y in zip(xs, ms, ys)]
    for c, out in zip(chains, outs):
        out_ref[c] = out
    if qkv is not None:
        for c, out, mc in zip(chains, outs, ms):
            n_mix = _norm_mod(out, g[2:3], mc[:, 4:5], mc[:, 3:4]).astype(BF16).reshape(-1, d)
            _emit_qkv(n_mix, c.start, qkv_refs, qkv_outs, **qkv)


def _ffn_half(st, x, mod, g, weights, layer, half, latent_only=False, mixer=None, qkv=None):
    w1, w3, w2 = weights
    d, f = w1.shape[-2:]
    nb = FFN_SEQS if st.b % FFN_SEQS == 0 else st.seqs
    nct, ctx_blk = st.n_ctx_tiles, st.b // nb
    skip = nct if latent_only else 0
    split_input = isinstance(x, tuple)

    def rows_spec(width):
        return pl.BlockSpec((nb, TILE, width), lambda b, t: (b, t + skip, 0))

    def weight_spec(w):
        return pl.BlockSpec((1, 1) + w.shape[-2:], lambda b, t: (layer, half, 0, 0), pipeline_mode=pl.Buffered(1))

    if split_input:
        specs = [pl.BlockSpec((nb, TILE, d), lambda b, t: (b, jnp.minimum(t, nct - 1), 0)),
                 pl.BlockSpec((nb, TILE, d), lambda b, t: (b, jnp.maximum(t - nct, 0), 0))]
        args = list(x)
    else:
        specs, args = [rows_spec(d)], [x]
    mod_spec = pl.BlockSpec((nb, N_MOD, d), lambda b, t: (jnp.where(t + skip < nct, ctx_blk, b), 0, 0))
    specs += [mod_spec, _const_spec(g.shape), weight_spec(w1), weight_spec(w3), weight_spec(w2)]
    args += [mod, g, w1, w3, w2]
    if mixer is not None:
        o, wo = mixer
        specs += [rows_spec(o.shape[-1]), _const_spec(wo.shape)]
        args += [o, wo]
    out_specs = [pl.BlockSpec((nb, TILE, d), lambda b, t: (b, t, 0))]
    out_shape = [jax.ShapeDtypeStruct((st.b, st.t - skip * TILE, d), F32)]
    qkv_static = None
    if qkv is not None:
        assert not latent_only
        a, rope, seg, qk_norm = qkv
        consts = [a["wq"], a["wk"], a["wv"], a["gq"], a["gk"], seg]
        rope_spec = pl.BlockSpec((TILE, LANES), lambda b, t: (t, 0))
        specs += [_const_spec(c.shape) for c in consts] + [rope_spec, rope_spec]
        args += consts + list(rope)
        for n_heads, width in ((a["wq"].shape[1] // HEAD_DIM, HEAD_DIM), (a["wk"].shape[1] // HEAD_DIM, HEAD_DIM),
                               (a["wv"].shape[1] // HEAD_DIM, LANES)):
            out_specs.append(pl.BlockSpec((nb, n_heads, TILE, width), lambda b, t: (b, 0, t, 0)))
            out_shape.append(jax.ShapeDtypeStruct((st.b, n_heads, st.t, width), BF16))
        qkv_static = {"qk_norm": qk_norm, "q_scale": HEAD_DIM ** -0.5 * LOG2_E}
    res = pl.pallas_call(
        functools.partial(_ffn_body, mod_row=6 * half, g_row=4 * half, with_mixer=mixer is not None,
                          n_ctx_tiles=nct, split_input=split_input, qkv=qkv_static),
        grid=(st.b // nb, st.n_tiles - skip),
        in_specs=specs,
        out_specs=out_specs,
        out_shape=out_shape,
        compiler_params=_cparams("parallel", "parallel"),
        name="swiglu_half",
    )(*args)
    return res if qkv is not None else res[0]


def _rwkv_feat_body(*refs, n_ctx_tiles, n_tiles, with_vres):
    (x_ref, xp_ref, xn_ref, mod_ref, g_ref, mu_ref, wr_ref, wk_ref, wv_ref, kk_ref, ka_ref,
     w0_ref, w1_ref, w2_ref, a0_ref, a1_ref, a2_ref, g1_ref, g2_ref, ones_ref) = refs[:20]
    rest = refs[20:]
    if with_vres:
        vf_ref, v0_ref, v1_ref, v2_ref = rest[:4]
        rest = rest[4:]
    r_out, v_out, kk_out, lw_out, kd_out, ab_out, g_out = rest

    t = pl.program_id(1)
    m = mod_ref[0]
    gn = g_ref[...][2:3]
    shift, scale = m[3:4], m[4:5]
    n = _norm_mod(x_ref[0], gn, scale, shift)
    first = jnp.logical_or(t == 0, t == n_ctx_tiles)
    last = jnp.logical_or(t == n_ctx_tiles - 1, t == n_tiles - 1)
    n_before = jnp.where(first, 0.0, _norm_mod(xp_ref[0][7:8], gn, scale, shift))
    n_after = jnp.where(last, 0.0, _norm_mod(xn_ref[0][0:1], gn, scale, shift))
    row = lax.broadcasted_iota(jnp.int32, (TILE, 1), 0)
    n_prev = jnp.where(row == 0, n_before, pltpu.roll(n, 1, 0))
    n_next = jnp.where(row == TILE - 1, n_after, pltpu.roll(n, TILE - 1, 0))
    xx = 0.5 * (n_prev + n_next) - n
    mu = mu_ref[...]
    xr, xw, xk, xv, xa, xg = (n + xx * mu[i:i + 1] for i in range(6))

    r_out[0] = _dot(xr, wr_ref[...]).astype(BF16)
    v = _dot(xv, wv_ref[...])
    if with_vres:
        lora = _dot(_dot(xv, v1_ref[...]), v2_ref[...])
        v = v + (vf_ref[0] - v) * _sigmoid(v0_ref[...] + lora)
    v_out[0] = v
    k = _dot(xk, wk_ref[...])
    kk = k * kk_ref[...]
    kk = kk / jnp.maximum(jnp.sqrt(_seg_sum(kk * kk, ones_ref, 2)), L2_EPS)
    kk_out[0] = kk
    k_ka = k * ka_ref[...]
    k_rest = k - k_ka
    for dr in range(2):
        wl = w0_ref[dr] + _dot(jnp.tanh(_dot(xw, w1_ref[dr])), w2_ref[dr])
        a = _sigmoid(a0_ref[dr] + _dot(_dot(xa, a1_ref[dr]), a2_ref[dr]))
        lw_out[dr, 0] = -EXP_NEG_HALF * _sigmoid(wl)
        kd_out[dr, 0] = (k_rest + k_ka * a).astype(BF16)
        ab_out[dr, 0] = kk * a
        g_out[dr, 0] = _dot(_sigmoid(_dot(xg, g1_ref[dr])), g2_ref[dr]).astype(BF16)


def _rwkv_features(st, x, mod, g, p, seg, vres):
    d = st.d
    n8 = st.t // 8
    per8 = TILE // 8
    prev_spec = pl.BlockSpec((1, 8, d), lambda b, t: (b, jnp.maximum(t * per8 - 1, 0), 0))
    next_spec = pl.BlockSpec((1, 8, d), lambda b, t: (b, jnp.minimum((t + 1) * per8, n8 - 1), 0))
    dir_spec = pl.BlockSpec((2, 1, TILE, d), lambda b, t: (0, b, t, 0))
    dir_f32 = jax.ShapeDtypeStruct((2, st.b, st.t, d), F32)
    dir_bf16 = jax.ShapeDtypeStruct((2, st.b, st.t, d), BF16)
    names = ("mu", "wr", "wk", "wv", "k_k", "k_a", "w0", "w1", "w2", "a0", "a1", "a2", "g1", "g2")
    consts = [p[nm] for nm in names] + [seg]
    args = [x, x, x, mod, g] + consts
    specs = [st.tile_spec(), prev_spec, next_spec, st.mod_spec(), _const_spec(g.shape)]
    specs += [_const_spec(c.shape) for c in consts]
    if vres is not None:
        v_first, v0, v1, v2 = vres
        args += [v_first, v0, v1, v2]
        specs += [st.tile_spec(), _const_spec(v0.shape), _const_spec(v1.shape), _const_spec(v2.shape)]
    return pl.pallas_call(
        functools.partial(_rwkv_feat_body, n_ctx_tiles=st.n_ctx_tiles, n_tiles=st.n_tiles,
                          with_vres=vres is not None),
        grid=(st.b, st.n_tiles),
        in_specs=specs,
        out_specs=[st.tile_spec()] * 3 + [dir_spec] * 4,
        out_shape=[st.out_shape(dtype=BF16), st.out_shape(), st.out_shape(), dir_f32, dir_bf16, dir_f32, dir_bf16],
        compiler_params=_cparams("parallel", "parallel"),
        name="rwkv_features",
    )(*args)


def _unit_triangular_inverses(mats, ri, ci, size, upper):
    n = mats[0].shape[0]
    eye = (ri == ci).astype(F32)
    ts = [jnp.where((ri >> 1) == (ci >> 1), a, 0.0) + eye for a in mats]
    level = 1
    while (2 << level) <= size:
        m = 1 << level
        off = jnp.logical_and((ri >> (level + 1)) == (ci >> (level + 1)), (ri >> level) != (ci >> level))
        a_off = [jnp.where(off, a, 0.0) for a in mats]
        if m % SUBLANES == 0:
            moving = range(0 if upper else 1, n // m, 2)

            def rows(x):
                return jnp.concatenate([x[b * m:(b + 1) * m] for b in moving], axis=0)

            zs = [_dot(_dot(rows(t), a), t) for t, a in zip(ts, a_off)]
            ts = [jnp.concatenate([t[b * m:(b + 1) * m] + z[(b // 2) * m:(b // 2 + 1) * m] if b in moving
                                   else t[b * m:(b + 1) * m] for b in range(n // m)], axis=0)
                  for t, z in zip(ts, zs)]
        else:
            ts = [t + _dot(_dot(t, a), t) for t, a in zip(ts, a_off)]
        level += 1
    return ts


def _scan_body(r_ref, lw_ref, k_ref, v_ref, kk_ref, ab_ref, y_ref, s_ref, *, reverse):
    c = CHUNK
    n2 = 2 * c

    @pl.when(pl.program_id(1) == 0)
    def _():
        s_ref[...] = jnp.zeros_like(s_ref)

    row = lax.broadcasted_iota(jnp.int32, (c, 1), 0)

    def running_sum(x):
        step = 1
        while step < c:
            if reverse:
                x = x + jnp.where(row < c - step, pltpu.roll(x, c - step, 0), 0.0)
            else:
                x = x + jnp.where(row >= step, pltpu.roll(x, step, 0), 0.0)
            step *= 2
        return x

    ri = lax.broadcasted_iota(jnp.int32, (n2, n2), 0)
    ci = lax.broadcasted_iota(jnp.int32, (n2, n2), 1)
    same = (ri >> CHUNK_SHIFT) == (ci >> CHUNK_SHIFT)
    tt, ss = ri & (c - 1), ci & (c - 1)
    before = (ss > tt) if reverse else (ss < tt)
    strict = jnp.logical_and(same, before)
    incl = jnp.logical_and(same, jnp.logical_or(before, ss == tt))
    incl2 = jnp.concatenate([incl, incl], axis=1)
    even = lax.broadcasted_iota(jnp.int32, (1, LANES), 1) < RWKV_HEAD

    def stack(x):
        return jnp.concatenate([jnp.where(even, x, 0.0), jnp.where(even, 0.0, x)], axis=0)

    n_pairs = r_ref.shape[-1] // LANES
    chains = [(q, p) for q in range(r_ref.shape[0]) for p in range(n_pairs)]
    cols = {}
    for q in range(r_ref.shape[0]):
        r, k = r_ref[q].astype(F32), k_ref[0, q].astype(F32)
        lw, v, kk, ab = lw_ref[0, q], v_ref[q], kk_ref[q], ab_ref[0, q]
        cum = running_sum(lw)
        tot = cum[0:1] if reverse else cum[c - 1:c]
        inv = jnp.exp(-cum)
        to_end = jnp.exp(tot - cum)
        full = {"a": -kk * jnp.exp(cum - lw), "r": r * jnp.exp(cum), "b": ab * inv, "k": k * inv,
                "be": ab * to_end, "ke": k * to_end, "w": jnp.exp(tot), "v": v}
        for p in range(n_pairs):
            cols[q, p] = {nm: x[:, p * LANES:(p + 1) * LANES] for nm, x in full.items()}

    prods = [_dot_nt(jnp.concatenate([stack(cols[ch]["a"]), stack(cols[ch]["r"])], axis=0),
                     jnp.concatenate([stack(cols[ch]["b"]), stack(cols[ch]["k"])], axis=0)) for ch in chains]
    a_b = [jnp.where(strict, pr[0:n2, 0:n2], 0.0) for pr in prods]
    t_inv = _unit_triangular_inverses(a_b, ri, ci, c, reverse)
    v_st = [stack(cols[ch]["v"]) for ch in chains]
    u_own = [_dot(jnp.where(strict, pr[0:n2, n2:2 * n2], 0.0), vs) for pr, vs in zip(prods, v_st)]
    from_state = [_dot_nt(jnp.concatenate([cols[ch]["a"], cols[ch]["r"]], axis=0), s_ref[ch])
                  for ch in chains]
    u_st = [_dot(t, stack(fs[0:c]) + uo) for t, fs, uo in zip(t_inv, from_state, u_own)]
    y_st = [_dot(jnp.where(incl2, pr[n2:2 * n2], 0.0), jnp.concatenate([us, vs], axis=0))
            for pr, us, vs in zip(prods, u_st, v_st)]
    for i, (q, p) in enumerate(chains):
        y_ref[q, :, p * LANES:(p + 1) * LANES] = (from_state[i][c:2 * c] + y_st[i][0:c] + y_st[i][c:n2]).astype(BF16)
        u = u_st[i][0:c] + u_st[i][c:n2]
        upd = _dot_tn(jnp.concatenate([u, cols[q, p]["v"]], axis=0),
                      jnp.concatenate([cols[q, p]["be"], cols[q, p]["ke"]], axis=0))
        s_ref[q, p] = s_ref[q, p] * cols[q, p]["w"] + jnp.where(same, upd, 0.0)


def _rwkv_scan(st, r, lw, kd, v, kk, ab, direction):
    d = st.d
    nq = SCAN_SEQS if st.b % SCAN_SEQS == 0 else 1
    nc, ncc = st.t // CHUNK, st.ctx // CHUNK
    if direction == 0:
        def cidx(s):
            return s
    else:
        def cidx(s):
            return jnp.where(s < ncc, ncc - 1 - s, nc - 1 - s + ncc)
    shared = pl.BlockSpec((nq, CHUNK, d), lambda b, s: (b, cidx(s), 0))
    per_dir = pl.BlockSpec((1, nq, CHUNK, d), lambda b, s: (direction, b, cidx(s), 0))
    return pl.pallas_call(
        functools.partial(_scan_body, reverse=direction == 1),
        grid=(st.b // nq, nc),
        in_specs=[shared, per_dir, per_dir, shared, shared, per_dir],
        out_specs=shared,
        out_shape=st.out_shape(dtype=BF16),
        scratch_shapes=[pltpu.VMEM((nq, d // LANES, LANES, LANES), F32)],
        compiler_params=_cparams("parallel", "arbitrary"),
        name="rwkv_scan_bwd" if direction else "rwkv_scan_fwd",
    )(r, lw, kd, v, kk, ab)


def _rwkv_out_body(x_ref, mod_ref, g_ref, yf_ref, yb_ref, r_ref, kd_ref, v_ref, gate_ref,
                   lnw_ref, lnb_ref, rk_ref, wo_ref, ones_ref, o_ref):
    x = x_ref[...]
    flat = (-1, x.shape[-1])
    r, v = r_ref[...].astype(F32).reshape(flat), v_ref[...].reshape(flat)
    inv_n = 1.0 / RWKV_HEAD
    o = None
    for dr, y_ref in enumerate((yf_ref, yb_ref)):
        y = y_ref[...].astype(F32).reshape(flat)
        cen = y - _seg_sum(y, ones_ref, 1) * inv_n
        var = _seg_sum(cen * cen, ones_ref, 1) * inv_n
        yn = cen * lax.rsqrt(var + GN_EPS) * lnw_ref[dr] + lnb_ref[dr]
        bonus = _seg_sum(r * kd_ref[dr].astype(F32).reshape(flat) * rk_ref[...], ones_ref, 1) * v
        od = (yn + bonus) * gate_ref[dr].astype(F32).reshape(flat)
        o = od if o is None else o + od
    y_mix = _dot(o, wo_ref[...]).reshape(x.shape)
    o_ref[...] = x + mod_ref[...][:, 5:6] * _rms(y_mix, g_ref[...][3:4])


def _rwkv_out(st, x, mod, g, y_f, y_b, r, kd, v, gate, p, seg):
    d, nb = st.d, st.seqs
    rows = st.tile_spec(seqs=nb)
    dir_spec = pl.BlockSpec((2, nb, TILE, d), lambda b, t: (0, b, t, 0))
    consts = [p["ln_w"], p["ln_b"], p["r_k"], p["wo"]] + [seg]
    return pl.pallas_call(
        _rwkv_out_body,
        grid=(st.b // nb, st.n_tiles),
        in_specs=[rows, st.mod_spec(nb), _const_spec(g.shape), rows, rows, rows, dir_spec, rows, dir_spec]
                 + [_const_spec(c.shape) for c in consts],
        out_specs=rows,
        out_shape=st.out_shape(),
        compiler_params=_cparams("parallel", "parallel"),
        name="rwkv_readout",
    )(x, mod, g, y_f, y_b, r, kd, v, gate, *consts)


def _emit_qkv(n, seq0, qkv_refs, outs, *, qk_norm, q_scale):
    wq_ref, wk_ref, wv_ref, gq_ref, gk_ref, ones_ref, cos_ref, sin_ref = qkv_refs
    q_out, k_out, v_out = outs
    nb = n.shape[0] // TILE
    cos = jnp.concatenate([cos_ref[...]] * nb, axis=0)
    sin = jnp.concatenate([sin_ref[...]] * nb, axis=0)
    lane = lax.broadcasted_iota(jnp.int32, (1, LANES), 1)
    second_half = (lane & N_FREQ) != 0
    tail = lane >= HEAD_DIM

    def rotated(w_ref, gain_ref, out_ref, scale):
        t_all = jnp.dot(n, w_ref[...], preferred_element_type=F32)
        if qk_norm:
            mean_sq = _seg_sum(t_all * t_all, ones_ref, 2) * (1.0 / HEAD_DIM)
            t_all = t_all * lax.rsqrt(mean_sq + NORM_EPS) * gain_ref[...]
        for j in range(t_all.shape[1] // LANES):
            t = t_all[:, j * LANES:(j + 1) * LANES]
            partner = jnp.where(second_half, pltpu.roll(t, N_FREQ, 1), pltpu.roll(t, LANES - N_FREQ, 1))
            out = t * cos + partner * sin
            out = (out * scale if scale != 1.0 else out).astype(BF16)
            for q in range(nb):
                out_ref[seq0 + q, 2 * j] = out[q * TILE:(q + 1) * TILE, 0:HEAD_DIM]
                out_ref[seq0 + q, 2 * j + 1] = out[q * TILE:(q + 1) * TILE, HEAD_DIM:LANES]

    rotated(wq_ref, gq_ref, q_out, q_scale)
    rotated(wk_ref, gk_ref, k_out, 1.0)
    v_all = jnp.dot(n, wv_ref[...], preferred_element_type=F32)
    for j in range(v_all.shape[1] // LANES):
        v2 = v_all[:, j * LANES:(j + 1) * LANES]
        first = jnp.where(tail, 1.0, v2).astype(BF16)
        second = jnp.where(tail, 1.0, pltpu.roll(v2, HEAD_DIM, 1)).astype(BF16)
        for q in range(nb):
            v_out[seq0 + q, 2 * j] = first[q * TILE:(q + 1) * TILE]
            v_out[seq0 + q, 2 * j + 1] = second[q * TILE:(q + 1) * TILE]


def _key_chunks(k0, k1):
    n = -(-(k1 - k0) // MAX_KEYS)
    size = -(-(k1 - k0) // (n * LANES)) * LANES
    return [(a, min(size, k1 - a)) for a in range(k0, k1, size)]


def _attn_body(*refs, ctx_len, n_ctx_tiles, total, windowed):
    if windowed:
        sink_ref, q_ref, k_ref, v_ref, o_ref = refs
    else:
        q_ref, k_ref, v_ref, o_ref = refs
    heads = q_ref.shape[1]
    group = heads // k_ref.shape[1]
    t = pl.program_id(2)
    tail = lax.broadcasted_iota(jnp.int32, (1, LANES), 1) >= HEAD_DIM
    chains = range(heads)
    qs = [q_ref[0, c] for c in chains]

    def scores(k0, size):
        return [_dot_nt(q, k_ref[0, c // group, pl.ds(k0, size), :]) for c, q in zip(chains, qs)]

    def weigh(ps, k0, size):
        return [jnp.dot(p.astype(BF16), v_ref[0, c // group, pl.ds(k0, size), :], preferred_element_type=F32)
                for c, p in zip(chains, ps)]

    def finish(accs):
        invs = [1.0 / jnp.where(tail, acc, 1.0) for acc in accs]
        cols = []
        for c in range(0, heads, 2):
            first = accs[c] * pltpu.roll(invs[c], HEAD_DIM, 1)
            second = pltpu.roll(accs[c + 1], HEAD_DIM, 1) * invs[c + 1]
            cols.append(jnp.where(tail, second, first))
        return jnp.concatenate(cols, axis=1)

    def row_max(xs):
        return [jnp.max(x, axis=-1, keepdims=True) for x in xs]

    if windowed:
        sinks = [jnp.full((TILE, 1), sink_ref[pl.program_id(1) * heads + c], F32) * LOG2_E for c in chains]

        def with_sink(s_parts, spans):
            ms = sinks
            for part in s_parts:
                ms = [jnp.maximum(m, pm) for m, pm in zip(ms, row_max(part))]
            accs = [jnp.where(tail, jnp.exp2(sk - m), 0.0) for sk, m in zip(sinks, ms)]
            for part, (k0, size) in zip(s_parts, spans):
                pvs = weigh([jnp.exp2(s - m) for s, m in zip(part, ms)], k0, size)
                accs = [acc + pv for acc, pv in zip(accs, pvs)]
            return finish(accs)

        def context_queries():
            return with_sink([scores(0, ctx_len)], [(0, ctx_len)])

        def latent_queries():
            k0 = pl.multiple_of(jnp.clip((t - 1) * TILE, 0, total - LOCAL_KEYS), TILE)
            k_pos = k0 + lax.broadcasted_iota(jnp.int32, (TILE, LOCAL_KEYS), 1)
            q_pos = t * TILE + lax.broadcasted_iota(jnp.int32, (TILE, LOCAL_KEYS), 0)
            mask = jnp.logical_and(jnp.abs(k_pos - q_pos) <= WINDOW, k_pos >= ctx_len)
            s_loc = [jnp.where(mask, s, NEG_INF) for s in scores(k0, LOCAL_KEYS)]
            return with_sink([scores(0, ctx_len), s_loc], [(0, ctx_len), (k0, LOCAL_KEYS)])
    else:
        def online(spans):
            ms = accs = None
            for k0, size in spans:
                ss = scores(k0, size)
                m_chunk = row_max(ss)
                if ms is None:
                    ms, accs = m_chunk, weigh([jnp.exp2(s - m) for s, m in zip(ss, m_chunk)], k0, size)
                else:
                    m_new = [jnp.maximum(m, mc) for m, mc in zip(ms, m_chunk)]
                    pvs = weigh([jnp.exp2(s - m) for s, m in zip(ss, m_new)], k0, size)
                    accs = [jnp.exp2(m - mn) * acc + pv for m, mn, acc, pv in zip(ms, m_new, accs, pvs)]
                    ms = m_new
            return finish(accs)

        def context_queries():
            return online([(0, ctx_len)])

        def latent_queries():
            return online(_key_chunks(0, total))

    o_ref[0] = lax.cond(t < n_ctx_tiles, context_queries, latent_queries).astype(BF16)


def _attention(st, q, k, v, sink):
    nh, nkv = q.shape[1], k.shape[1]
    group = nh // nkv
    windowed = sink is not None
    assert st.t >= LOCAL_KEYS
    want = WINDOW_KV_HEADS if windowed else ATTN_KV_HEADS
    kvs = want if nkv % want == 0 else 1
    q_spec = pl.BlockSpec((1, kvs * group, TILE, HEAD_DIM), lambda b, h, t: (b, h, t, 0))
    k_spec = pl.BlockSpec((1, kvs, st.t, HEAD_DIM), lambda b, h, t: (b, h, 0, 0))
    v_spec = pl.BlockSpec((1, kvs, st.t, LANES), lambda b, h, t: (b, h, 0, 0))
    specs, args = [q_spec, k_spec, v_spec], [q, k, v]
    if windowed:
        specs = [pl.BlockSpec(memory_space=pltpu.SMEM)] + specs
        args = [sink] + args
    return pl.pallas_call(
        functools.partial(_attn_body, ctx_len=st.ctx, n_ctx_tiles=st.n_ctx_tiles, total=st.t,
                          windowed=windowed),
        grid=(st.b, nkv // kvs, st.n_tiles),
        in_specs=specs,
        out_specs=pl.BlockSpec((1, TILE, kvs * group * HEAD_DIM), lambda b, h, t: (b, t, h)),
        out_shape=jax.ShapeDtypeStruct((st.b, st.t, nh * HEAD_DIM), BF16),
        compiler_params=_cparams("parallel", "parallel", "parallel"),
        name="window_attention" if windowed else "global_attention",
    )(*args)


def _pad_to(w, axis, mult):
    pad = (-w.shape[axis]) % mult
    if pad == 0:
        return w
    widths = [(0, 0)] * w.ndim
    widths[axis] = (0, pad)
    return jnp.pad(w, widths)


def _rope_tables(st):
    rows = (st.t - st.ctx) // GRID_W
    row = jnp.broadcast_to(jnp.arange(rows)[:, None], (rows, GRID_W)).reshape(-1)
    col = jnp.broadcast_to(jnp.arange(GRID_W)[None, :], (rows, GRID_W)).reshape(-1)
    inv_freq = ROPE_THETA ** (-jnp.arange(N_FREQ, dtype=F32) / N_FREQ)
    ang = jnp.stack([row, col], axis=-1).astype(F32)[:, :, None] * inv_freq
    cos, sin = jnp.cos(ang), jnp.sin(ang)
    cos_t = jnp.stack([cos, cos], axis=2).reshape(-1, HEAD_DIM)
    sin_t = jnp.stack([-sin, sin], axis=2).reshape(-1, HEAD_DIM)
    cos_t = jnp.concatenate([jnp.ones((st.ctx, HEAD_DIM), F32), cos_t], axis=0)
    sin_t = jnp.concatenate([jnp.zeros((st.ctx, HEAD_DIM), F32), sin_t], axis=0)
    return jnp.tile(cos_t, (1, LANES // HEAD_DIM)), jnp.tile(sin_t, (1, LANES // HEAD_DIM))


def _attn_params(wq, wk, wv, wo, gq, gk):
    nh, nkv = wq.shape[1] // HEAD_DIM, wk.shape[1] // HEAD_DIM
    return {"wq": wq.astype(BF16), "wk": wk.astype(BF16), "wv": wv.astype(BF16), "wo": wo.astype(BF16),
            "gq": jnp.tile(gq, nh)[None], "gk": jnp.tile(gk, nkv)[None]}


def _segment_ones():
    head = np.arange(SEG_TILE) // RWKV_HEAD
    return jnp.asarray(head[:, None] == head[None, :], dtype=BF16)


def kernel(x, c, ctx, c_ctx, mod_w, mod_b, norm_g, ffn_w1, ffn_w3, ffn_w2, rwkv_mu, rwkv_wr, rwkv_wk, rwkv_wv, rwkv_wo, rwkv_k_k, rwkv_k_a, rwkv_r_k, rwkv_w0, rwkv_w1, rwkv_w2, rwkv_a0, rwkv_a1, rwkv_a2, rwkv_g1, rwkv_g2, rwkv_ln_w, rwkv_ln_b, rwkv_v0, rwkv_v1, rwkv_v2, gattn_wq, gattn_wk, gattn_wv, gattn_wo, gattn_q_norm, gattn_k_norm, wattn_wq, wattn_wk, wattn_wv, wattn_wo, wattn_sink):
    batch, seq, d = x.shape
    depth = mod_w.shape[0]
    st = _Stream(batch, ctx.shape[1], seq, d)

    ctx_rows = max(ROW_SEQS, FFN_SEQS)
    rows = -(-(batch + ctx_rows) // SUBLANES) * SUBLANES
    c_rows = jnp.zeros((rows, d), F32).at[:batch].set(c).at[batch:batch + ctx_rows].set(c_ctx)
    mods = _modulation(c_rows, mod_w, mod_b).reshape(depth, rows, N_MOD, d)

    ffn_w = (ffn_w1.astype(BF16), ffn_w3.astype(BF16), ffn_w2.astype(BF16))
    rope = _rope_tables(st)
    seg = _segment_ones()
    bf = lambda w: w.astype(BF16)

    h = (ctx, x)
    v_first = None
    for i in range(depth):
        kind, j = i % 3, i // 3
        g, mod = norm_g[i], mods[i]
        mixer = None
        if kind == 0:
            h = _ffn_half(st, h, mod, g, ffn_w, i, 0)
            p = {"mu": rwkv_mu[j], "wr": bf(rwkv_wr[j]), "wk": bf(rwkv_wk[j]), "wv": bf(rwkv_wv[j]),
                 "wo": bf(rwkv_wo[j]), "k_k": rwkv_k_k[j][None], "k_a": rwkv_k_a[j][None],
                 "r_k": rwkv_r_k[j].reshape(1, d),
                 "w0": rwkv_w0[j][:, None], "w1": bf(_pad_to(rwkv_w1[j], 2, LANES)), "w2": bf(_pad_to(rwkv_w2[j], 1, LANES)),
                 "a0": rwkv_a0[j][:, None], "a1": bf(_pad_to(rwkv_a1[j], 2, LANES)), "a2": bf(_pad_to(rwkv_a2[j], 1, LANES)),
                 "g1": bf(_pad_to(rwkv_g1[j], 2, LANES)), "g2": bf(_pad_to(rwkv_g2[j], 1, LANES)),
                 "ln_w": rwkv_ln_w[j][:, None], "ln_b": rwkv_ln_b[j][:, None]}
            vres = None
            if j > 0:
                vres = (v_first, rwkv_v0[j - 1][None], bf(_pad_to(rwkv_v1[j - 1], 1, LANES)),
                        bf(_pad_to(rwkv_v2[j - 1], 0, LANES)))
            r, v, kk, lw, kd, ab, gate = _rwkv_features(st, h, mod, g, p, seg, vres)
            if j == 0:
                v_first = v
            y_f = _rwkv_scan(st, r, lw, kd, v, kk, ab, 0)
            y_b = _rwkv_scan(st, r, lw, kd, v, kk, ab, 1)
            h = _rwkv_out(st, h, mod, g, y_f, y_b, r, kd, v, gate, p, seg)
        else:
            if kind == 1:
                a = _attn_params(gattn_wq[j], gattn_wk[j], gattn_wv[j], gattn_wo[j],
                                 gattn_q_norm[j], gattn_k_norm[j])
                sink = None
            else:
                ones = jnp.ones((HEAD_DIM,), F32)
                a = _attn_params(wattn_wq[j], wattn_wk[j], wattn_wv[j], wattn_wo[j], ones, ones)
                sink = wattn_sink[j]
            h, q, k, v = _ffn_half(st, h, mod, g, ffn_w, i, 0, qkv=(a, rope, seg, kind == 1))
            mixer = (_attention(st, q, k, v, sink), a["wo"])
        h = _ffn_half(st, h, mod, g, ffn_w, i, 1, latent_only=i == depth - 1, mixer=mixer)
    return h
```
